```python
import math
import jax, jax.numpy as jnp
from jax import lax
import numpy as np

D_MODEL = 1024
BATCH = 4
SEQ = 4096
DEPTH = 2

HEAD_DIM = 64
N_GROUPS = D_MODEL // HEAD_DIM
N_MEM_HEADS = 4
N_MIX_HEADS = N_GROUPS - N_MEM_HEADS
D_MIX = N_MIX_HEADS * HEAD_DIM
D_MEM = N_MEM_HEADS * HEAD_DIM
CHUNK = 128
Q_BLOCK = 128
N_MEM_TOKENS = 256
D_FF = 4 * D_MODEL
N_A_LAYERS = DEPTH - DEPTH // 2
N_B_LAYERS = DEPTH // 2
DN_ALPHA = (2 * DEPTH) ** 0.25
DN_BETA = (8 * DEPTH) ** -0.25
LN_EPS = 1e-5

kernel_name = "yoco_gmlp_fox_memory_hybrid"


def layer_norm(x, g, b):
    xf = x.astype(jnp.float32)
    mu = jnp.mean(xf, axis=-1, keepdims=True)
    var = jnp.mean(jnp.square(xf - mu), axis=-1, keepdims=True)
    y = (xf - mu) * lax.rsqrt(var + LN_EPS)
    return (y * g.astype(jnp.float32) + b.astype(jnp.float32)).astype(x.dtype)


def chunked_sgu(u, v, w_s, b_s):
    bsz, s, g, dh = v.shape
    n_chunks = s // CHUNK
    causal = jnp.tril(jnp.ones((CHUNK, CHUNK), dtype=bool))
    w = jnp.where(causal[None], w_s, jnp.zeros_like(w_s))
    vc = v.reshape(bsz, n_chunks, CHUNK, g, dh)
    mixed = jnp.einsum('gts,bnsgd->bntgd', w, vc) + b_s.T[None, None, :, :, None]
    return u * mixed.reshape(bsz, s, g, dh)


def fox_attention(q, k, v, logcum):
    bsz, s, h, dh = q.shape
    n_blk = s // Q_BLOCK
    scale = 1.0 / math.sqrt(dh)
    qb = q.reshape(bsz, n_blk, Q_BLOCK, h, dh).transpose(1, 0, 2, 3, 4)
    cb = logcum.reshape(bsz, h, n_blk, Q_BLOCK).transpose(2, 0, 1, 3)
    kpos = jnp.arange(s)

    def one_block(args):
        qi, ci, i = args
        sc = jnp.einsum('bqhd,bkhd->bhqk', qi, k).astype(jnp.float32) * scale
        sc = sc + ci[..., None] - logcum[:, :, None, :]
        qpos = i * Q_BLOCK + jnp.arange(Q_BLOCK)
        mask = kpos[None, :] <= qpos[:, None]
        sc = jnp.where(mask[None, None], sc, -jnp.inf)
        p = jax.nn.softmax(sc, axis=-1)
        return jnp.einsum('bhqk,bkhd->bqhd', p.astype(v.dtype), v)

    out = lax.map(one_block, (qb, cb, jnp.arange(n_blk)))
    return out.transpose(1, 0, 2, 3, 4).reshape(bsz, s, h, dh)


def memory_attention(qm, mk, mv):
    scale = 1.0 / math.sqrt(qm.shape[-1])
    sc = jnp.einsum('bshd,bmhd->bhsm', qm, mk).astype(jnp.float32) * scale
    p = jax.nn.softmax(sc, axis=-1)
    out = jnp.einsum('bhsm,bmhd->bshd', p.astype(mv.dtype), mv)
    return out.reshape(qm.shape[0], qm.shape[1], D_MEM)


def sq_relu_mlp(x, w_up, w_down):
    h = jnp.square(jax.nn.relu(x @ w_up))
    return h @ w_down


def setup_inputs(seed: int = 0) -> dict:
    key = jax.random.key(seed)
    ks = jax.random.split(key, 20)
    f32 = jnp.float32
    d = D_MODEL
    nrm = lambda k, shp: jax.random.normal(k, shp, f32)
    x = nrm(ks[0], (BATCH, SEQ, d))
    mem = nrm(ks[1], (BATCH, N_MEM_TOKENS, d))
    a_w_in = nrm(ks[2], (N_A_LAYERS, d, 2 * D_MIX + D_MEM)) * d ** -0.5
    a_sgu_ln_g = 1.0 + 0.02 * nrm(ks[3], (N_A_LAYERS, D_MIX))
    a_sgu_ln_b = 0.02 * nrm(ks[4], (N_A_LAYERS, D_MIX))
    a_w_s = nrm(ks[5], (N_A_LAYERS, N_MIX_HEADS, CHUNK, CHUNK)) * (0.5 * CHUNK ** -0.5)
    a_b_s = 1.0 + 0.02 * nrm(ks[6], (N_A_LAYERS, N_MIX_HEADS, CHUNK))
    kv_wk = nrm(ks[7], (d, D_MIX)) * d ** -0.5
    kv_wv = nrm(ks[8], (d, D_MIX)) * d ** -0.5 * DN_BETA
    kv_wf = nrm(ks[9], (d, N_MIX_HEADS)) * d ** -0.5
    kv_w = jnp.concatenate([kv_wk, kv_wv, kv_wf], axis=-1)
    kv_b_f = jnp.linspace(1.0, 6.0, N_MIX_HEADS, dtype=f32) + 0.01 * nrm(ks[10], (N_MIX_HEADS,))
    b_w_q = nrm(ks[11], (N_B_LAYERS, d, D_MIX + D_MEM)) * d ** -0.5
    mk = nrm(ks[12], (DEPTH, d, D_MEM)) * d ** -0.5
    mv = nrm(ks[13], (DEPTH, d, D_MEM)) * d ** -0.5 * DN_BETA
    mem_w_kv = jnp.concatenate([mk, mv], axis=-1)
    w_o = nrm(ks[14], (DEPTH, D_MIX + D_MEM, d)) * (D_MIX + D_MEM) ** -0.5 * DN_BETA
    ln_g = 1.0 + 0.02 * nrm(ks[15], (DEPTH, 2, d))
    ln_b = 0.02 * nrm(ks[16], (DEPTH, 2, d))
    w_up = nrm(ks[17], (DEPTH, d, D_FF)) * d ** -0.5
    w_down = nrm(ks[18], (DEPTH, D_FF, d)) * D_FF ** -0.5 * DN_BETA
    return {"x": x, "mem": mem, "a_w_in": a_w_in, "a_sgu_ln_g": a_sgu_ln_g,
            "a_sgu_ln_b": a_sgu_ln_b, "a_w_s": a_w_s, "a_b_s": a_b_s,
            "kv_w": kv_w, "kv_b_f": kv_b_f, "b_w_q": b_w_q, "mem_w_kv": mem_w_kv,
            "w_o": w_o, "ln_g": ln_g, "ln_b": ln_b, "w_up": w_up, "w_down": w_down}


def reference(x, mem, a_w_in, a_sgu_ln_g, a_sgu_ln_b, a_w_s, a_b_s, kv_w, kv_b_f,
              b_w_q, mem_w_kv, w_o, ln_g, ln_b, w_up, w_down):
    bsz, s, _ = x.shape
    m = mem.shape[1]
    k_sh = v_sh = logcum = None
    for layer in range(DEPTH):
        if layer < N_A_LAYERS:
            z = x @ a_w_in[layer]
            zu = jax.nn.gelu(z[..., :D_MIX])
            zv = layer_norm(jax.nn.gelu(z[..., D_MIX:2 * D_MIX]), a_sgu_ln_g[layer], a_sgu_ln_b[layer])
            qm = z[..., 2 * D_MIX:]
            mix = chunked_sgu(zu.reshape(bsz, s, N_MIX_HEADS, HEAD_DIM),
                              zv.reshape(bsz, s, N_MIX_HEADS, HEAD_DIM),
                              a_w_s[layer], a_b_s[layer]).reshape(bsz, s, D_MIX)
        else:
            if layer == N_A_LAYERS:
                kvf = x @ kv_w
                k_sh = kvf[..., :D_MIX].reshape(bsz, s, N_MIX_HEADS, HEAD_DIM)
                v_sh = kvf[..., D_MIX:2 * D_MIX].reshape(bsz, s, N_MIX_HEADS, HEAD_DIM)
                f_logit = kvf[..., 2 * D_MIX:].astype(jnp.float32) + kv_b_f.astype(jnp.float32)
                logcum = jnp.cumsum(jax.nn.log_sigmoid(f_logit), axis=1).transpose(0, 2, 1)
            j = layer - N_A_LAYERS
            z = x @ b_w_q[j]
            q = z[..., :D_MIX].reshape(bsz, s, N_MIX_HEADS, HEAD_DIM)
            qm = z[..., D_MIX:]
            mix = fox_attention(q, k_sh, v_sh, logcum).reshape(bsz, s, D_MIX)
        mkv = mem @ mem_w_kv[layer]
        mk = mkv[..., :D_MEM].reshape(bsz, m, N_MEM_HEADS, HEAD_DIM)
        mv = mkv[..., D_MEM:].reshape(bsz, m, N_MEM_HEADS, HEAD_DIM)
        mo = memory_attention(qm.reshape(bsz, s, N_MEM_HEADS, HEAD_DIM), mk, mv)
        y = jnp.concatenate([mix, mo], axis=-1) @ w_o[layer]
        x = layer_norm(DN_ALPHA * x + y, ln_g[layer, 0], ln_b[layer, 0])
        x = layer_norm(DN_ALPHA * x + sq_relu_mlp(x, w_up[layer], w_down[layer]),
                       ln_g[layer, 1], ln_b[layer, 1])
    return x
```

```python
import functools
import math

import jax
import jax.numpy as jnp
from jax import lax
from jax.experimental import pallas as pl
from jax.experimental.pallas import tpu as pltpu

D_MODEL = 1024
HEAD_DIM = 64
N_MIX_HEADS = 12
N_MEM_HEADS = 4
D_MIX = N_MIX_HEADS * HEAD_DIM
D_MEM = N_MEM_HEADS * HEAD_DIM
CHUNK = 128
N_MEM_TOKENS = 256
D_FF = 4 * D_MODEL
DEPTH = 2
DN_ALPHA = (2 * DEPTH) ** 0.25
LN_EPS = 1e-5

LANES = 128
V_ROWS = 80
BIAS_ROWS = 64
NEG_BIG = -1e30

F32 = jnp.float32
BF16 = jnp.bfloat16

VMEM_LIMIT = 56 * 1024 * 1024


def _const_spec(shape):
    nd = len(shape)
    return pl.BlockSpec(shape, lambda *_: (0,) * nd, pipeline_mode=pl.Buffered(1))


def _dot(a, b):
    return jnp.dot(a, b, preferred_element_type=F32)


def _dot_nt(a, b):
    return lax.dot_general(a, b, (((1,), (1,)), ((), ())), preferred_element_type=F32)


def _layer_norm(r, g, b):
    mu = jnp.mean(r, axis=-1, keepdims=True)
    d = r - mu
    var = jnp.mean(d * d, axis=-1, keepdims=True)
    return d * lax.rsqrt(var + LN_EPS) * g + b


def _split3(c):
    hi = c.astype(BF16).astype(F32)
    r1 = c - hi
    mid = r1.astype(BF16).astype(F32)
    lo = (r1 - mid).astype(BF16).astype(F32)
    return hi, mid, lo


def _mem_kv_kernel(mem_ref, wkT_ref, wv_ref, mkT_ref, mvb_ref):
    memb = mem_ref[0].astype(BF16)
    kT = _dot_nt(wkT_ref[0], memb)
    kT4 = jnp.concatenate([kT] * N_MEM_HEADS, axis=1)
    row = lax.broadcasted_iota(jnp.int32, kT4.shape, 0)
    col = lax.broadcasted_iota(jnp.int32, kT4.shape, 1)
    same = (row >> 6) == (col >> 8)
    mkT_ref[0, 0] = jnp.where(same, kT4, 0.0).astype(BF16)
    v = _dot(memb, wv_ref[0])
    v4 = jnp.concatenate([v] * N_MEM_HEADS, axis=0)
    row = lax.broadcasted_iota(jnp.int32, v4.shape, 0)
    col = lax.broadcasted_iota(jnp.int32, v4.shape, 1)
    same = (row >> 8) == (col >> 6)
    mvb_ref[0, 0] = jnp.where(same, v4, 0.0).astype(BF16)


def _mem_kv(mem, wkT, wv):
    bsz = mem.shape[0]
    m = N_MEM_TOKENS
    return pl.pallas_call(
        _mem_kv_kernel,
        grid=(DEPTH, bsz),
        in_specs=[
            pl.BlockSpec((1, m, D_MODEL), lambda l, b: (b, 0, 0)),
            pl.BlockSpec((1, D_MEM, D_MODEL), lambda l, b: (l, 0, 0)),
            pl.BlockSpec((1, D_MODEL, D_MEM), lambda l, b: (l, 0, 0)),
        ],
        out_specs=[
            pl.BlockSpec((1, 1, D_MEM, N_MEM_HEADS * m), lambda l, b: (l, b, 0, 0)),
            pl.BlockSpec((1, 1, N_MEM_HEADS * m, D_MEM), lambda l, b: (l, b, 0, 0)),
        ],
        out_shape=[
            jax.ShapeDtypeStruct((DEPTH, bsz, D_MEM, N_MEM_HEADS * m), BF16),
            jax.ShapeDtypeStruct((DEPTH, bsz, N_MEM_HEADS * m, D_MEM), BF16),
        ],
        name="mem_kv",
    )(mem, wkT, wv)


def _sgu_kernel(x_ref, win_ref, g_ref, b_ref, ws_ref, bs_ref, mix_ref, qm_ref, *, tm):
    xb = x_ref[...].astype(BF16)
    z = _dot(xb, win_ref[...])
    zu = jax.nn.gelu(z[:, :D_MIX], approximate=True)
    zv = jax.nn.gelu(z[:, D_MIX:2 * D_MIX], approximate=True)
    zv = _layer_norm(zv, g_ref[...], b_ref[...])
    qm_ref[...] = z[:, 2 * D_MIX:].astype(BF16)

    lane = lax.broadcasted_iota(jnp.int32, (CHUNK, LANES), 1)
    low = lane < HEAD_DIM
    wrow = lax.broadcasted_iota(jnp.int32, (CHUNK, 2 * CHUNK), 0)
    wcol = lax.broadcasted_iota(jnp.int32, (CHUNK, 2 * CHUNK), 1) & (CHUNK - 1)
    causal = wcol <= wrow
    for p in range(N_MIX_HEADS // 2):
        w = jnp.where(causal, ws_ref[p], 0.0).astype(BF16)
        bias = bs_ref[:, p * LANES:(p + 1) * LANES]
        for c0 in range(0, tm // CHUNK, 2):
            rhs = []
            for c in (c0, c0 + 1):
                slab = zv[c * CHUNK:(c + 1) * CHUNK, p * LANES:(p + 1) * LANES]
                rhs.append(jnp.concatenate(
                    [jnp.where(low, slab, 0.0), jnp.where(low, 0.0, slab)], axis=0))
            rhs = jnp.concatenate(rhs, axis=1).astype(BF16)
            mixed = _dot(w, rhs)
            for i, c in enumerate((c0, c0 + 1)):
                u = zu[c * CHUNK:(c + 1) * CHUNK, p * LANES:(p + 1) * LANES]
                out = u * (mixed[:, i * LANES:(i + 1) * LANES] + bias)
                mix_ref[c * CHUNK:(c + 1) * CHUNK, p * LANES:(p + 1) * LANES] = out.astype(BF16)


def _sgu(x2d, w_in, ln_g, ln_b, ws_pair, bs_full, tm):
    t = x2d.shape[0]
    n_in = w_in.shape[1]
    return pl.pallas_call(
        functools.partial(_sgu_kernel, tm=tm),
        grid=(t // tm,),
        in_specs=[
            pl.BlockSpec((tm, D_MODEL), lambda i: (i, 0)),
            _const_spec((D_MODEL, n_in)),
            _const_spec((1, D_MIX)),
            _const_spec((1, D_MIX)),
            _const_spec((N_MIX_HEADS // 2, CHUNK, 2 * CHUNK)),
            _const_spec((CHUNK, D_MIX)),
        ],
        out_specs=[pl.BlockSpec((tm, D_MIX), lambda i: (i, 0)),
                   pl.BlockSpec((tm, D_MEM), lambda i: (i, 0))],
        out_shape=[jax.ShapeDtypeStruct((t, D_MIX), BF16),
                   jax.ShapeDtypeStruct((t, D_MEM), BF16)],
        compiler_params=pltpu.CompilerParams(
            dimension_semantics=("arbitrary",), vmem_limit_bytes=VMEM_LIMIT),
        name="sgu_mixer",
    )(x2d, w_in, ln_g, ln_b, ws_pair, bs_full)


def _tail_kernel(mix_ref, qm_ref, x_ref, mkT_ref, mvb_ref, wo_ref, g_ref, b_ref, o_ref):
    m = N_MEM_TOKENS
    s = _dot(qm_ref[...], mkT_ref[0])
    ps = []
    for h in range(N_MEM_HEADS):
        sh = s[:, h * m:(h + 1) * m]
        e = jnp.exp(sh - jnp.max(sh, axis=-1, keepdims=True))
        ps.append((e / jnp.sum(e, axis=-1, keepdims=True)).astype(BF16))
    p = jnp.concatenate(ps, axis=1)
    mo = _dot(p, mvb_ref[0]).astype(BF16)
    y = _dot(mix_ref[...], wo_ref[:D_MIX, :]) + _dot(mo, wo_ref[D_MIX:, :])
    r = DN_ALPHA * x_ref[...] + y
    o_ref[...] = _layer_norm(r, g_ref[...], b_ref[...])


def _tail(mix, qm, x2d, mkT, mvb, w_o, ln_g, ln_b, seq, tm):
    t = x2d.shape[0]
    m = N_MEM_TOKENS
    per_b = seq // tm
    return pl.pallas_call(
        _tail_kernel,
        grid=(t // tm,),
        in_specs=[
            pl.BlockSpec((tm, D_MIX), lambda i: (i, 0)),
            pl.BlockSpec((tm, D_MEM), lambda i: (i, 0)),
            pl.BlockSpec((tm, D_MODEL), lambda i: (i, 0)),
            pl.BlockSpec((1, D_MEM, N_MEM_HEADS * m), lambda i: (i // per_b, 0, 0)),
            pl.BlockSpec((1, N_MEM_HEADS * m, D_MEM), lambda i: (i // per_b, 0, 0)),
            _const_spec((D_MODEL, D_MODEL)),
            _const_spec((1, D_MODEL)),
            _const_spec((1, D_MODEL)),
        ],
        out_specs=pl.BlockSpec((tm, D_MODEL), lambda i: (i, 0)),
        out_shape=jax.ShapeDtypeStruct((t, D_MODEL), F32),
        compiler_params=pltpu.CompilerParams(
            dimension_semantics=("arbitrary",), vmem_limit_bytes=VMEM_LIMIT),
        name="mixer_tail",
    )(mix, qm, x2d, mkT, mvb, w_o, ln_g, ln_b)


def _mlp_kernel(x_ref, wu_ref, wd_ref, g_ref, b_ref, o_ref, h_ref, *, n_chunk):
    x = x_ref[...]
    xb = x.astype(BF16)
    for c in range(D_FF // n_chunk):
        h = _dot(xb, wu_ref[:, c * n_chunk:(c + 1) * n_chunk])
        h = jnp.maximum(h, 0.0)
        h_ref[:, c * n_chunk:(c + 1) * n_chunk] = (h * h).astype(BF16)
    y = _dot(h_ref[...], wd_ref[...])
    r = DN_ALPHA * x + y
    o_ref[...] = _layer_norm(r, g_ref[...], b_ref[...])


def _mlp(x2d, w_up, w_down, ln_g, ln_b, tm, n_chunk=1024):
    t = x2d.shape[0]
    return pl.pallas_call(
        functools.partial(_mlp_kernel, n_chunk=n_chunk),
        grid=(t // tm,),
        in_specs=[
            pl.BlockSpec((tm, D_MODEL), lambda i: (i, 0)),
            _const_spec((D_MODEL, D_FF)),
            _const_spec((D_FF, D_MODEL)),
            _const_spec((1, D_MODEL)),
            _const_spec((1, D_MODEL)),
        ],
        out_specs=pl.BlockSpec((tm, D_MODEL), lambda i: (i, 0)),
        out_shape=jax.ShapeDtypeStruct((t, D_MODEL), F32),
        scratch_shapes=[pltpu.VMEM((tm, D_FF), BF16)],
        compiler_params=pltpu.CompilerParams(
            dimension_semantics=("arbitrary",), vmem_limit_bytes=VMEM_LIMIT),
        name="relu2_mlp",
    )(x2d, w_up, w_down, ln_g, ln_b)


def _log_sigmoid(x):
    return jnp.minimum(x, 0.0) - jnp.log(1.0 + jnp.exp(-jnp.abs(x)))


def _bproj_kernel(x_ref, wnt_ref, wnn_ref, bf_ref,
                  qT_ref, qb_ref, k_ref, kb_ref, vT_ref, qm_ref, carry_ref, *, tm):
    @pl.when(pl.program_id(1) == 0)
    def _():
        carry_ref[...] = jnp.zeros_like(carry_ref)

    xb = x_ref[...].astype(BF16)
    nt = _dot_nt(wnt_ref[...], xb)
    qT_ref[0] = nt[:D_MIX].astype(BF16)
    vrow = lax.broadcasted_iota(jnp.int32, (V_ROWS - HEAD_DIM, tm), 0)
    ones_rows = jnp.where(vrow == 0, 1.0, 0.0).astype(BF16)
    for h in range(N_MIX_HEADS):
        vT_ref[0, h, :HEAD_DIM, :] = nt[D_MIX + h * HEAD_DIM:D_MIX + (h + 1) * HEAD_DIM].astype(BF16)
        vT_ref[0, h, HEAD_DIM:, :] = ones_rows

    nn = _dot(xb, wnn_ref[...])
    k_ref[0] = nn[:, :D_MIX].astype(BF16)
    qm_ref[...] = nn[:, D_MIX:D_MIX + D_MEM].astype(BF16)

    f3 = nn[:, D_MIX + D_MEM:] + bf_ref[...]
    lane = lax.broadcasted_iota(jnp.int32, (tm, LANES), 1)
    valid = (lane & 15) < N_MIX_HEADS
    valid = valid & (lane < 48)
    c = jnp.where(valid, _log_sigmoid(f3), 0.0)
    rowi = lax.broadcasted_iota(jnp.int32, (tm, LANES), 0)
    d = 1
    while d < tm:
        c = c + jnp.where(rowi >= d, pltpu.roll(c, d, axis=0), 0.0)
        d *= 2
    c = c + carry_ref[...]
    carry_ref[...] = c[tm - 1:tm, :]

    hi, mid, lo = _split3(c)
    kb = jnp.where(lane < 16, -hi,
                   jnp.where(lane < 32, -mid,
                             jnp.where(lane < 48, -lo,
                                       jnp.where(lane < 51, 1.0, 0.0))))
    kb_ref[0] = kb.astype(BF16)

    cT = c.T
    brow = lax.broadcasted_iota(jnp.int32, (BIAS_ROWS, tm), 0)
    for h in range(N_MIX_HEADS):
        chi, cmid, clo = _split3(cT[h:h + 1, :])
        sel = (brow == h) | (brow == 16 + h) | (brow == 32 + h)
        tile = jnp.where(brow == 48, chi,
                         jnp.where(brow == 49, cmid,
                                   jnp.where(brow == 50, clo,
                                             jnp.where(sel, 1.0, 0.0))))
        qb_ref[0, h] = tile.astype(BF16)


def _bproj(x3d, wnt, wnn, bf3, tm):
    bsz, seq, _ = x3d.shape
    n_nn = wnn.shape[1]
    return pl.pallas_call(
        functools.partial(_bproj_kernel, tm=tm),
        grid=(bsz, seq // tm),
        in_specs=[
            pl.BlockSpec((None, tm, D_MODEL), lambda b, j: (b, j, 0)),
            _const_spec((2 * D_MIX, D_MODEL)),
            _const_spec((D_MODEL, n_nn)),
            _const_spec((1, LANES)),
        ],
        out_specs=[
            pl.BlockSpec((1, D_MIX, tm), lambda b, j: (b, 0, j)),
            pl.BlockSpec((1, N_MIX_HEADS, BIAS_ROWS, tm), lambda b, j: (b, 0, 0, j)),
            pl.BlockSpec((1, tm, D_MIX), lambda b, j: (b, j, 0)),
            pl.BlockSpec((1, tm, LANES), lambda b, j: (b, j, 0)),
            pl.BlockSpec((1, N_MIX_HEADS, V_ROWS, tm), lambda b, j: (b, 0, 0, j)),
            pl.BlockSpec((tm, D_MEM), lambda b, j: (b * (seq // tm) + j, 0)),
        ],
        out_shape=[
            jax.ShapeDtypeStruct((bsz, D_MIX, seq), BF16),
            jax.ShapeDtypeStruct((bsz, N_MIX_HEADS, BIAS_ROWS, seq), BF16),
            jax.ShapeDtypeStruct((bsz, seq, D_MIX), BF16),
            jax.ShapeDtypeStruct((bsz, seq, LANES), BF16),
            jax.ShapeDtypeStruct((bsz, N_MIX_HEADS, V_ROWS, seq), BF16),
            jax.ShapeDtypeStruct((bsz * seq, D_MEM), BF16),
        ],
        scratch_shapes=[pltpu.VMEM((1, LANES), F32)],
        compiler_params=pltpu.CompilerParams(
            dimension_semantics=("arbitrary", "arbitrary"), vmem_limit_bytes=VMEM_LIMIT),
        name="fox_proj",
    )(x3d, wnt, wnn, bf3)


def _fox_kernel(qT_ref, qb_ref, k_ref, kb_ref, vT_ref, o_ref, *, tq, tk):
    qi = pl.program_id(2)
    slab = qT_ref[0]
    srow = lax.broadcasted_iota(jnp.int32, slab.shape, 0)
    zpad = jnp.zeros((LANES - BIAS_ROWS, tq), BF16)
    krow = lax.broadcasted_iota(jnp.int32, (tk, tq), 0)
    qcol = lax.broadcasted_iota(jnp.int32, (tk, tq), 1)

    outs = []
    for hh in range(2):
        keep = (srow < HEAD_DIM) if hh == 0 else (srow >= HEAD_DIM)
        qa = jnp.concatenate(
            [jnp.where(keep, slab, jnp.zeros_like(slab)), qb_ref[0, hh], zpad], axis=0)

        def step(ks, carry, mask):
            m_prev, acc = carry
            ka = jnp.concatenate([k_ref[0, pl.ds(ks, tk), :], kb_ref[0, pl.ds(ks, tk), :]], axis=1)
            s = _dot(ka, qa)
            if mask is not None:
                s = jnp.where(mask, s, NEG_BIG)
            m_new = jnp.maximum(m_prev, jnp.max(s, axis=0, keepdims=True))
            alpha = jnp.exp(m_prev - m_new)
            p = jnp.exp(s - m_new).astype(BF16)
            pv = _dot(vT_ref[0, hh, :, pl.ds(ks, tk)], p)
            return m_new, alpha * acc + pv

        def body(j, carry):
            return step(pl.multiple_of(j * tk, tk), carry, None)

        carry = (jnp.full((1, tq), NEG_BIG, F32), jnp.zeros((V_ROWS, tq), F32))
        carry = lax.fori_loop(0, qi * (tq // tk), body, carry)
        for dj in range(tq // tk):
            mask = (krow + dj * tk) <= qcol
            carry = step(pl.multiple_of(qi * tq + dj * tk, tk), carry, mask)
        acc = carry[1]
        outs.append(acc[:HEAD_DIM] / acc[HEAD_DIM:HEAD_DIM + 1])
    o = jnp.concatenate(outs, axis=0)
    o_ref[0] = o.T.astype(BF16)


def _fox(qT, qb, k, kb, vT, tq, tk):
    bsz, _, seq = qT.shape
    n_pair = N_MIX_HEADS // 2
    return pl.pallas_call(
        functools.partial(_fox_kernel, tq=tq, tk=tk),
        grid=(bsz, n_pair, seq // tq),
        in_specs=[
            pl.BlockSpec((1, 2 * HEAD_DIM, tq), lambda b, p, i: (b, p, i)),
            pl.BlockSpec((1, 2, BIAS_ROWS, tq), lambda b, p, i: (b, p, 0, i)),
            pl.BlockSpec((1, seq, LANES), lambda b, p, i: (b, 0, p)),
            pl.BlockSpec((1, seq, LANES), lambda b, p, i: (b, 0, 0)),
            pl.BlockSpec((1, 2, V_ROWS, seq), lambda b, p, i: (b, p, 0, 0)),
        ],
        out_specs=pl.BlockSpec((1, tq, LANES), lambda b, p, i: (b, i, p)),
        out_shape=jax.ShapeDtypeStruct((bsz, seq, D_MIX), BF16),
        compiler_params=pltpu.CompilerParams(
            dimension_semantics=("arbitrary", "arbitrary", "arbitrary"),
            vmem_limit_bytes=VMEM_LIMIT),
        name="fox_attention",
    )(qT, qb, k, kb, vT)


def kernel(x, mem, a_w_in, a_sgu_ln_g, a_sgu_ln_b, a_w_s, a_b_s, kv_w, kv_b_f, b_w_q,
           mem_w_kv, w_o, ln_g, ln_b, w_up, w_down):
    bsz, seq, d = x.shape
    t = bsz * seq
    scale = 1.0 / math.sqrt(HEAD_DIM)

    wkT = (jnp.swapaxes(mem_w_kv[:, :, :D_MEM], 1, 2) * scale).astype(BF16)
    wv = mem_w_kv[:, :, D_MEM:].astype(BF16)
    w_in = a_w_in[0].astype(BF16)
    ws_pair = (a_w_s[0].reshape(N_MIX_HEADS // 2, 2, CHUNK, CHUNK)
               .transpose(0, 2, 1, 3).reshape(N_MIX_HEADS // 2, CHUNK, 2 * CHUNK))
    bs_full = jnp.repeat(a_b_s[0].T, HEAD_DIM, axis=1)
    wq = b_w_q[0]
    wnt = jnp.concatenate([wq[:, :D_MIX].T * scale, kv_w[:, D_MIX:2 * D_MIX].T], axis=0).astype(BF16)
    wf = kv_w[:, 2 * D_MIX:]
    zf = jnp.zeros((d, 16 - N_MIX_HEADS), F32)
    wf3 = jnp.concatenate([wf, zf, wf, zf, wf, zf, jnp.zeros((d, LANES - 48), F32)], axis=1)
    wnn = jnp.concatenate([kv_w[:, :D_MIX], wq[:, D_MIX:], wf3], axis=1).astype(BF16)
    zb = jnp.zeros((16 - N_MIX_HEADS,), F32)
    bf3 = jnp.concatenate([kv_b_f, zb, kv_b_f, zb, kv_b_f, zb, jnp.zeros((LANES - 48,), F32)])[None, :]
    w_o_b = w_o.astype(BF16)
    w_up_b = w_up.astype(BF16)
    w_down_b = w_down.astype(BF16)
    row = lambda v: v[None, :]

    mkT, mvb = _mem_kv(mem, wkT, wv)
    x2d = x.reshape(t, d)

    tm_a = min(256, seq)
    mix, qm = _sgu(x2d, w_in, row(a_sgu_ln_g[0]), row(a_sgu_ln_b[0]), ws_pair, bs_full, tm_a)
    x2d = _tail(mix, qm, x2d, mkT[0], mvb[0], w_o_b[0], row(ln_g[0, 0]), row(ln_b[0, 0]), seq, tm_a)
    tm_m = min(512, seq)
    x2d = _mlp(x2d, w_up_b[0], w_down_b[0], row(ln_g[0, 1]), row(ln_b[0, 1]), tm_m)

    tm_b = min(512, seq)
    qT, qb, k, kb, vT, qm = _bproj(x2d.reshape(bsz, seq, d), wnt, wnn, bf3, tm_b)
    tq = min(256, seq)
    att = _fox(qT, qb, k, kb, vT, tq, tq).reshape(t, D_MIX)
    x2d = _tail(att, qm, x2d, mkT[1], mvb[1], w_o_b[1], row(ln_g[1, 0]), row(ln_b[1, 0]), seq, tm_a)
    x2d = _mlp(x2d, w_up_b[1], w_down_b[1], row(ln_g[1, 1]), row(ln_b[1, 1]), tm_m)
    return x2d.reshape(bsz, seq, d)
```

```python
import functools
import math

import jax
import jax.numpy as jnp
from jax import lax
from jax.experimental import pallas as pl
from jax.experimental.pallas import tpu as pltpu

D_MODEL = 1024
HEAD_DIM = 64
N_MIX_HEADS = 12
N_MEM_HEADS = 4
D_MIX = N_MIX_HEADS * HEAD_DIM
D_MEM = N_MEM_HEADS * HEAD_DIM
CHUNK = 128
N_MEM_TOKENS = 256
D_FF = 4 * D_MODEL
DEPTH = 2
DN_ALPHA = (2 * DEPTH) ** 0.25
LN_EPS = 1e-5

LANES = 128
V_ROWS = 80
BIAS_ROWS = 64
NEG_BIG = -1e30

F32 = jnp.float32
BF16 = jnp.bfloat16

VMEM_LIMIT = 56 * 1024 * 1024


def _const_spec(shape):
    nd = len(shape)
    return pl.BlockSpec(shape, lambda *_: (0,) * nd, pipeline_mode=pl.Buffered(1))


def _dot(a, b):
    return jnp.dot(a, b, preferred_element_type=F32)


def _dot_nt(a, b):
    return lax.dot_general(a, b, (((1,), (1,)), ((), ())), preferred_element_type=F32)


def _layer_norm(r, g, b):
    mu = jnp.mean(r, axis=-1, keepdims=True)
    d = r - mu
    var = jnp.mean(d * d, axis=-1, keepdims=True)
    return d * lax.rsqrt(var + LN_EPS) * g + b


def _split3(c):
    hi = c.astype(BF16).astype(F32)
    r1 = c - hi
    mid = r1.astype(BF16).astype(F32)
    lo = (r1 - mid).astype(BF16).astype(F32)
    return hi, mid, lo


def _mem_kv_kernel(mem_ref, wkT_ref, wv_ref, mkT_ref, mvb_ref):
    memb = mem_ref[0].astype(BF16)
    kT = _dot_nt(wkT_ref[0], memb)
    kT4 = jnp.concatenate([kT] * N_MEM_HEADS, axis=1)
    row = lax.broadcasted_iota(jnp.int32, kT4.shape, 0)
    col = lax.broadcasted_iota(jnp.int32, kT4.shape, 1)
    same = (row >> 6) == (col >> 8)
    mkT_ref[0, 0] = jnp.where(same, kT4, 0.0).astype(BF16)
    v = _dot(memb, wv_ref[0])
    v4 = jnp.concatenate([v] * N_MEM_HEADS, axis=0)
    row = lax.broadcasted_iota(jnp.int32, v4.shape, 0)
    col = lax.broadcasted_iota(jnp.int32, v4.shape, 1)
    same = (row >> 8) == (col >> 6)
    mvb_ref[0, 0] = jnp.where(same, v4, 0.0).astype(BF16)


def _mem_kv(mem, wkT, wv):
    bsz = mem.shape[0]
    m = N_MEM_TOKENS
    return pl.pallas_call(
        _mem_kv_kernel,
        grid=(DEPTH, bsz),
        in_specs=[
            pl.BlockSpec((1, m, D_MODEL), lambda l, b: (b, 0, 0)),
            pl.BlockSpec((1, D_MEM, D_MODEL), lambda l, b: (l, 0, 0)),
            pl.BlockSpec((1, D_MODEL, D_MEM), lambda l, b: (l, 0, 0)),
        ],
        out_specs=[
            pl.BlockSpec((1, 1, D_MEM, N_MEM_HEADS * m), lambda l, b: (l, b, 0, 0)),
            pl.BlockSpec((1, 1, N_MEM_HEADS * m, D_MEM), lambda l, b: (l, b, 0, 0)),
        ],
        out_shape=[
            jax.ShapeDtypeStruct((DEPTH, bsz, D_MEM, N_MEM_HEADS * m), BF16),
            jax.ShapeDtypeStruct((DEPTH, bsz, N_MEM_HEADS * m, D_MEM), BF16),
        ],
        name="mem_kv",
    )(mem, wkT, wv)


def _sgu_kernel(x_ref, win_ref, g_ref, b_ref, ws_ref, bs_ref, mix_ref, qm_ref, *, tm):
    xb = x_ref[...].astype(BF16)
    z = _dot(xb, win_ref[...])
    zu = jax.nn.gelu(z[:, :D_MIX], approximate=True)
    zv = jax.nn.gelu(z[:, D_MIX:2 * D_MIX], approximate=True)
    zv = _layer_norm(zv, g_ref[...], b_ref[...])
    qm_ref[...] = z[:, 2 * D_MIX:].astype(BF16)

    lane = lax.broadcasted_iota(jnp.int32, (CHUNK, LANES), 1)
    low = lane < HEAD_DIM
    wrow = lax.broadcasted_iota(jnp.int32, (CHUNK, 2 * CHUNK), 0)
    wcol = lax.broadcasted_iota(jnp.int32, (CHUNK, 2 * CHUNK), 1) & (CHUNK - 1)
    causal = wcol <= wrow
    for p in range(N_MIX_HEADS // 2):
        w = jnp.where(causal, ws_ref[p], 0.0).astype(BF16)
        bias = bs_ref[:, p * LANES:(p + 1) * LANES]
        for c0 in range(0, tm // CHUNK, 2):
            rhs = []
            for c in (c0, c0 + 1):
                slab = zv[c * CHUNK:(c + 1) * CHUNK, p * LANES:(p + 1) * LANES]
                rhs.append(jnp.concatenate(
                    [jnp.where(low, slab, 0.0), jnp.where(low, 0.0, slab)], axis=0))
            rhs = jnp.concatenate(rhs, axis=1).astype(BF16)
            mixed = _dot(w, rhs)
            for i, c in enumerate((c0, c0 + 1)):
                u = zu[c * CHUNK:(c + 1) * CHUNK, p * LANES:(p + 1) * LANES]
                out = u * (mixed[:, i * LANES:(i + 1) * LANES] + bias)
                mix_ref[c * CHUNK:(c + 1) * CHUNK, p * LANES:(p + 1) * LANES] = out.astype(BF16)


def _sgu(x2d, w_in, ln_g, ln_b, ws_pair, bs_full, tm):
    t = x2d.shape[0]
    n_in = w_in.shape[1]
    return pl.pallas_call(
        functools.partial(_sgu_kernel, tm=tm),
        grid=(t // tm,),
        in_specs=[
            pl.BlockSpec((tm, D_MODEL), lambda i: (i, 0)),
            _const_spec((D_MODEL, n_in)),
            _const_spec((1, D_MIX)),
            _const_spec((1, D_MIX)),
            _const_spec((N_MIX_HEADS // 2, CHUNK, 2 * CHUNK)),
            _const_spec((CHUNK, D_MIX)),
        ],
        out_specs=[pl.BlockSpec((tm, D_MIX), lambda i: (i, 0)),
                   pl.BlockSpec((tm, D_MEM), lambda i: (i, 0))],
        out_shape=[jax.ShapeDtypeStruct((t, D_MIX), BF16),
                   jax.ShapeDtypeStruct((t, D_MEM), BF16)],
        compiler_params=pltpu.CompilerParams(
            dimension_semantics=("arbitrary",), vmem_limit_bytes=VMEM_LIMIT),
        name="sgu_mixer",
    )(x2d, w_in, ln_g, ln_b, ws_pair, bs_full)


def _tail_kernel(mix_ref, qm_ref, x_ref, mkT_ref, mvb_ref, wo_ref, g_ref, b_ref, o_ref):
    m = N_MEM_TOKENS
    s = _dot(qm_ref[...], mkT_ref[0])
    ps = []
    for h in range(N_MEM_HEADS):
        sh = s[:, h * m:(h + 1) * m]
        e = jnp.exp(sh - jnp.max(sh, axis=-1, keepdims=True))
        ps.append((e / jnp.sum(e, axis=-1, keepdims=True)).astype(BF16))
    p = jnp.concatenate(ps, axis=1)
    mo = _dot(p, mvb_ref[0]).astype(BF16)
    y = _dot(mix_ref[...], wo_ref[:D_MIX, :]) + _dot(mo, wo_ref[D_MIX:, :])
    r = DN_ALPHA * x_ref[...] + y
    o_ref[...] = _layer_norm(r, g_ref[...], b_ref[...])


def _tail(mix, qm, x2d, mkT, mvb, w_o, ln_g, ln_b, seq, tm):
    t = x2d.shape[0]
    m = N_MEM_TOKENS
    per_b = seq // tm
    return pl.pallas_call(
        _tail_kernel,
        grid=(t // tm,),
        in_specs=[
            pl.BlockSpec((tm, D_MIX), lambda i: (i, 0)),
            pl.BlockSpec((tm, D_MEM), lambda i: (i, 0)),
            pl.BlockSpec((tm, D_MODEL), lambda i: (i, 0)),
            pl.BlockSpec((1, D_MEM, N_MEM_HEADS * m), lambda i: (i // per_b, 0, 0)),
            pl.BlockSpec((1, N_MEM_HEADS * m, D_MEM), lambda i: (i // per_b, 0, 0)),
            _const_spec((D_MODEL, D_MODEL)),
            _const_spec((1, D_MODEL)),
            _const_spec((1, D_MODEL)),
        ],
        out_specs=pl.BlockSpec((tm, D_MODEL), lambda i: (i, 0)),
        out_shape=jax.ShapeDtypeStruct((t, D_MODEL), F32),
        compiler_params=pltpu.CompilerParams(
            dimension_semantics=("arbitrary",), vmem_limit_bytes=VMEM_LIMIT),
        name="mixer_tail",
    )(mix, qm, x2d, mkT, mvb, w_o, ln_g, ln_b)


def _mlp_kernel(x_ref, wu_ref, wd_ref, g_ref, b_ref, o_ref, h_ref, *, n_chunk):
    x = x_ref[...]
    xb = x.astype(BF16)
    for c in range(D_FF // n_chunk):
        h = _dot(xb, wu_ref[:, c * n_chunk:(c + 1) * n_chunk])
        h = jnp.maximum(h, 0.0)
        h_ref[:, c * n_chunk:(c + 1) * n_chunk] = (h * h).astype(BF16)
    y = _dot(h_ref[...], wd_ref[...])
    r = DN_ALPHA * x + y
    o_ref[...] = _layer_norm(r, g_ref[...], b_ref[...])


def _mlp(x2d, w_up, w_down, ln_g, ln_b, tm, n_chunk=1024):
    t = x2d.shape[0]
    return pl.pallas_call(
        functools.partial(_mlp_kernel, n_chunk=n_chunk),
        grid=(t // tm,),
        in_specs=[
            pl.BlockSpec((tm, D_MODEL), lambda i: (i, 0)),
            _const_spec((D_MODEL, D_FF)),
            _const_spec((D_FF, D_MODEL)),
            _const_spec((1, D_MODEL)),
            _const_spec((1, D_MODEL)),
        ],
        out_specs=pl.BlockSpec((tm, D_MODEL), lambda i: (i, 0)),
        out_shape=jax.ShapeDtypeStruct((t, D_MODEL), F32),
        scratch_shapes=[pltpu.VMEM((tm, D_FF), BF16)],
        compiler_params=pltpu.CompilerParams(
            dimension_semantics=("arbitrary",), vmem_limit_bytes=VMEM_LIMIT),
        name="relu2_mlp",
    )(x2d, w_up, w_down, ln_g, ln_b)


def _log_sigmoid(x):
    return jnp.minimum(x, 0.0) - jnp.log(1.0 + jnp.exp(-jnp.abs(x)))


def _bproj_kernel(x_ref, wnt_ref, wnn_ref, bf_ref,
                  qT_ref, qb_ref, k_ref, kb_ref, vT_ref, qm_ref, carry_ref, *, tm):
    @pl.when(pl.program_id(1) == 0)
    def _():
        carry_ref[...] = jnp.zeros_like(carry_ref)

    xb = x_ref[...].astype(BF16)
    nt = _dot_nt(wnt_ref[...], xb)
    qT_ref[0] = nt[:D_MIX].astype(BF16)
    vrow = lax.broadcasted_iota(jnp.int32, (V_ROWS - HEAD_DIM, tm), 0)
    ones_rows = jnp.where(vrow == 0, 1.0, 0.0).astype(BF16)
    for h in range(N_MIX_HEADS):
        vT_ref[0, h, :HEAD_DIM, :] = nt[D_MIX + h * HEAD_DIM:D_MIX + (h + 1) * HEAD_DIM].astype(BF16)
        vT_ref[0, h, HEAD_DIM:, :] = ones_rows

    nn = _dot(xb, wnn_ref[...])
    k_ref[0] = nn[:, :D_MIX].astype(BF16)
    qm_ref[...] = nn[:, D_MIX:D_MIX + D_MEM].astype(BF16)

    f3 = nn[:, D_MIX + D_MEM:] + bf_ref[...]
    lane = lax.broadcasted_iota(jnp.int32, (tm, LANES), 1)
    valid = (lane & 15) < N_MIX_HEADS
    valid = valid & (lane < 48)
    c = jnp.where(valid, _log_sigmoid(f3), 0.0)
    rowi = lax.broadcasted_iota(jnp.int32, (tm, LANES), 0)
    d = 1
    while d < tm:
        c = c + jnp.where(rowi >= d, pltpu.roll(c, d, axis=0), 0.0)
        d *= 2
    c = c + carry_ref[...]
    carry_ref[...] = c[tm - 1:tm, :]

    hi, mid, lo = _split3(c)
    kb = jnp.where(lane < 16, -hi,
                   jnp.where(lane < 32, -mid,
                             jnp.where(lane < 48, -lo,
                                       jnp.where(lane < 51, 1.0, 0.0))))
    kb_ref[0] = kb.astype(BF16)

    cT = c.T
    brow = lax.broadcasted_iota(jnp.int32, (BIAS_ROWS, tm), 0)
    for h in range(N_MIX_HEADS):
        chi, cmid, clo = _split3(cT[h:h + 1, :])
        sel = (brow == h) | (brow == 16 + h) | (brow == 32 + h)
        tile = jnp.where(brow == 48, chi,
                         jnp.where(brow == 49, cmid,
                                   jnp.where(brow == 50, clo,
                                             jnp.where(sel, 1.0, 0.0))))
        qb_ref[0, h] = tile.astype(BF16)


def _bproj(x3d, wnt, wnn, bf3, tm):
    bsz, seq, _ = x3d.shape
    n_nn = wnn.shape[1]
    return pl.pallas_call(
        functools.partial(_bproj_kernel, tm=tm),
        grid=(bsz, seq // tm),
        in_specs=[
            pl.BlockSpec((None, tm, D_MODEL), lambda b, j: (b, j, 0)),
            _const_spec((2 * D_MIX, D_MODEL)),
            _const_spec((D_MODEL, n_nn)),
            _const_spec((1, LANES)),
        ],
        out_specs=[
            pl.BlockSpec((1, D_MIX, tm), lambda b, j: (b, 0, j)),
            pl.BlockSpec((1, N_MIX_HEADS, BIAS_ROWS, tm), lambda b, j: (b, 0, 0, j)),
            pl.BlockSpec((1, tm, D_MIX), lambda b, j: (b, j, 0)),
            pl.BlockSpec((1, tm, LANES), lambda b, j: (b, j, 0)),
            pl.BlockSpec((1, N_MIX_HEADS, V_ROWS, tm), lambda b, j: (b, 0, 0, j)),
            pl.BlockSpec((tm, D_MEM), lambda b, j: (b * (seq // tm) + j, 0)),
        ],
        out_shape=[
            jax.ShapeDtypeStruct((bsz, D_MIX, seq), BF16),
            jax.ShapeDtypeStruct((bsz, N_MIX_HEADS, BIAS_ROWS, seq), BF16),
            jax.ShapeDtypeStruct((bsz, seq, D_MIX), BF16),
            jax.ShapeDtypeStruct((bsz, seq, LANES), BF16),
            jax.ShapeDtypeStruct((bsz, N_MIX_HEADS, V_ROWS, seq), BF16),
            jax.ShapeDtypeStruct((bsz * seq, D_MEM), BF16),
        ],
        scratch_shapes=[pltpu.VMEM((1, LANES), F32)],
        compiler_params=pltpu.CompilerParams(
            dimension_semantics=("arbitrary", "arbitrary"), vmem_limit_bytes=VMEM_LIMIT),
        name="fox_proj",
    )(x3d, wnt, wnn, bf3)


def _fox_kernel(qT_ref, qb_ref, k_ref, kb_ref, vT_ref, o_ref, m_ref, acc_ref, s_ref, *, tq, tk):
    qi = pl.program_id(2)
    slab = qT_ref[0]
    srow = lax.broadcasted_iota(jnp.int32, slab.shape, 0)
    zpad = jnp.zeros((LANES - BIAS_ROWS, tq), BF16)
    zslab = jnp.zeros_like(slab)
    qa = jnp.concatenate(
        [jnp.concatenate([jnp.where(srow < HEAD_DIM, slab, zslab), qb_ref[0, 0], zpad], axis=0),
         jnp.concatenate([jnp.where(srow < HEAD_DIM, zslab, slab), qb_ref[0, 1], zpad], axis=0)],
        axis=1)
    krow = lax.broadcasted_iota(jnp.int32, (tk, 2 * tq), 0)
    qcol = lax.broadcasted_iota(jnp.int32, (tk, 2 * tq), 1) & (tq - 1)

    m_ref[...] = jnp.full(m_ref.shape, NEG_BIG, F32)
    acc_ref[...] = jnp.zeros(acc_ref.shape, F32)

    def scores(i, slot):
        ks = pl.multiple_of(i * tk, tk)
        ka = jnp.concatenate([k_ref[0, pl.ds(ks, tk), :], kb_ref[0, pl.ds(ks, tk), :]], axis=1)
        s_ref[slot] = _dot(ka, qa)

    def softmax_pv(i, slot, mask):
        ks = pl.multiple_of(i * tk, tk)
        s = s_ref[slot]
        if mask is not None:
            s = jnp.where(mask, s, NEG_BIG)
        m_prev = m_ref[...]
        m_new = jnp.maximum(m_prev, jnp.max(s, axis=0, keepdims=True))
        m_ref[...] = m_new
        alpha = jnp.exp(m_prev - m_new)
        p = jnp.exp(s - m_new).astype(BF16)
        pv = jnp.concatenate(
            [_dot(vT_ref[0, hh, :, pl.ds(ks, tk)], p[:, hh * tq:(hh + 1) * tq]) for hh in range(2)],
            axis=1)
        acc_ref[...] = alpha * acc_ref[...] + pv

    n_diag = tq // tk
    assert n_diag % 2 == 0
    n_off = qi * n_diag
    scores(0, 0)

    def body(j, carry):
        scores(2 * j + 1, 1)
        softmax_pv(2 * j, 0, None)
        scores(2 * j + 2, 0)
        softmax_pv(2 * j + 1, 1, None)
        return carry

    lax.fori_loop(0, n_off // 2, body, 0)
    for dj in range(n_diag):
        if dj + 1 < n_diag:
            scores(n_off + dj + 1, (dj + 1) % 2)
        softmax_pv(n_off + dj, dj % 2, (krow + dj * tk) <= qcol)
    acc = acc_ref[...]
    o = jnp.concatenate(
        [acc[:HEAD_DIM, hh * tq:(hh + 1) * tq] / acc[HEAD_DIM:HEAD_DIM + 1, hh * tq:(hh + 1) * tq]
         for hh in range(2)], axis=0)
    o_ref[0] = o.T.astype(BF16)


def _fox(qT, qb, k, kb, vT, tq, tk):
    bsz, _, seq = qT.shape
    n_pair = N_MIX_HEADS // 2
    return pl.pallas_call(
        functools.partial(_fox_kernel, tq=tq, tk=tk),
        grid=(bsz, n_pair, seq // tq),
        in_specs=[
            pl.BlockSpec((1, 2 * HEAD_DIM, tq), lambda b, p, i: (b, p, i)),
            pl.BlockSpec((1, 2, BIAS_ROWS, tq), lambda b, p, i: (b, p, 0, i)),
            pl.BlockSpec((1, seq, LANES), lambda b, p, i: (b, 0, p)),
            pl.BlockSpec((1, seq, LANES), lambda b, p, i: (b, 0, 0)),
            pl.BlockSpec((1, 2, V_ROWS, seq), lambda b, p, i: (b, p, 0, 0)),
        ],
        out_specs=pl.BlockSpec((1, tq, LANES), lambda b, p, i: (b, i, p)),
        out_shape=jax.ShapeDtypeStruct((bsz, seq, D_MIX), BF16),
        scratch_shapes=[pltpu.VMEM((1, 2 * tq), F32), pltpu.VMEM((V_ROWS, 2 * tq), F32),
                        pltpu.VMEM((2, tk, 2 * tq), F32)],
        compiler_params=pltpu.CompilerParams(
            dimension_semantics=("arbitrary", "arbitrary", "arbitrary"),
            vmem_limit_bytes=VMEM_LIMIT),
        name="fox_attention",
    )(qT, qb, k, kb, vT)


def kernel(x, mem, a_w_in, a_sgu_ln_g, a_sgu_ln_b, a_w_s, a_b_s, kv_w, kv_b_f, b_w_q,
           mem_w_kv, w_o, ln_g, ln_b, w_up, w_down):
    bsz, seq, d = x.shape
    t = bsz * seq
    scale = 1.0 / math.sqrt(HEAD_DIM)

    wkT = (jnp.swapaxes(mem_w_kv[:, :, :D_MEM], 1, 2) * scale).astype(BF16)
    wv = mem_w_kv[:, :, D_MEM:].astype(BF16)
    w_in = a_w_in[0].astype(BF16)
    ws_pair = (a_w_s[0].reshape(N_MIX_HEADS // 2, 2, CHUNK, CHUNK)
               .transpose(0, 2, 1, 3).reshape(N_MIX_HEADS // 2, CHUNK, 2 * CHUNK))
    bs_full = jnp.repeat(a_b_s[0].T, HEAD_DIM, axis=1)
    wq = b_w_q[0]
    wnt = jnp.concatenate([wq[:, :D_MIX].T * scale, kv_w[:, D_MIX:2 * D_MIX].T], axis=0).astype(BF16)
    wf = kv_w[:, 2 * D_MIX:]
    zf = jnp.zeros((d, 16 - N_MIX_HEADS), F32)
    wf3 = jnp.concatenate([wf, zf, wf, zf, wf, zf, jnp.zeros((d, LANES - 48), F32)], axis=1)
    wnn = jnp.concatenate([kv_w[:, :D_MIX], wq[:, D_MIX:], wf3], axis=1).astype(BF16)
    zb = jnp.zeros((16 - N_MIX_HEADS,), F32)
    bf3 = jnp.concatenate([kv_b_f, zb, kv_b_f, zb, kv_b_f, zb, jnp.zeros((LANES - 48,), F32)])[None, :]
    w_o_b = w_o.astype(BF16)
    w_up_b = w_up.astype(BF16)
    w_down_b = w_down.astype(BF16)
    row = lambda v: v[None, :]

    mkT, mvb = _mem_kv(mem, wkT, wv)
    x2d = x.reshape(t, d)

    tm_a = min(256, seq)
    mix, qm = _sgu(x2d, w_in, row(a_sgu_ln_g[0]), row(a_sgu_ln_b[0]), ws_pair, bs_full, tm_a)
    x2d = _tail(mix, qm, x2d, mkT[0], mvb[0], w_o_b[0], row(ln_g[0, 0]), row(ln_b[0, 0]), seq, tm_a)
    tm_m = min(512, seq)
    x2d = _mlp(x2d, w_up_b[0], w_down_b[0], row(ln_g[0, 1]), row(ln_b[0, 1]), tm_m)

    tm_b = min(512, seq)
    qT, qb, k, kb, vT, qm = _bproj(x2d.reshape(bsz, seq, d), wnt, wnn, bf3, tm_b)
    tq = min(512, seq)
    att = _fox(qT, qb, k, kb, vT, tq, min(256, seq)).reshape(t, D_MIX)
    x2d = _tail(att, qm, x2d, mkT[1], mvb[1], w_o_b[1], row(ln_g[1, 0]), row(ln_b[1, 0]), seq, tm_a)
    x2d = _mlp(x2d, w_up_b[1], w_down_b[1], row(ln_g[1, 1]), row(ln_b[1, 1]), tm_m)
    return x2d.reshape(bsz, seq, d)
```

```python
import functools
import math

import jax
import jax.numpy as jnp
from jax import lax
from jax.experimental import pallas as pl
from jax.experimental.pallas import tpu as pltpu

D_MODEL = 1024
HEAD_DIM = 64
N_MIX_HEADS = 12
N_MEM_HEADS = 4
D_MIX = N_MIX_HEADS * HEAD_DIM
D_MEM = N_MEM_HEADS * HEAD_DIM
CHUNK = 128
N_MEM_TOKENS = 256
D_FF = 4 * D_MODEL
DEPTH = 2
DN_ALPHA = (2 * DEPTH) ** 0.25
LN_EPS = 1e-5

LANES = 128
V_ROWS = 80
BIAS_ROWS = 64
NEG_BIG = -1e30
LOG2E = math.log2(math.e)

F32 = jnp.float32
BF16 = jnp.bfloat16

VMEM_LIMIT = 56 * 1024 * 1024


def _const_spec(shape):
    nd = len(shape)
    return pl.BlockSpec(shape, lambda *_: (0,) * nd, pipeline_mode=pl.Buffered(1))


def _dot(a, b):
    return jnp.dot(a, b, preferred_element_type=F32)


def _dot_nt(a, b):
    return lax.dot_general(a, b, (((1,), (1,)), ((), ())), preferred_element_type=F32)


def _layer_norm(r, g, b):
    mu = jnp.mean(r, axis=-1, keepdims=True)
    d = r - mu
    var = jnp.mean(d * d, axis=-1, keepdims=True)
    return d * lax.rsqrt(var + LN_EPS) * g + b


def _split3(c):
    hi = c.astype(BF16).astype(F32)
    r1 = c - hi
    mid = r1.astype(BF16).astype(F32)
    lo = (r1 - mid).astype(BF16).astype(F32)
    return hi, mid, lo


def _mem_kv_kernel(mem_ref, wkT_ref, wv_ref, mkT_ref, mvb_ref):
    memb = mem_ref[0].astype(BF16)
    kT = _dot_nt(wkT_ref[0], memb)
    kT4 = jnp.concatenate([kT] * N_MEM_HEADS, axis=1)
    row = lax.broadcasted_iota(jnp.int32, kT4.shape, 0)
    col = lax.broadcasted_iota(jnp.int32, kT4.shape, 1)
    same = (row >> 6) == (col >> 8)
    mkT_ref[0, 0] = jnp.where(same, kT4, 0.0).astype(BF16)
    v = _dot(memb, wv_ref[0])
    v4 = jnp.concatenate([v] * N_MEM_HEADS, axis=0)
    row = lax.broadcasted_iota(jnp.int32, v4.shape, 0)
    col = lax.broadcasted_iota(jnp.int32, v4.shape, 1)
    same = (row >> 8) == (col >> 6)
    mvb_ref[0, 0] = jnp.where(same, v4, 0.0).astype(BF16)


def _mem_kv(mem, wkT, wv):
    bsz = mem.shape[0]
    m = N_MEM_TOKENS
    return pl.pallas_call(
        _mem_kv_kernel,
        grid=(DEPTH, bsz),
        in_specs=[
            pl.BlockSpec((1, m, D_MODEL), lambda l, b: (b, 0, 0)),
            pl.BlockSpec((1, D_MEM, D_MODEL), lambda l, b: (l, 0, 0)),
            pl.BlockSpec((1, D_MODEL, D_MEM), lambda l, b: (l, 0, 0)),
        ],
        out_specs=[
            pl.BlockSpec((1, 1, D_MEM, N_MEM_HEADS * m), lambda l, b: (l, b, 0, 0)),
            pl.BlockSpec((1, 1, N_MEM_HEADS * m, D_MEM), lambda l, b: (l, b, 0, 0)),
        ],
        out_shape=[
            jax.ShapeDtypeStruct((DEPTH, bsz, D_MEM, N_MEM_HEADS * m), BF16),
            jax.ShapeDtypeStruct((DEPTH, bsz, N_MEM_HEADS * m, D_MEM), BF16),
        ],
        name="mem_kv",
    )(mem, wkT, wv)


def _sgu_kernel(x_ref, win_ref, g_ref, b_ref, ws_ref, bs_ref, mix_ref, qm_ref, *, tm):
    xb = x_ref[...].astype(BF16)
    z = _dot(xb, win_ref[...])
    zu = jax.nn.gelu(z[:, :D_MIX], approximate=True)
    zv = jax.nn.gelu(z[:, D_MIX:2 * D_MIX], approximate=True)
    zv = _layer_norm(zv, g_ref[...], b_ref[...])
    qm_ref[...] = z[:, 2 * D_MIX:].astype(BF16)

    lane = lax.broadcasted_iota(jnp.int32, (CHUNK, LANES), 1)
    low = lane < HEAD_DIM
    wrow = lax.broadcasted_iota(jnp.int32, (CHUNK, 2 * CHUNK), 0)
    wcol = lax.broadcasted_iota(jnp.int32, (CHUNK, 2 * CHUNK), 1) & (CHUNK - 1)
    causal = wcol <= wrow
    for p in range(N_MIX_HEADS // 2):
        w = jnp.where(causal, ws_ref[p], 0.0).astype(BF16)
        bias = bs_ref[:, p * LANES:(p + 1) * LANES]
        for c0 in range(0, tm // CHUNK, 2):
            rhs = []
            for c in (c0, c0 + 1):
                slab = zv[c * CHUNK:(c + 1) * CHUNK, p * LANES:(p + 1) * LANES]
                rhs.append(jnp.concatenate(
                    [jnp.where(low, slab, 0.0), jnp.where(low, 0.0, slab)], axis=0))
            rhs = jnp.concatenate(rhs, axis=1).astype(BF16)
            mixed = _dot(w, rhs)
            for i, c in enumerate((c0, c0 + 1)):
                u = zu[c * CHUNK:(c + 1) * CHUNK, p * LANES:(p + 1) * LANES]
                out = u * (mixed[:, i * LANES:(i + 1) * LANES] + bias)
                mix_ref[c * CHUNK:(c + 1) * CHUNK, p * LANES:(p + 1) * LANES] = out.astype(BF16)


def _sgu(x2d, w_in, ln_g, ln_b, ws_pair, bs_full, tm):
    t = x2d.shape[0]
    n_in = w_in.shape[1]
    return pl.pallas_call(
        functools.partial(_sgu_kernel, tm=tm),
        grid=(t // tm,),
        in_specs=[
            pl.BlockSpec((tm, D_MODEL), lambda i: (i, 0)),
            _const_spec((D_MODEL, n_in)),
            _const_spec((1, D_MIX)),
            _const_spec((1, D_MIX)),
            _const_spec((N_MIX_HEADS // 2, CHUNK, 2 * CHUNK)),
            _const_spec((CHUNK, D_MIX)),
        ],
        out_specs=[pl.BlockSpec((tm, D_MIX), lambda i: (i, 0)),
                   pl.BlockSpec((tm, D_MEM), lambda i: (i, 0))],
        out_shape=[jax.ShapeDtypeStruct((t, D_MIX), BF16),
                   jax.ShapeDtypeStruct((t, D_MEM), BF16)],
        compiler_params=pltpu.CompilerParams(
            dimension_semantics=("arbitrary",), vmem_limit_bytes=VMEM_LIMIT),
        name="sgu_mixer",
    )(x2d, w_in, ln_g, ln_b, ws_pair, bs_full)


def _tail_kernel(mix_ref, qm_ref, x_ref, mkT_ref, mvb_ref, wo_ref, g_ref, b_ref, o_ref):
    m = N_MEM_TOKENS
    s = _dot(qm_ref[...], mkT_ref[0])
    ps = []
    for h in range(N_MEM_HEADS):
        sh = s[:, h * m:(h + 1) * m]
        e = jnp.exp(sh - jnp.max(sh, axis=-1, keepdims=True))
        ps.append((e / jnp.sum(e, axis=-1, keepdims=True)).astype(BF16))
    p = jnp.concatenate(ps, axis=1)
    mo = _dot(p, mvb_ref[0]).astype(BF16)
    y = _dot(mix_ref[...], wo_ref[:D_MIX, :]) + _dot(mo, wo_ref[D_MIX:, :])
    r = DN_ALPHA * x_ref[...] + y
    o_ref[...] = _layer_norm(r, g_ref[...], b_ref[...])


def _tail(mix, qm, x2d, mkT, mvb, w_o, ln_g, ln_b, seq, tm):
    t = x2d.shape[0]
    m = N_MEM_TOKENS
    per_b = seq // tm
    return pl.pallas_call(
        _tail_kernel,
        grid=(t // tm,),
        in_specs=[
            pl.BlockSpec((tm, D_MIX), lambda i: (i, 0)),
            pl.BlockSpec((tm, D_MEM), lambda i: (i, 0)),
            pl.BlockSpec((tm, D_MODEL), lambda i: (i, 0)),
            pl.BlockSpec((1, D_MEM, N_MEM_HEADS * m), lambda i: (i // per_b, 0, 0)),
            pl.BlockSpec((1, N_MEM_HEADS * m, D_MEM), lambda i: (i // per_b, 0, 0)),
            _const_spec((D_MODEL, D_MODEL)),
            _const_spec((1, D_MODEL)),
            _const_spec((1, D_MODEL)),
        ],
        out_specs=pl.BlockSpec((tm, D_MODEL), lambda i: (i, 0)),
        out_shape=jax.ShapeDtypeStruct((t, D_MODEL), F32),
        compiler_params=pltpu.CompilerParams(
            dimension_semantics=("arbitrary",), vmem_limit_bytes=VMEM_LIMIT),
        name="mixer_tail",
    )(mix, qm, x2d, mkT, mvb, w_o, ln_g, ln_b)


def _mlp_kernel(x_ref, wu_ref, wd_ref, g_ref, b_ref, o_ref, h_ref, *, n_chunk):
    x = x_ref[...]
    xb = x.astype(BF16)
    for c in range(D_FF // n_chunk):
        h = _dot(xb, wu_ref[:, c * n_chunk:(c + 1) * n_chunk])
        h = jnp.maximum(h, 0.0)
        h_ref[:, c * n_chunk:(c + 1) * n_chunk] = (h * h).astype(BF16)
    y = _dot(h_ref[...], wd_ref[...])
    r = DN_ALPHA * x + y
    o_ref[...] = _layer_norm(r, g_ref[...], b_ref[...])


def _mlp(x2d, w_up, w_down, ln_g, ln_b, tm, n_chunk=1024):
    t = x2d.shape[0]
    return pl.pallas_call(
        functools.partial(_mlp_kernel, n_chunk=n_chunk),
        grid=(t // tm,),
        in_specs=[
            pl.BlockSpec((tm, D_MODEL), lambda i: (i, 0)),
            _const_spec((D_MODEL, D_FF)),
            _const_spec((D_FF, D_MODEL)),
            _const_spec((1, D_MODEL)),
            _const_spec((1, D_MODEL)),
        ],
        out_specs=pl.BlockSpec((tm, D_MODEL), lambda i: (i, 0)),
        out_shape=jax.ShapeDtypeStruct((t, D_MODEL), F32),
        scratch_shapes=[pltpu.VMEM((tm, D_FF), BF16)],
        compiler_params=pltpu.CompilerParams(
            dimension_semantics=("arbitrary",), vmem_limit_bytes=VMEM_LIMIT),
        name="relu2_mlp",
    )(x2d, w_up, w_down, ln_g, ln_b)


def _log_sigmoid(x):
    return jnp.minimum(x, 0.0) - jnp.log(1.0 + jnp.exp(-jnp.abs(x)))


def _bproj_kernel(x_ref, wnt_ref, wnn_ref, bf_ref,
                  qT_ref, qb_ref, k_ref, kb_ref, vT_ref, qm_ref, carry_ref, *, tm):
    @pl.when(pl.program_id(1) == 0)
    def _():
        carry_ref[...] = jnp.zeros_like(carry_ref)

    xb = x_ref[...].astype(BF16)
    nt = _dot_nt(wnt_ref[...], xb)
    qT_ref[0] = nt[:D_MIX].astype(BF16)
    vrow = lax.broadcasted_iota(jnp.int32, (V_ROWS - HEAD_DIM, tm), 0)
    ones_rows = jnp.where(vrow == 0, 1.0, 0.0).astype(BF16)
    for h in range(N_MIX_HEADS):
        vT_ref[0, h, :HEAD_DIM, :] = nt[D_MIX + h * HEAD_DIM:D_MIX + (h + 1) * HEAD_DIM].astype(BF16)
        vT_ref[0, h, HEAD_DIM:, :] = ones_rows

    nn = _dot(xb, wnn_ref[...])
    k_ref[0] = nn[:, :D_MIX].astype(BF16)
    qm_ref[...] = nn[:, D_MIX:D_MIX + D_MEM].astype(BF16)

    f3 = nn[:, D_MIX + D_MEM:] + bf_ref[...]
    lane = lax.broadcasted_iota(jnp.int32, (tm, LANES), 1)
    valid = (lane & 15) < N_MIX_HEADS
    valid = valid & (lane < 48)
    c = jnp.where(valid, _log_sigmoid(f3) * LOG2E, 0.0)
    rowi = lax.broadcasted_iota(jnp.int32, (tm, LANES), 0)
    d = 1
    while d < tm:
        c = c + jnp.where(rowi >= d, pltpu.roll(c, d, axis=0), 0.0)
        d *= 2
    c = c + carry_ref[...]
    carry_ref[...] = c[tm - 1:tm, :]

    hi, mid, lo = _split3(c)
    kb = jnp.where(lane < 16, -hi,
                   jnp.where(lane < 32, -mid,
                             jnp.where(lane < 48, -lo,
                                       jnp.where(lane < 51, 1.0, 0.0))))
    kb_ref[0] = kb.astype(BF16)

    cT = c.T
    brow = lax.broadcasted_iota(jnp.int32, (BIAS_ROWS, tm), 0)
    for h in range(N_MIX_HEADS):
        chi, cmid, clo = _split3(cT[h:h + 1, :])
        sel = (brow == h) | (brow == 16 + h) | (brow == 32 + h)
        tile = jnp.where(brow == 48, chi,
                         jnp.where(brow == 49, cmid,
                                   jnp.where(brow == 50, clo,
                                             jnp.where(sel, 1.0, 0.0))))
        qb_ref[0, h] = tile.astype(BF16)


def _bproj(x3d, wnt, wnn, bf3, tm):
    bsz, seq, _ = x3d.shape
    n_nn = wnn.shape[1]
    return pl.pallas_call(
        functools.partial(_bproj_kernel, tm=tm),
        grid=(bsz, seq // tm),
        in_specs=[
            pl.BlockSpec((None, tm, D_MODEL), lambda b, j: (b, j, 0)),
            _const_spec((2 * D_MIX, D_MODEL)),
            _const_spec((D_MODEL, n_nn)),
            _const_spec((1, LANES)),
        ],
        out_specs=[
            pl.BlockSpec((1, D_MIX, tm), lambda b, j: (b, 0, j)),
            pl.BlockSpec((1, N_MIX_HEADS, BIAS_ROWS, tm), lambda b, j: (b, 0, 0, j)),
            pl.BlockSpec((1, tm, D_MIX), lambda b, j: (b, j, 0)),
            pl.BlockSpec((1, tm, LANES), lambda b, j: (b, j, 0)),
            pl.BlockSpec((1, N_MIX_HEADS, V_ROWS, tm), lambda b, j: (b, 0, 0, j)),
            pl.BlockSpec((tm, D_MEM), lambda b, j: (b * (seq // tm) + j, 0)),
        ],
        out_shape=[
            jax.ShapeDtypeStruct((bsz, D_MIX, seq), BF16),
            jax.ShapeDtypeStruct((bsz, N_MIX_HEADS, BIAS_ROWS, seq), BF16),
            jax.ShapeDtypeStruct((bsz, seq, D_MIX), BF16),
            jax.ShapeDtypeStruct((bsz, seq, LANES), BF16),
            jax.ShapeDtypeStruct((bsz, N_MIX_HEADS, V_ROWS, seq), BF16),
            jax.ShapeDtypeStruct((bsz * seq, D_MEM), BF16),
        ],
        scratch_shapes=[pltpu.VMEM((1, LANES), F32)],
        compiler_params=pltpu.CompilerParams(
            dimension_semantics=("arbitrary", "arbitrary"), vmem_limit_bytes=VMEM_LIMIT),
        name="fox_proj",
    )(x3d, wnt, wnn, bf3)


def _fox_kernel(qT_ref, qb_ref, k_ref, kb_ref, vT_ref, o_ref, m_ref, acc_ref, s_ref, *, tq, tk):
    qi = pl.program_id(2)
    n_sub = tq // tk
    assert n_sub % 2 == 0
    gw = 2 * tk
    width = n_sub * gw

    slab = qT_ref[0]
    srow = lax.broadcasted_iota(jnp.int32, (2 * HEAD_DIM, tk), 0)
    zpad = jnp.zeros((LANES - BIAS_ROWS, tk), BF16)
    pieces = []
    for sub in range(n_sub):
        cols = slice(sub * tk, (sub + 1) * tk)
        sl = slab[:, cols]
        for hh in range(2):
            keep = (srow < HEAD_DIM) if hh == 0 else (srow >= HEAD_DIM)
            pieces.append(jnp.concatenate(
                [jnp.where(keep, sl, jnp.zeros_like(sl)), qb_ref[0, hh, :, cols], zpad], axis=0))
    qa = jnp.concatenate(pieces, axis=1)

    m_ref[...] = jnp.full(m_ref.shape, NEG_BIG, F32)
    acc_ref[...] = jnp.zeros(acc_ref.shape, F32)

    def scores(i, slot, col0):
        ks = pl.multiple_of(i * tk, tk)
        ka = jnp.concatenate([k_ref[0, pl.ds(ks, tk), :], kb_ref[0, pl.ds(ks, tk), :]], axis=1)
        s_ref[slot, :, col0:] = _dot(ka, qa[:, col0:])

    def softmax_pv(i, slot, col0, diagonal):
        ks = pl.multiple_of(i * tk, tk)
        s = s_ref[slot, :, col0:]
        if diagonal:
            row = lax.broadcasted_iota(jnp.int32, s.shape, 0)
            col = lax.broadcasted_iota(jnp.int32, s.shape, 1)
            s = jnp.where((col >= gw) | (row <= (col & (tk - 1))), s, NEG_BIG)
        m_prev = m_ref[:, col0:]
        m_new = jnp.maximum(m_prev, jnp.max(s, axis=0, keepdims=True))
        m_ref[:, col0:] = m_new
        alpha = jnp.exp2(m_prev - m_new)
        p = jnp.exp2((s - m_new).astype(BF16))
        pv = jnp.concatenate(
            [_dot(vT_ref[0, g % 2, :, pl.ds(ks, tk)], p[:, g * tk:(g + 1) * tk])
             for g in range((width - col0) // tk)], axis=1)
        acc_ref[:, col0:] = alpha * acc_ref[:, col0:] + pv

    n_off = qi * n_sub
    scores(0, 0, 0)

    def body(j, carry):
        scores(2 * j + 1, 1, 0)
        softmax_pv(2 * j, 0, 0, False)
        scores(2 * j + 2, 0, 0)
        softmax_pv(2 * j + 1, 1, 0, False)
        return carry

    lax.fori_loop(0, n_off // 2, body, 0)
    for dj in range(n_sub):
        if dj + 1 < n_sub:
            scores(n_off + dj + 1, (dj + 1) % 2, (dj + 1) * gw)
        softmax_pv(n_off + dj, dj % 2, dj * gw, True)

    acc = acc_ref[...]
    o = acc[:HEAD_DIM] * (1.0 / acc[HEAD_DIM:HEAD_DIM + 1])
    o = jnp.concatenate(
        [jnp.concatenate([o[:, (2 * sub + hh) * tk:(2 * sub + hh + 1) * tk] for sub in range(n_sub)],
                         axis=1) for hh in range(2)], axis=0)
    o_ref[0] = o.T.astype(BF16)


def _fox(qT, qb, k, kb, vT, tq, tk):
    bsz, _, seq = qT.shape
    n_pair = N_MIX_HEADS // 2
    return pl.pallas_call(
        functools.partial(_fox_kernel, tq=tq, tk=tk),
        grid=(bsz, n_pair, seq // tq),
        in_specs=[
            pl.BlockSpec((1, 2 * HEAD_DIM, tq), lambda b, p, i: (b, p, i)),
            pl.BlockSpec((1, 2, BIAS_ROWS, tq), lambda b, p, i: (b, p, 0, i)),
            pl.BlockSpec((1, seq, LANES), lambda b, p, i: (b, 0, p)),
            pl.BlockSpec((1, seq, LANES), lambda b, p, i: (b, 0, 0)),
            pl.BlockSpec((1, 2, V_ROWS, seq), lambda b, p, i: (b, p, 0, 0)),
        ],
        out_specs=pl.BlockSpec((1, tq, LANES), lambda b, p, i: (b, i, p)),
        out_shape=jax.ShapeDtypeStruct((bsz, seq, D_MIX), BF16),
        scratch_shapes=[pltpu.VMEM((1, 2 * tq), F32), pltpu.VMEM((V_ROWS, 2 * tq), F32),
                        pltpu.VMEM((2, tk, 2 * tq), F32)],
        compiler_params=pltpu.CompilerParams(
            dimension_semantics=("arbitrary", "arbitrary", "arbitrary"),
            vmem_limit_bytes=VMEM_LIMIT),
        name="fox_attention",
    )(qT, qb, k, kb, vT)


def kernel(x, mem, a_w_in, a_sgu_ln_g, a_sgu_ln_b, a_w_s, a_b_s, kv_w, kv_b_f, b_w_q,
           mem_w_kv, w_o, ln_g, ln_b, w_up, w_down):
    bsz, seq, d = x.shape
    t = bsz * seq
    scale = 1.0 / math.sqrt(HEAD_DIM)

    wkT = (jnp.swapaxes(mem_w_kv[:, :, :D_MEM], 1, 2) * scale).astype(BF16)
    wv = mem_w_kv[:, :, D_MEM:].astype(BF16)
    w_in = a_w_in[0].astype(BF16)
    ws_pair = (a_w_s[0].reshape(N_MIX_HEADS // 2, 2, CHUNK, CHUNK)
               .transpose(0, 2, 1, 3).reshape(N_MIX_HEADS // 2, CHUNK, 2 * CHUNK))
    bs_full = jnp.repeat(a_b_s[0].T, HEAD_DIM, axis=1)
    wq = b_w_q[0]
    wnt = jnp.concatenate([wq[:, :D_MIX].T * (scale * LOG2E), kv_w[:, D_MIX:2 * D_MIX].T], axis=0).astype(BF16)
    wf = kv_w[:, 2 * D_MIX:]
    zf = jnp.zeros((d, 16 - N_MIX_HEADS), F32)
    wf3 = jnp.concatenate([wf, zf, wf, zf, wf, zf, jnp.zeros((d, LANES - 48), F32)], axis=1)
    wnn = jnp.concatenate([kv_w[:, :D_MIX], wq[:, D_MIX:], wf3], axis=1).astype(BF16)
    zb = jnp.zeros((16 - N_MIX_HEADS,), F32)
    bf3 = jnp.concatenate([kv_b_f, zb, kv_b_f, zb, kv_b_f, zb, jnp.zeros((LANES - 48,), F32)])[None, :]
    w_o_b = w_o.astype(BF16)
    w_up_b = w_up.astype(BF16)
    w_down_b = w_down.astype(BF16)
    row = lambda v: v[None, :]

    mkT, mvb = _mem_kv(mem, wkT, wv)
    x2d = x.reshape(t, d)

    tm_a = min(256, seq)
    mix, qm = _sgu(x2d, w_in, row(a_sgu_ln_g[0]), row(a_sgu_ln_b[0]), ws_pair, bs_full, tm_a)
    x2d = _tail(mix, qm, x2d, mkT[0], mvb[0], w_o_b[0], row(ln_g[0, 0]), row(ln_b[0, 0]), seq, tm_a)
    tm_m = min(512, seq)
    x2d = _mlp(x2d, w_up_b[0], w_down_b[0], row(ln_g[0, 1]), row(ln_b[0, 1]), tm_m)

    tm_b = min(512, seq)
    qT, qb, k, kb, vT, qm = _bproj(x2d.reshape(bsz, seq, d), wnt, wnn, bf3, tm_b)
    tq = min(1024, seq)
    att = _fox(qT, qb, k, kb, vT, tq, min(256, seq)).reshape(t, D_MIX)
    x2d = _tail(att, qm, x2d, mkT[1], mvb[1], w_o_b[1], row(ln_g[1, 0]), row(ln_b[1, 0]), seq, tm_a)
    x2d = _mlp(x2d, w_up_b[1], w_down_b[1], row(ln_g[1, 1]), row(ln_b[1, 1]), tm_m)
    return x2d.reshape(bsz, seq, d)
```

```python
import functools
import math

import jax
import jax.numpy as jnp
from jax import lax
from jax.experimental import pallas as pl
from jax.experimental.pallas import tpu as pltpu

D_MODEL = 1024
HEAD_DIM = 64
N_MIX_HEADS = 12
N_MEM_HEADS = 4
D_MIX = N_MIX_HEADS * HEAD_DIM
D_MEM = N_MEM_HEADS * HEAD_DIM
CHUNK = 128
N_MEM_TOKENS = 256
D_FF = 4 * D_MODEL
DEPTH = 2
DN_ALPHA = (2 * DEPTH) ** 0.25
LN_EPS = 1e-5

LANES = 128
V_ROWS = 80
BIAS_ROWS = 64
NEG_BIG = -1e30
LOG2E = math.log2(math.e)

F32 = jnp.float32
BF16 = jnp.bfloat16

VMEM_LIMIT = 56 * 1024 * 1024


def _const_spec(shape):
    nd = len(shape)
    return pl.BlockSpec(shape, lambda *_: (0,) * nd, pipeline_mode=pl.Buffered(1))


def _dot(a, b):
    return jnp.dot(a, b, preferred_element_type=F32)


def _dot_nt(a, b):
    return lax.dot_general(a, b, (((1,), (1,)), ((), ())), preferred_element_type=F32)


def _layer_norm(r, g, b):
    mu = jnp.mean(r, axis=-1, keepdims=True)
    d = r - mu
    var = jnp.mean(d * d, axis=-1, keepdims=True)
    return d * lax.rsqrt(var + LN_EPS) * g + b


def _split3(c):
    hi = c.astype(BF16).astype(F32)
    r1 = c - hi
    mid = r1.astype(BF16).astype(F32)
    lo = (r1 - mid).astype(BF16).astype(F32)
    return hi, mid, lo


def _mem_kv_kernel(mem_ref, wkT_ref, wv_ref, mkT_ref, mvb_ref):
    memb = mem_ref[0].astype(BF16)
    kT = _dot_nt(wkT_ref[0], memb)
    kT4 = jnp.concatenate([kT] * N_MEM_HEADS, axis=1)
    row = lax.broadcasted_iota(jnp.int32, kT4.shape, 0)
    col = lax.broadcasted_iota(jnp.int32, kT4.shape, 1)
    same = (row >> 6) == (col >> 8)
    mkT_ref[0, 0] = jnp.where(same, kT4, 0.0).astype(BF16)
    v = _dot(memb, wv_ref[0])
    v4 = jnp.concatenate([v] * N_MEM_HEADS, axis=0)
    row = lax.broadcasted_iota(jnp.int32, v4.shape, 0)
    col = lax.broadcasted_iota(jnp.int32, v4.shape, 1)
    same = (row >> 8) == (col >> 6)
    mvb_ref[0, 0] = jnp.where(same, v4, 0.0).astype(BF16)


def _mem_kv(mem, wkT, wv):
    bsz = mem.shape[0]
    m = N_MEM_TOKENS
    return pl.pallas_call(
        _mem_kv_kernel,
        grid=(DEPTH, bsz),
        in_specs=[
            pl.BlockSpec((1, m, D_MODEL), lambda l, b: (b, 0, 0)),
            pl.BlockSpec((1, D_MEM, D_MODEL), lambda l, b: (l, 0, 0)),
            pl.BlockSpec((1, D_MODEL, D_MEM), lambda l, b: (l, 0, 0)),
        ],
        out_specs=[
            pl.BlockSpec((1, 1, D_MEM, N_MEM_HEADS * m), lambda l, b: (l, b, 0, 0)),
            pl.BlockSpec((1, 1, N_MEM_HEADS * m, D_MEM), lambda l, b: (l, b, 0, 0)),
        ],
        out_shape=[
            jax.ShapeDtypeStruct((DEPTH, bsz, D_MEM, N_MEM_HEADS * m), BF16),
            jax.ShapeDtypeStruct((DEPTH, bsz, N_MEM_HEADS * m, D_MEM), BF16),
        ],
        name="mem_kv",
    )(mem, wkT, wv)


def _sgu_kernel(x_ref, win_ref, g_ref, b_ref, ws_ref, bs_ref, mix_ref, qm_ref, *, tm):
    xb = x_ref[...].astype(BF16)
    z = _dot(xb, win_ref[...])
    zu = jax.nn.gelu(z[:, :D_MIX], approximate=True)
    zv = jax.nn.gelu(z[:, D_MIX:2 * D_MIX], approximate=True)
    zv = _layer_norm(zv, g_ref[...], b_ref[...])
    qm_ref[...] = z[:, 2 * D_MIX:].astype(BF16)

    lane = lax.broadcasted_iota(jnp.int32, (CHUNK, LANES), 1)
    low = lane < HEAD_DIM
    wrow = lax.broadcasted_iota(jnp.int32, (CHUNK, 2 * CHUNK), 0)
    wcol = lax.broadcasted_iota(jnp.int32, (CHUNK, 2 * CHUNK), 1) & (CHUNK - 1)
    causal = wcol <= wrow
    for p in range(N_MIX_HEADS // 2):
        w = jnp.where(causal, ws_ref[p], 0.0).astype(BF16)
        bias = bs_ref[:, p * LANES:(p + 1) * LANES]
        for c0 in range(0, tm // CHUNK, 2):
            rhs = []
            for c in (c0, c0 + 1):
                slab = zv[c * CHUNK:(c + 1) * CHUNK, p * LANES:(p + 1) * LANES]
                rhs.append(jnp.concatenate(
                    [jnp.where(low, slab, 0.0), jnp.where(low, 0.0, slab)], axis=0))
            rhs = jnp.concatenate(rhs, axis=1).astype(BF16)
            mixed = _dot(w, rhs)
            for i, c in enumerate((c0, c0 + 1)):
                u = zu[c * CHUNK:(c + 1) * CHUNK, p * LANES:(p + 1) * LANES]
                out = u * (mixed[:, i * LANES:(i + 1) * LANES] + bias)
                mix_ref[c * CHUNK:(c + 1) * CHUNK, p * LANES:(p + 1) * LANES] = out.astype(BF16)


def _sgu(x2d, w_in, ln_g, ln_b, ws_pair, bs_full, tm):
    t = x2d.shape[0]
    n_in = w_in.shape[1]
    return pl.pallas_call(
        functools.partial(_sgu_kernel, tm=tm),
        grid=(t // tm,),
        in_specs=[
            pl.BlockSpec((tm, D_MODEL), lambda i: (i, 0)),
            _const_spec((D_MODEL, n_in)),
            _const_spec((1, D_MIX)),
            _const_spec((1, D_MIX)),
            _const_spec((N_MIX_HEADS // 2, CHUNK, 2 * CHUNK)),
            _const_spec((CHUNK, D_MIX)),
        ],
        out_specs=[pl.BlockSpec((tm, D_MIX), lambda i: (i, 0)),
                   pl.BlockSpec((tm, D_MEM), lambda i: (i, 0))],
        out_shape=[jax.ShapeDtypeStruct((t, D_MIX), BF16),
                   jax.ShapeDtypeStruct((t, D_MEM), BF16)],
        compiler_params=pltpu.CompilerParams(
            dimension_semantics=("arbitrary",), vmem_limit_bytes=VMEM_LIMIT),
        name="sgu_mixer",
    )(x2d, w_in, ln_g, ln_b, ws_pair, bs_full)


def _tail_kernel(mix_ref, qm_ref, x_ref, mkT_ref, mvb_ref, wo_ref, g_ref, b_ref, o_ref):
    m = N_MEM_TOKENS
    s = _dot(qm_ref[...], mkT_ref[0])
    ps = []
    for h in range(N_MEM_HEADS):
        sh = s[:, h * m:(h + 1) * m]
        e = jnp.exp(sh - jnp.max(sh, axis=-1, keepdims=True))
        ps.append((e / jnp.sum(e, axis=-1, keepdims=True)).astype(BF16))
    p = jnp.concatenate(ps, axis=1)
    mo = _dot(p, mvb_ref[0]).astype(BF16)
    y = _dot(mix_ref[...], wo_ref[:D_MIX, :]) + _dot(mo, wo_ref[D_MIX:, :])
    r = DN_ALPHA * x_ref[...] + y
    o_ref[...] = _layer_norm(r, g_ref[...], b_ref[...])


def _tail(mix, qm, x2d, mkT, mvb, w_o, ln_g, ln_b, seq, tm):
    t = x2d.shape[0]
    m = N_MEM_TOKENS
    per_b = seq // tm
    return pl.pallas_call(
        _tail_kernel,
        grid=(t // tm,),
        in_specs=[
            pl.BlockSpec((tm, D_MIX), lambda i: (i, 0)),
            pl.BlockSpec((tm, D_MEM), lambda i: (i, 0)),
            pl.BlockSpec((tm, D_MODEL), lambda i: (i, 0)),
            pl.BlockSpec((1, D_MEM, N_MEM_HEADS * m), lambda i: (i // per_b, 0, 0)),
            pl.BlockSpec((1, N_MEM_HEADS * m, D_MEM), lambda i: (i // per_b, 0, 0)),
            _const_spec((D_MODEL, D_MODEL)),
            _const_spec((1, D_MODEL)),
            _const_spec((1, D_MODEL)),
        ],
        out_specs=pl.BlockSpec((tm, D_MODEL), lambda i: (i, 0)),
        out_shape=jax.ShapeDtypeStruct((t, D_MODEL), F32),
        compiler_params=pltpu.CompilerParams(
            dimension_semantics=("arbitrary",), vmem_limit_bytes=VMEM_LIMIT),
        name="mixer_tail",
    )(mix, qm, x2d, mkT, mvb, w_o, ln_g, ln_b)


def _mlp_kernel(x_ref, wu_ref, wd_ref, g_ref, b_ref, o_ref, h_ref, *, n_chunk):
    x = x_ref[...]
    xb = x.astype(BF16)
    for c in range(D_FF // n_chunk):
        h = _dot(xb, wu_ref[:, c * n_chunk:(c + 1) * n_chunk])
        h = jnp.maximum(h, 0.0)
        h_ref[:, c * n_chunk:(c + 1) * n_chunk] = (h * h).astype(BF16)
    y = _dot(h_ref[...], wd_ref[...])
    r = DN_ALPHA * x + y
    o_ref[...] = _layer_norm(r, g_ref[...], b_ref[...])


def _mlp(x2d, w_up, w_down, ln_g, ln_b, tm, n_chunk=1024):
    t = x2d.shape[0]
    return pl.pallas_call(
        functools.partial(_mlp_kernel, n_chunk=n_chunk),
        grid=(t // tm,),
        in_specs=[
            pl.BlockSpec((tm, D_MODEL), lambda i: (i, 0)),
            _const_spec((D_MODEL, D_FF)),
            _const_spec((D_FF, D_MODEL)),
            _const_spec((1, D_MODEL)),
            _const_spec((1, D_MODEL)),
        ],
        out_specs=pl.BlockSpec((tm, D_MODEL), lambda i: (i, 0)),
        out_shape=jax.ShapeDtypeStruct((t, D_MODEL), F32),
        scratch_shapes=[pltpu.VMEM((tm, D_FF), BF16)],
        compiler_params=pltpu.CompilerParams(
            dimension_semantics=("arbitrary",), vmem_limit_bytes=VMEM_LIMIT),
        name="relu2_mlp",
    )(x2d, w_up, w_down, ln_g, ln_b)


def _log_sigmoid(x):
    return jnp.minimum(x, 0.0) - jnp.log(1.0 + jnp.exp(-jnp.abs(x)))


def _bproj_kernel(x_ref, wnt_ref, wnn_ref, bf_ref,
                  qT_ref, qb_ref, ka_ref, vT_ref, qm_ref, carry_ref, *, tm):
    @pl.when(pl.program_id(1) == 0)
    def _():
        carry_ref[...] = jnp.zeros_like(carry_ref)

    xb = x_ref[...].astype(BF16)
    nt = _dot_nt(wnt_ref[...], xb)
    qT_ref[0] = nt[:D_MIX].astype(BF16)
    vrow = lax.broadcasted_iota(jnp.int32, (V_ROWS - HEAD_DIM, tm), 0)
    ones_rows = jnp.where(vrow == 0, 1.0, 0.0).astype(BF16)
    for h in range(N_MIX_HEADS):
        vT_ref[0, h, :HEAD_DIM, :] = nt[D_MIX + h * HEAD_DIM:D_MIX + (h + 1) * HEAD_DIM].astype(BF16)
        vT_ref[0, h, HEAD_DIM:, :] = ones_rows

    nn = _dot(xb, wnn_ref[...])
    qm_ref[...] = nn[:, D_MIX:D_MIX + D_MEM].astype(BF16)

    f3 = nn[:, D_MIX + D_MEM:] + bf_ref[...]
    lane = lax.broadcasted_iota(jnp.int32, (tm, LANES), 1)
    valid = (lane & 15) < N_MIX_HEADS
    valid = valid & (lane < 48)
    c = jnp.where(valid, _log_sigmoid(f3) * LOG2E, 0.0)
    rowi = lax.broadcasted_iota(jnp.int32, (tm, LANES), 0)
    d = 1
    while d < tm:
        c = c + jnp.where(rowi >= d, pltpu.roll(c, d, axis=0), 0.0)
        d *= 2
    c = c + carry_ref[...]
    carry_ref[...] = c[tm - 1:tm, :]

    hi, mid, lo = _split3(c)
    kb = jnp.where(lane < 16, -hi,
                   jnp.where(lane < 32, -mid,
                             jnp.where(lane < 48, -lo,
                                       jnp.where(lane < 51, 1.0, 0.0))))
    kb = kb.astype(BF16)
    for p in range(N_MIX_HEADS // 2):
        ka_ref[0, p, :, :LANES] = nn[:, p * LANES:(p + 1) * LANES].astype(BF16)
        ka_ref[0, p, :, LANES:] = kb

    cT = c.T
    brow = lax.broadcasted_iota(jnp.int32, (BIAS_ROWS, tm), 0)
    for h in range(N_MIX_HEADS):
        chi, cmid, clo = _split3(cT[h:h + 1, :])
        sel = (brow == h) | (brow == 16 + h) | (brow == 32 + h)
        tile = jnp.where(brow == 48, chi,
                         jnp.where(brow == 49, cmid,
                                   jnp.where(brow == 50, clo,
                                             jnp.where(sel, 1.0, 0.0))))
        qb_ref[0, h] = tile.astype(BF16)


def _bproj(x3d, wnt, wnn, bf3, tm):
    bsz, seq, _ = x3d.shape
    n_nn = wnn.shape[1]
    return pl.pallas_call(
        functools.partial(_bproj_kernel, tm=tm),
        grid=(bsz, seq // tm),
        in_specs=[
            pl.BlockSpec((None, tm, D_MODEL), lambda b, j: (b, j, 0)),
            _const_spec((2 * D_MIX, D_MODEL)),
            _const_spec((D_MODEL, n_nn)),
            _const_spec((1, LANES)),
        ],
        out_specs=[
            pl.BlockSpec((1, D_MIX, tm), lambda b, j: (b, 0, j)),
            pl.BlockSpec((1, N_MIX_HEADS, BIAS_ROWS, tm), lambda b, j: (b, 0, 0, j)),
            pl.BlockSpec((1, N_MIX_HEADS // 2, tm, 2 * LANES), lambda b, j: (b, 0, j, 0)),
            pl.BlockSpec((1, N_MIX_HEADS, V_ROWS, tm), lambda b, j: (b, 0, 0, j)),
            pl.BlockSpec((tm, D_MEM), lambda b, j: (b * (seq // tm) + j, 0)),
        ],
        out_shape=[
            jax.ShapeDtypeStruct((bsz, D_MIX, seq), BF16),
            jax.ShapeDtypeStruct((bsz, N_MIX_HEADS, BIAS_ROWS, seq), BF16),
            jax.ShapeDtypeStruct((bsz, N_MIX_HEADS // 2, seq, 2 * LANES), BF16),
            jax.ShapeDtypeStruct((bsz, N_MIX_HEADS, V_ROWS, seq), BF16),
            jax.ShapeDtypeStruct((bsz * seq, D_MEM), BF16),
        ],
        scratch_shapes=[pltpu.VMEM((1, LANES), F32)],
        compiler_params=pltpu.CompilerParams(
            dimension_semantics=("arbitrary", "arbitrary"), vmem_limit_bytes=VMEM_LIMIT),
        name="fox_proj",
    )(x3d, wnt, wnn, bf3)


def _fox_kernel(qT_ref, qb_ref, ka_ref, vT_ref, o_ref, m_ref, acc_ref, s_ref, *, tq, tk):
    qi = pl.program_id(2)
    n_sub = tq // tk
    assert n_sub % 2 == 0
    gw = 2 * tk
    width = n_sub * gw

    slab = qT_ref[0]
    srow = lax.broadcasted_iota(jnp.int32, (2 * HEAD_DIM, tk), 0)
    zpad = jnp.zeros((LANES - BIAS_ROWS, tk), BF16)
    pieces = []
    for sub in range(n_sub):
        cols = slice(sub * tk, (sub + 1) * tk)
        sl = slab[:, cols]
        for hh in range(2):
            keep = (srow < HEAD_DIM) if hh == 0 else (srow >= HEAD_DIM)
            pieces.append(jnp.concatenate(
                [jnp.where(keep, sl, jnp.zeros_like(sl)), qb_ref[0, hh, :, cols], zpad], axis=0))
    qa = jnp.concatenate(pieces, axis=1)

    m_ref[...] = jnp.full(m_ref.shape, NEG_BIG, F32)
    acc_ref[...] = jnp.zeros(acc_ref.shape, F32)

    n_groups = width // tk
    tri = (lax.broadcasted_iota(jnp.int32, (tk, tk), 0)
           <= lax.broadcasted_iota(jnp.int32, (tk, tk), 1))

    def scores(i, slot, g):
        ks = pl.multiple_of(i * tk, tk)
        ka = ka_ref[0, 0, pl.ds(ks, tk), :]
        cols = slice(g * tk, (g + 1) * tk)
        s_ref[slot, :, cols] = _dot(ka, qa[:, cols])

    def softmax_pv(i, slot, g, masked):
        ks = pl.multiple_of(i * tk, tk)
        cols = slice(g * tk, (g + 1) * tk)
        s = s_ref[slot, :, cols]
        if masked:
            s = jnp.where(tri, s, NEG_BIG)
        m_prev = m_ref[:, cols]
        m_new = jnp.maximum(m_prev, jnp.max(s, axis=0, keepdims=True))
        m_ref[:, cols] = m_new
        p = jnp.exp2((s - m_new).astype(BF16))
        pv = _dot(vT_ref[0, g % 2, :, pl.ds(ks, tk)], p)
        acc_ref[:, cols] = jnp.exp2(m_prev - m_new) * acc_ref[:, cols] + pv

    def block_pair(i_next, slot_next, i_cur, slot_cur, g0, diagonal):
        for g in range(g0, n_groups):
            if i_next is not None and g >= g0 + (2 if diagonal else 0):
                scores(i_next, slot_next, g)
            softmax_pv(i_cur, slot_cur, g, diagonal and g < g0 + 2)

    n_off = qi * n_sub
    for g in range(n_groups):
        scores(0, 0, g)

    def body(j, carry):
        for u in range(n_sub):
            block_pair(n_sub * j + u + 1, (u + 1) % 2, n_sub * j + u, u % 2, 0, False)
        return carry

    lax.fori_loop(0, qi, body, 0)
    for dj in range(n_sub):
        nxt = n_off + dj + 1 if dj + 1 < n_sub else None
        block_pair(nxt, (dj + 1) % 2, n_off + dj, dj % 2, 2 * dj, True)

    acc = acc_ref[...]
    o = acc[:HEAD_DIM] * (1.0 / acc[HEAD_DIM:HEAD_DIM + 1])
    o = jnp.concatenate(
        [jnp.concatenate([o[:, (2 * sub + hh) * tk:(2 * sub + hh + 1) * tk] for sub in range(n_sub)],
                         axis=1) for hh in range(2)], axis=0)
    o_ref[0] = o.T.astype(BF16)


def _fox(qT, qb, ka, vT, tq, tk):
    bsz, _, seq = qT.shape
    n_pair = N_MIX_HEADS // 2
    return pl.pallas_call(
        functools.partial(_fox_kernel, tq=tq, tk=tk),
        grid=(bsz, n_pair, seq // tq),
        in_specs=[
            pl.BlockSpec((1, 2 * HEAD_DIM, tq), lambda b, p, i: (b, p, i)),
            pl.BlockSpec((1, 2, BIAS_ROWS, tq), lambda b, p, i: (b, p, 0, i)),
            pl.BlockSpec((1, 1, seq, 2 * LANES), lambda b, p, i: (b, p, 0, 0)),
            pl.BlockSpec((1, 2, V_ROWS, seq), lambda b, p, i: (b, p, 0, 0)),
        ],
        out_specs=pl.BlockSpec((1, tq, LANES), lambda b, p, i: (b, i, p)),
        out_shape=jax.ShapeDtypeStruct((bsz, seq, D_MIX), BF16),
        scratch_shapes=[pltpu.VMEM((1, 2 * tq), F32), pltpu.VMEM((V_ROWS, 2 * tq), F32),
                        pltpu.VMEM((2, tk, 2 * tq), F32)],
        compiler_params=pltpu.CompilerParams(
            dimension_semantics=("arbitrary", "arbitrary", "arbitrary"),
            vmem_limit_bytes=VMEM_LIMIT),
        name="fox_attention",
    )(qT, qb, ka, vT)


def kernel(x, mem, a_w_in, a_sgu_ln_g, a_sgu_ln_b, a_w_s, a_b_s, kv_w, kv_b_f, b_w_q,
           mem_w_kv, w_o, ln_g, ln_b, w_up, w_down):
    bsz, seq, d = x.shape
    t = bsz * seq
    scale = 1.0 / math.sqrt(HEAD_DIM)

    wkT = (jnp.swapaxes(mem_w_kv[:, :, :D_MEM], 1, 2) * scale).astype(BF16)
    wv = mem_w_kv[:, :, D_MEM:].astype(BF16)
    w_in = a_w_in[0].astype(BF16)
    ws_pair = (a_w_s[0].reshape(N_MIX_HEADS // 2, 2, CHUNK, CHUNK)
               .transpose(0, 2, 1, 3).reshape(N_MIX_HEADS // 2, CHUNK, 2 * CHUNK))
    bs_full = jnp.repeat(a_b_s[0].T, HEAD_DIM, axis=1)
    wq = b_w_q[0]
    wnt = jnp.concatenate([wq[:, :D_MIX].T * (scale * LOG2E), kv_w[:, D_MIX:2 * D_MIX].T], axis=0).astype(BF16)
    wf = kv_w[:, 2 * D_MIX:]
    zf = jnp.zeros((d, 16 - N_MIX_HEADS), F32)
    wf3 = jnp.concatenate([wf, zf, wf, zf, wf, zf, jnp.zeros((d, LANES - 48), F32)], axis=1)
    wnn = jnp.concatenate([kv_w[:, :D_MIX], wq[:, D_MIX:], wf3], axis=1).astype(BF16)
    zb = jnp.zeros((16 - N_MIX_HEADS,), F32)
    bf3 = jnp.concatenate([kv_b_f, zb, kv_b_f, zb, kv_b_f, zb, jnp.zeros((LANES - 48,), F32)])[None, :]
    w_o_b = w_o.astype(BF16)
    w_up_b = w_up.astype(BF16)
    w_down_b = w_down.astype(BF16)
    row = lambda v: v[None, :]

    mkT, mvb = _mem_kv(mem, wkT, wv)
    x2d = x.reshape(t, d)

    tm_a = min(256, seq)
    mix, qm = _sgu(x2d, w_in, row(a_sgu_ln_g[0]), row(a_sgu_ln_b[0]), ws_pair, bs_full, tm_a)
    x2d = _tail(mix, qm, x2d, mkT[0], mvb[0], w_o_b[0], row(ln_g[0, 0]), row(ln_b[0, 0]), seq, tm_a)
    tm_m = min(512, seq)
    x2d = _mlp(x2d, w_up_b[0], w_down_b[0], row(ln_g[0, 1]), row(ln_b[0, 1]), tm_m)

    tm_b = min(512, seq)
    qT, qb, ka, vT, qm = _bproj(x2d.reshape(bsz, seq, d), wnt, wnn, bf3, tm_b)
    tq = min(1024, seq)
    att = _fox(qT, qb, ka, vT, tq, min(256, seq)).reshape(t, D_MIX)
    x2d = _tail(att, qm, x2d, mkT[1], mvb[1], w_o_b[1], row(ln_g[1, 0]), row(ln_b[1, 0]), seq, tm_a)
    x2d = _mlp(x2d, w_up_b[1], w_down_b[1], row(ln_g[1, 1]), row(ln_b[1, 1]), tm_m)
    return x2d.reshape(bsz, seq, d)
```

```python
import functools
import math

import jax
import jax.numpy as jnp
from jax import lax
from jax.experimental import pallas as pl
from jax.experimental.pallas import tpu as pltpu

D_MODEL = 1024
HEAD_DIM = 64
N_MIX_HEADS = 12
N_MEM_HEADS = 4
D_MIX = N_MIX_HEADS * HEAD_DIM
D_MEM = N_MEM_HEADS * HEAD_DIM
CHUNK = 128
N_MEM_TOKENS = 256
D_FF = 4 * D_MODEL
DEPTH = 2
DN_ALPHA = (2 * DEPTH) ** 0.25
LN_EPS = 1e-5

LANES = 128
V_ROWS = 80
BIAS_ROWS = 64
NEG_BIG = -1e30
LOG2E = math.log2(math.e)

F32 = jnp.float32
BF16 = jnp.bfloat16

VMEM_LIMIT = 56 * 1024 * 1024


def _const_spec(shape):
    nd = len(shape)
    return pl.BlockSpec(shape, lambda *_: (0,) * nd, pipeline_mode=pl.Buffered(1))


def _dot(a, b):
    return jnp.dot(a, b, preferred_element_type=F32)


def _dot_nt(a, b):
    return lax.dot_general(a, b, (((1,), (1,)), ((), ())), preferred_element_type=F32)


def _layer_norm(r, g, b):
    mu = jnp.mean(r, axis=-1, keepdims=True)
    d = r - mu
    var = jnp.mean(d * d, axis=-1, keepdims=True)
    return d * lax.rsqrt(var + LN_EPS) * g + b


def _split3(c):
    hi = c.astype(BF16).astype(F32)
    r1 = c - hi
    mid = r1.astype(BF16).astype(F32)
    lo = (r1 - mid).astype(BF16).astype(F32)
    return hi, mid, lo


def _mem_kv_kernel(mem_ref, wkT_ref, wv_ref, mkT_ref, mvb_ref):
    memb = mem_ref[0].astype(BF16)
    kT = _dot_nt(wkT_ref[0], memb)
    kT4 = jnp.concatenate([kT] * N_MEM_HEADS, axis=1)
    row = lax.broadcasted_iota(jnp.int32, kT4.shape, 0)
    col = lax.broadcasted_iota(jnp.int32, kT4.shape, 1)
    same = (row >> 6) == (col >> 8)
    mkT_ref[0, 0] = jnp.where(same, kT4, 0.0).astype(BF16)
    v = _dot(memb, wv_ref[0])
    v4 = jnp.concatenate([v] * N_MEM_HEADS, axis=0)
    row = lax.broadcasted_iota(jnp.int32, v4.shape, 0)
    col = lax.broadcasted_iota(jnp.int32, v4.shape, 1)
    same = (row >> 8) == (col >> 6)
    mvb_ref[0, 0] = jnp.where(same, v4, 0.0).astype(BF16)


def _mem_kv(mem, wkT, wv):
    bsz = mem.shape[0]
    m = N_MEM_TOKENS
    return pl.pallas_call(
        _mem_kv_kernel,
        grid=(DEPTH, bsz),
        in_specs=[
            pl.BlockSpec((1, m, D_MODEL), lambda l, b: (b, 0, 0)),
            pl.BlockSpec((1, D_MEM, D_MODEL), lambda l, b: (l, 0, 0)),
            pl.BlockSpec((1, D_MODEL, D_MEM), lambda l, b: (l, 0, 0)),
        ],
        out_specs=[
            pl.BlockSpec((1, 1, D_MEM, N_MEM_HEADS * m), lambda l, b: (l, b, 0, 0)),
            pl.BlockSpec((1, 1, N_MEM_HEADS * m, D_MEM), lambda l, b: (l, b, 0, 0)),
        ],
        out_shape=[
            jax.ShapeDtypeStruct((DEPTH, bsz, D_MEM, N_MEM_HEADS * m), BF16),
            jax.ShapeDtypeStruct((DEPTH, bsz, N_MEM_HEADS * m, D_MEM), BF16),
        ],
        name="mem_kv",
    )(mem, wkT, wv)


def _sgu_kernel(x_ref, win_ref, g_ref, b_ref, ws_ref, bs_ref, mix_ref, qm_ref, *, tm):
    xb = x_ref[...].astype(BF16)
    z = _dot(xb, win_ref[...])
    zu = jax.nn.gelu(z[:, :D_MIX], approximate=True)
    zv = jax.nn.gelu(z[:, D_MIX:2 * D_MIX], approximate=True)
    zv = _layer_norm(zv, g_ref[...], b_ref[...])
    qm_ref[...] = z[:, 2 * D_MIX:].astype(BF16)

    lane = lax.broadcasted_iota(jnp.int32, (CHUNK, LANES), 1)
    low = lane < HEAD_DIM
    wrow = lax.broadcasted_iota(jnp.int32, (CHUNK, 2 * CHUNK), 0)
    wcol = lax.broadcasted_iota(jnp.int32, (CHUNK, 2 * CHUNK), 1) & (CHUNK - 1)
    causal = wcol <= wrow
    for p in range(N_MIX_HEADS // 2):
        w = jnp.where(causal, ws_ref[p], 0.0).astype(BF16)
        bias = bs_ref[:, p * LANES:(p + 1) * LANES]
        for c0 in range(0, tm // CHUNK, 2):
            rhs = []
            for c in (c0, c0 + 1):
                slab = zv[c * CHUNK:(c + 1) * CHUNK, p * LANES:(p + 1) * LANES]
                rhs.append(jnp.concatenate(
                    [jnp.where(low, slab, 0.0), jnp.where(low, 0.0, slab)], axis=0))
            rhs = jnp.concatenate(rhs, axis=1).astype(BF16)
            mixed = _dot(w, rhs)
            for i, c in enumerate((c0, c0 + 1)):
                u = zu[c * CHUNK:(c + 1) * CHUNK, p * LANES:(p + 1) * LANES]
                out = u * (mixed[:, i * LANES:(i + 1) * LANES] + bias)
                mix_ref[c * CHUNK:(c + 1) * CHUNK, p * LANES:(p + 1) * LANES] = out.astype(BF16)


def _sgu(x2d, w_in, ln_g, ln_b, ws_pair, bs_full, tm):
    t = x2d.shape[0]
    n_in = w_in.shape[1]
    return pl.pallas_call(
        functools.partial(_sgu_kernel, tm=tm),
        grid=(t // tm,),
        in_specs=[
            pl.BlockSpec((tm, D_MODEL), lambda i: (i, 0)),
            _const_spec((D_MODEL, n_in)),
            _const_spec((1, D_MIX)),
            _const_spec((1, D_MIX)),
            _const_spec((N_MIX_HEADS // 2, CHUNK, 2 * CHUNK)),
            _const_spec((CHUNK, D_MIX)),
        ],
        out_specs=[pl.BlockSpec((tm, D_MIX), lambda i: (i, 0)),
                   pl.BlockSpec((tm, D_MEM), lambda i: (i, 0))],
        out_shape=[jax.ShapeDtypeStruct((t, D_MIX), BF16),
                   jax.ShapeDtypeStruct((t, D_MEM), BF16)],
        compiler_params=pltpu.CompilerParams(
            dimension_semantics=("arbitrary",), vmem_limit_bytes=VMEM_LIMIT),
        name="sgu_mixer",
    )(x2d, w_in, ln_g, ln_b, ws_pair, bs_full)


def _tail_kernel(mix_ref, qm_ref, x_ref, mkT_ref, mvb_ref, wo_ref, g_ref, b_ref, o_ref):
    m = N_MEM_TOKENS
    s = _dot(qm_ref[...], mkT_ref[0])
    ps = []
    for h in range(N_MEM_HEADS):
        sh = s[:, h * m:(h + 1) * m]
        e = jnp.exp(sh - jnp.max(sh, axis=-1, keepdims=True))
        ps.append((e / jnp.sum(e, axis=-1, keepdims=True)).astype(BF16))
    p = jnp.concatenate(ps, axis=1)
    mo = _dot(p, mvb_ref[0]).astype(BF16)
    y = _dot(mix_ref[...], wo_ref[:D_MIX, :]) + _dot(mo, wo_ref[D_MIX:, :])
    r = DN_ALPHA * x_ref[...] + y
    o_ref[...] = _layer_norm(r, g_ref[...], b_ref[...])


def _tail(mix, qm, x2d, mkT, mvb, w_o, ln_g, ln_b, seq, tm):
    t = x2d.shape[0]
    m = N_MEM_TOKENS
    per_b = seq // tm
    return pl.pallas_call(
        _tail_kernel,
        grid=(t // tm,),
        in_specs=[
            pl.BlockSpec((tm, D_MIX), lambda i: (i, 0)),
            pl.BlockSpec((tm, D_MEM), lambda i: (i, 0)),
            pl.BlockSpec((tm, D_MODEL), lambda i: (i, 0)),
            pl.BlockSpec((1, D_MEM, N_MEM_HEADS * m), lambda i: (i // per_b, 0, 0)),
            pl.BlockSpec((1, N_MEM_HEADS * m, D_MEM), lambda i: (i // per_b, 0, 0)),
            _const_spec((D_MODEL, D_MODEL)),
            _const_spec((1, D_MODEL)),
            _const_spec((1, D_MODEL)),
        ],
        out_specs=pl.BlockSpec((tm, D_MODEL), lambda i: (i, 0)),
        out_shape=jax.ShapeDtypeStruct((t, D_MODEL), F32),
        compiler_params=pltpu.CompilerParams(
            dimension_semantics=("arbitrary",), vmem_limit_bytes=VMEM_LIMIT),
        name="mixer_tail",
    )(mix, qm, x2d, mkT, mvb, w_o, ln_g, ln_b)


def _mlp_kernel(x_ref, wu_ref, wd_ref, g_ref, b_ref, o_ref, h_ref, *, n_chunk):
    x = x_ref[...]
    xb = x.astype(BF16)
    for c in range(D_FF // n_chunk):
        h = _dot(xb, wu_ref[:, c * n_chunk:(c + 1) * n_chunk])
        h = jnp.maximum(h, 0.0)
        h_ref[:, c * n_chunk:(c + 1) * n_chunk] = (h * h).astype(BF16)
    y = _dot(h_ref[...], wd_ref[...])
    r = DN_ALPHA * x + y
    o_ref[...] = _layer_norm(r, g_ref[...], b_ref[...])


def _mlp(x2d, w_up, w_down, ln_g, ln_b, tm, n_chunk=1024):
    t = x2d.shape[0]
    return pl.pallas_call(
        functools.partial(_mlp_kernel, n_chunk=n_chunk),
        grid=(t // tm,),
        in_specs=[
            pl.BlockSpec((tm, D_MODEL), lambda i: (i, 0)),
            _const_spec((D_MODEL, D_FF)),
            _const_spec((D_FF, D_MODEL)),
            _const_spec((1, D_MODEL)),
            _const_spec((1, D_MODEL)),
        ],
        out_specs=pl.BlockSpec((tm, D_MODEL), lambda i: (i, 0)),
        out_shape=jax.ShapeDtypeStruct((t, D_MODEL), F32),
        scratch_shapes=[pltpu.VMEM((tm, D_FF), BF16)],
        compiler_params=pltpu.CompilerParams(
            dimension_semantics=("arbitrary",), vmem_limit_bytes=VMEM_LIMIT),
        name="relu2_mlp",
    )(x2d, w_up, w_down, ln_g, ln_b)


def _log_sigmoid(x):
    return jnp.minimum(x, 0.0) - jnp.log(1.0 + jnp.exp(-jnp.abs(x)))


def _bproj_kernel(x_ref, wnt_ref, wnn_ref, bf_ref,
                  qT_ref, qb_ref, ka_ref, vT_ref, qm_ref, carry_ref, *, tm):
    @pl.when(pl.program_id(1) == 0)
    def _():
        carry_ref[...] = jnp.zeros_like(carry_ref)

    xb = x_ref[...].astype(BF16)
    nt = _dot_nt(wnt_ref[...], xb)
    qT_ref[0] = nt[:D_MIX].astype(BF16)
    vrow = lax.broadcasted_iota(jnp.int32, (V_ROWS - HEAD_DIM, tm), 0)
    ones_rows = jnp.where(vrow == 0, 1.0, 0.0).astype(BF16)
    for h in range(N_MIX_HEADS):
        vT_ref[0, h, :HEAD_DIM, :] = nt[D_MIX + h * HEAD_DIM:D_MIX + (h + 1) * HEAD_DIM].astype(BF16)
        vT_ref[0, h, HEAD_DIM:, :] = ones_rows

    nn = _dot(xb, wnn_ref[...])
    qm_ref[...] = nn[:, D_MIX:D_MIX + D_MEM].astype(BF16)

    f3 = nn[:, D_MIX + D_MEM:] + bf_ref[...]
    lane = lax.broadcasted_iota(jnp.int32, (tm, LANES), 1)
    valid = (lane & 15) < N_MIX_HEADS
    valid = valid & (lane < 48)
    c = jnp.where(valid, _log_sigmoid(f3) * LOG2E, 0.0)
    rowi = lax.broadcasted_iota(jnp.int32, (tm, LANES), 0)
    d = 1
    while d < tm:
        c = c + jnp.where(rowi >= d, pltpu.roll(c, d, axis=0), 0.0)
        d *= 2
    c = c + carry_ref[...]
    carry_ref[...] = c[tm - 1:tm, :]

    hi, mid, lo = _split3(c)
    kb = jnp.where(lane < 16, -hi,
                   jnp.where(lane < 32, -mid,
                             jnp.where(lane < 48, -lo,
                                       jnp.where(lane < 51, 1.0, 0.0))))
    kb = kb.astype(BF16)
    for p in range(N_MIX_HEADS // 2):
        ka_ref[0, p, :, :LANES] = nn[:, p * LANES:(p + 1) * LANES].astype(BF16)
        ka_ref[0, p, :, LANES:] = kb

    cT = c.T
    brow = lax.broadcasted_iota(jnp.int32, (BIAS_ROWS, tm), 0)
    for h in range(N_MIX_HEADS):
        chi, cmid, clo = _split3(cT[h:h + 1, :])
        sel = (brow == h) | (brow == 16 + h) | (brow == 32 + h)
        tile = jnp.where(brow == 48, chi,
                         jnp.where(brow == 49, cmid,
                                   jnp.where(brow == 50, clo,
                                             jnp.where(sel, 1.0, 0.0))))
        qb_ref[0, h] = tile.astype(BF16)


def _bproj(x3d, wnt, wnn, bf3, tm):
    bsz, seq, _ = x3d.shape
    n_nn = wnn.shape[1]
    return pl.pallas_call(
        functools.partial(_bproj_kernel, tm=tm),
        grid=(bsz, seq // tm),
        in_specs=[
            pl.BlockSpec((None, tm, D_MODEL), lambda b, j: (b, j, 0)),
            _const_spec((2 * D_MIX, D_MODEL)),
            _const_spec((D_MODEL, n_nn)),
            _const_spec((1, LANES)),
        ],
        out_specs=[
            pl.BlockSpec((1, D_MIX, tm), lambda b, j: (b, 0, j)),
            pl.BlockSpec((1, N_MIX_HEADS, BIAS_ROWS, tm), lambda b, j: (b, 0, 0, j)),
            pl.BlockSpec((1, N_MIX_HEADS // 2, tm, 2 * LANES), lambda b, j: (b, 0, j, 0)),
            pl.BlockSpec((1, N_MIX_HEADS, V_ROWS, tm), lambda b, j: (b, 0, 0, j)),
            pl.BlockSpec((tm, D_MEM), lambda b, j: (b * (seq // tm) + j, 0)),
        ],
        out_shape=[
            jax.ShapeDtypeStruct((bsz, D_MIX, seq), BF16),
            jax.ShapeDtypeStruct((bsz, N_MIX_HEADS, BIAS_ROWS, seq), BF16),
            jax.ShapeDtypeStruct((bsz, N_MIX_HEADS // 2, seq, 2 * LANES), BF16),
            jax.ShapeDtypeStruct((bsz, N_MIX_HEADS, V_ROWS, seq), BF16),
            jax.ShapeDtypeStruct((bsz * seq, D_MEM), BF16),
        ],
        scratch_shapes=[pltpu.VMEM((1, LANES), F32)],
        compiler_params=pltpu.CompilerParams(
            dimension_semantics=("arbitrary", "arbitrary"), vmem_limit_bytes=VMEM_LIMIT),
        name="fox_proj",
    )(x3d, wnt, wnn, bf3)


def _fox_kernel(qT_ref, qb_ref, ka_ref, vT_ref, o_ref, m_ref, acc_ref, s_ref, *, tq, tk):
    qi = pl.program_id(2)
    n_sub = tq // tk
    assert n_sub % 2 == 0
    gw = 2 * tk
    width = n_sub * gw

    slab = qT_ref[0]
    srow = lax.broadcasted_iota(jnp.int32, (2 * HEAD_DIM, tk), 0)
    zpad = jnp.zeros((LANES - BIAS_ROWS, tk), BF16)
    pieces = []
    for sub in range(n_sub):
        cols = slice(sub * tk, (sub + 1) * tk)
        sl = slab[:, cols]
        for hh in range(2):
            keep = (srow < HEAD_DIM) if hh == 0 else (srow >= HEAD_DIM)
            pieces.append(jnp.concatenate(
                [jnp.where(keep, sl, jnp.zeros_like(sl)), qb_ref[0, hh, :, cols], zpad], axis=0))
    qa = jnp.concatenate(pieces, axis=1)

    m_ref[...] = jnp.full(m_ref.shape, NEG_BIG, F32)
    acc_ref[...] = jnp.zeros(acc_ref.shape, F32)

    n_groups = width // tk
    tri = (lax.broadcasted_iota(jnp.int32, (tk, tk), 0)
           <= lax.broadcasted_iota(jnp.int32, (tk, tk), 1))

    def scores(i, slot, g):
        ks = pl.multiple_of(i * tk, tk)
        ka = ka_ref[0, 0, pl.ds(ks, tk), :]
        cols = slice(g * tk, (g + 1) * tk)
        s_ref[slot, :, cols] = _dot(ka, qa[:, cols])

    def softmax_pv(i, slot, g, masked):
        ks = pl.multiple_of(i * tk, tk)
        cols = slice(g * tk, (g + 1) * tk)
        s = s_ref[slot, :, cols]
        if masked:
            s = jnp.where(tri, s, NEG_BIG)
        m_prev = m_ref[:, cols]
        m_new = jnp.maximum(m_prev, jnp.max(s, axis=0, keepdims=True))
        m_ref[:, cols] = m_new
        p = jnp.exp2((s - m_new).astype(BF16))
        pv = _dot(vT_ref[0, g % 2, :, pl.ds(ks, tk)], p)
        acc_ref[:, cols] = jnp.exp2(m_prev - m_new) * acc_ref[:, cols] + pv

    def block_pair(i_next, slot_next, i_cur, slot_cur, g0, diagonal):
        for g in range(g0, n_groups):
            if i_next is not None and g >= g0 + (2 if diagonal else 0):
                scores(i_next, slot_next, g)
            softmax_pv(i_cur, slot_cur, g, diagonal and g < g0 + 2)

    n_off = qi * n_sub
    for g in range(n_groups):
        scores(0, 0, g)

    def body(j, carry):
        for u in range(n_sub):
            block_pair(n_sub * j + u + 1, (u + 1) % 2, n_sub * j + u, u % 2, 0, False)
        return carry

    lax.fori_loop(0, qi, body, 0)
    for dj in range(n_sub):
        nxt = n_off + dj + 1 if dj + 1 < n_sub else None
        block_pair(nxt, (dj + 1) % 2, n_off + dj, dj % 2, 2 * dj, True)

    acc = acc_ref[...]
    o = acc[:HEAD_DIM] * (1.0 / acc[HEAD_DIM:HEAD_DIM + 1])
    o = jnp.concatenate(
        [jnp.concatenate([o[:, (2 * sub + hh) * tk:(2 * sub + hh + 1) * tk] for sub in range(n_sub)],
                         axis=1) for hh in range(2)], axis=0)
    o_ref[0] = o.T.astype(BF16)


def _fox(qT, qb, ka, vT, tq, tk):
    bsz, _, seq = qT.shape
    n_pair = N_MIX_HEADS // 2
    return pl.pallas_call(
        functools.partial(_fox_kernel, tq=tq, tk=tk),
        grid=(bsz, n_pair, seq // tq),
        in_specs=[
            pl.BlockSpec((1, 2 * HEAD_DIM, tq), lambda b, p, i: (b, p, i)),
            pl.BlockSpec((1, 2, BIAS_ROWS, tq), lambda b, p, i: (b, p, 0, i)),
            pl.BlockSpec((1, 1, seq, 2 * LANES), lambda b, p, i: (b, p, 0, 0)),
            pl.BlockSpec((1, 2, V_ROWS, seq), lambda b, p, i: (b, p, 0, 0)),
        ],
        out_specs=pl.BlockSpec((1, tq, LANES), lambda b, p, i: (b, i, p)),
        out_shape=jax.ShapeDtypeStruct((bsz, seq, D_MIX), BF16),
        scratch_shapes=[pltpu.VMEM((1, 2 * tq), F32), pltpu.VMEM((V_ROWS, 2 * tq), F32),
                        pltpu.VMEM((2, tk, 2 * tq), F32)],
        compiler_params=pltpu.CompilerParams(
            dimension_semantics=("arbitrary", "arbitrary", "arbitrary"),
            vmem_limit_bytes=VMEM_LIMIT),
        name="fox_attention",
    )(qT, qb, ka, vT)


def kernel(x, mem, a_w_in, a_sgu_ln_g, a_sgu_ln_b, a_w_s, a_b_s, kv_w, kv_b_f, b_w_q,
           mem_w_kv, w_o, ln_g, ln_b, w_up, w_down):
    bsz, seq, d = x.shape
    t = bsz * seq
    scale = 1.0 / math.sqrt(HEAD_DIM)

    wkT = (jnp.swapaxes(mem_w_kv[:, :, :D_MEM], 1, 2) * scale).astype(BF16)
    wv = mem_w_kv[:, :, D_MEM:].astype(BF16)
    w_in = a_w_in[0].astype(BF16)
    ws_pair = (a_w_s[0].reshape(N_MIX_HEADS // 2, 2, CHUNK, CHUNK)
               .transpose(0, 2, 1, 3).reshape(N_MIX_HEADS // 2, CHUNK, 2 * CHUNK))
    bs_full = jnp.repeat(a_b_s[0].T, HEAD_DIM, axis=1)
    wq = b_w_q[0]
    wnt = jnp.concatenate([wq[:, :D_MIX].T * (scale * LOG2E), kv_w[:, D_MIX:2 * D_MIX].T], axis=0).astype(BF16)
    wf = kv_w[:, 2 * D_MIX:]
    zf = jnp.zeros((d, 16 - N_MIX_HEADS), F32)
    wf3 = jnp.concatenate([wf, zf, wf, zf, wf, zf, jnp.zeros((d, LANES - 48), F32)], axis=1)
    wnn = jnp.concatenate([kv_w[:, :D_MIX], wq[:, D_MIX:], wf3], axis=1).astype(BF16)
    zb = jnp.zeros((16 - N_MIX_HEADS,), F32)
    bf3 = jnp.concatenate([kv_b_f, zb, kv_b_f, zb, kv_b_f, zb, jnp.zeros((LANES - 48,), F32)])[None, :]
    w_o_b = w_o.astype(BF16)
    w_up_b = w_up.astype(BF16)
    w_down_b = w_down.astype(BF16)
    row = lambda v: v[None, :]

    mkT, mvb = _mem_kv(mem, wkT, wv)
    x2d = x.reshape(t, d)

    tm_a = min(1024, seq)
    mix, qm = _sgu(x2d, w_in, row(a_sgu_ln_g[0]), row(a_sgu_ln_b[0]), ws_pair, bs_full, tm_a)
    x2d = _tail(mix, qm, x2d, mkT[0], mvb[0], w_o_b[0], row(ln_g[0, 0]), row(ln_b[0, 0]), seq, tm_a)
    tm_m = min(1024, seq)
    x2d = _mlp(x2d, w_up_b[0], w_down_b[0], row(ln_g[0, 1]), row(ln_b[0, 1]), tm_m)

    tm_b = min(1024, seq)
    qT, qb, ka, vT, qm = _bproj(x2d.reshape(bsz, seq, d), wnt, wnn, bf3, tm_b)
    tq = min(1024, seq)
    att = _fox(qT, qb, ka, vT, tq, min(256, seq)).reshape(t, D_MIX)
    x2d = _tail(att, qm, x2d, mkT[1], mvb[1], w_o_b[1], row(ln_g[1, 0]), row(ln_b[1, 0]), seq, tm_a)
    x2d = _mlp(x2d, w_up_b[1], w_down_b[1], row(ln_g[1, 1]), row(ln_b[1, 1]), tm_m)
    return x2d.reshape(bsz, seq, d)
```

```python
import functools
import math

import jax
import jax.numpy as jnp
from jax import lax
from jax.experimental import pallas as pl
from jax.experimental.pallas import tpu as pltpu

D_MODEL = 1024
HEAD_DIM = 64
N_MIX_HEADS = 12
N_MEM_HEADS = 4
D_MIX = N_MIX_HEADS * HEAD_DIM
D_MEM = N_MEM_HEADS * HEAD_DIM
CHUNK = 128
N_MEM_TOKENS = 256
D_FF = 4 * D_MODEL
DEPTH = 2
DN_ALPHA = (2 * DEPTH) ** 0.25
LN_EPS = 1e-5

LANES = 128
V_ROWS = 80
BIAS_ROWS = 64
NEG_BIG = -1e30
LOG2E = math.log2(math.e)

F32 = jnp.float32
BF16 = jnp.bfloat16

VMEM_LIMIT = 56 * 1024 * 1024


def _const_spec(shape):
    nd = len(shape)
    return pl.BlockSpec(shape, lambda *_: (0,) * nd, pipeline_mode=pl.Buffered(1))


def _dot(a, b):
    return jnp.dot(a, b, preferred_element_type=F32)


def _dot_nt(a, b):
    return lax.dot_general(a, b, (((1,), (1,)), ((), ())), preferred_element_type=F32)


def _layer_norm(r, g, b):
    mu = jnp.mean(r, axis=-1, keepdims=True)
    d = r - mu
    var = jnp.mean(d * d, axis=-1, keepdims=True)
    return d * lax.rsqrt(var + LN_EPS) * g + b


def _split3(c):
    hi = c.astype(BF16).astype(F32)
    r1 = c - hi
    mid = r1.astype(BF16).astype(F32)
    lo = (r1 - mid).astype(BF16).astype(F32)
    return hi, mid, lo


def _mem_kv_kernel(mem_ref, wkT_ref, wv_ref, mkT_ref, mvb_ref):
    memb = mem_ref[0].astype(BF16)
    kT = _dot_nt(wkT_ref[0], memb)
    kT4 = jnp.concatenate([kT] * N_MEM_HEADS, axis=1)
    row = lax.broadcasted_iota(jnp.int32, kT4.shape, 0)
    col = lax.broadcasted_iota(jnp.int32, kT4.shape, 1)
    same = (row >> 6) == (col >> 8)
    mkT_ref[0, 0] = jnp.where(same, kT4, 0.0).astype(BF16)
    v = _dot(memb, wv_ref[0])
    v4 = jnp.concatenate([v] * N_MEM_HEADS, axis=0)
    row = lax.broadcasted_iota(jnp.int32, v4.shape, 0)
    col = lax.broadcasted_iota(jnp.int32, v4.shape, 1)
    same = (row >> 8) == (col >> 6)
    mvb_ref[0, 0] = jnp.where(same, v4, 0.0).astype(BF16)


def _mem_kv(mem, wkT, wv):
    bsz = mem.shape[0]
    m = N_MEM_TOKENS
    return pl.pallas_call(
        _mem_kv_kernel,
        grid=(DEPTH, bsz),
        in_specs=[
            pl.BlockSpec((1, m, D_MODEL), lambda l, b: (b, 0, 0)),
            pl.BlockSpec((1, D_MEM, D_MODEL), lambda l, b: (l, 0, 0)),
            pl.BlockSpec((1, D_MODEL, D_MEM), lambda l, b: (l, 0, 0)),
        ],
        out_specs=[
            pl.BlockSpec((1, 1, D_MEM, N_MEM_HEADS * m), lambda l, b: (l, b, 0, 0)),
            pl.BlockSpec((1, 1, N_MEM_HEADS * m, D_MEM), lambda l, b: (l, b, 0, 0)),
        ],
        out_shape=[
            jax.ShapeDtypeStruct((DEPTH, bsz, D_MEM, N_MEM_HEADS * m), BF16),
            jax.ShapeDtypeStruct((DEPTH, bsz, N_MEM_HEADS * m, D_MEM), BF16),
        ],
        name="mem_kv",
    )(mem, wkT, wv)


def _sgu_kernel(x_ref, win_ref, g_ref, b_ref, ws_ref, bs_ref, mix_ref, qm_ref, *, tm):
    xb = x_ref[...].astype(BF16)
    z = _dot(xb, win_ref[...])
    zu = jax.nn.gelu(z[:, :D_MIX], approximate=True)
    zv = jax.nn.gelu(z[:, D_MIX:2 * D_MIX], approximate=True)
    zv = _layer_norm(zv, g_ref[...], b_ref[...])
    qm_ref[...] = z[:, 2 * D_MIX:].astype(BF16)

    lane = lax.broadcasted_iota(jnp.int32, (CHUNK, LANES), 1)
    low = lane < HEAD_DIM
    wrow = lax.broadcasted_iota(jnp.int32, (CHUNK, 2 * CHUNK), 0)
    wcol = lax.broadcasted_iota(jnp.int32, (CHUNK, 2 * CHUNK), 1) & (CHUNK - 1)
    causal = wcol <= wrow
    for p in range(N_MIX_HEADS // 2):
        w = jnp.where(causal, ws_ref[p], 0.0).astype(BF16)
        bias = bs_ref[:, p * LANES:(p + 1) * LANES]
        for c0 in range(0, tm // CHUNK, 2):
            rhs = []
            for c in (c0, c0 + 1):
                slab = zv[c * CHUNK:(c + 1) * CHUNK, p * LANES:(p + 1) * LANES]
                rhs.append(jnp.concatenate(
                    [jnp.where(low, slab, 0.0), jnp.where(low, 0.0, slab)], axis=0))
            rhs = jnp.concatenate(rhs, axis=1).astype(BF16)
            mixed = _dot(w, rhs)
            for i, c in enumerate((c0, c0 + 1)):
                u = zu[c * CHUNK:(c + 1) * CHUNK, p * LANES:(p + 1) * LANES]
                out = u * (mixed[:, i * LANES:(i + 1) * LANES] + bias)
                mix_ref[c * CHUNK:(c + 1) * CHUNK, p * LANES:(p + 1) * LANES] = out.astype(BF16)


def _sgu(x2d, w_in, ln_g, ln_b, ws_pair, bs_full, tm):
    t = x2d.shape[0]
    n_in = w_in.shape[1]
    return pl.pallas_call(
        functools.partial(_sgu_kernel, tm=tm),
        grid=(t // tm,),
        in_specs=[
            pl.BlockSpec((tm, D_MODEL), lambda i: (i, 0)),
            _const_spec((D_MODEL, n_in)),
            _const_spec((1, D_MIX)),
            _const_spec((1, D_MIX)),
            _const_spec((N_MIX_HEADS // 2, CHUNK, 2 * CHUNK)),
            _const_spec((CHUNK, D_MIX)),
        ],
        out_specs=[pl.BlockSpec((tm, D_MIX), lambda i: (i, 0)),
                   pl.BlockSpec((tm, D_MEM), lambda i: (i, 0))],
        out_shape=[jax.ShapeDtypeStruct((t, D_MIX), BF16),
                   jax.ShapeDtypeStruct((t, D_MEM), BF16)],
        compiler_params=pltpu.CompilerParams(
            dimension_semantics=("arbitrary",), vmem_limit_bytes=VMEM_LIMIT),
        name="sgu_mixer",
    )(x2d, w_in, ln_g, ln_b, ws_pair, bs_full)


def _tail_kernel(mix_ref, qm_ref, x_ref, mkT_ref, mvb_ref, wo_ref, g_ref, b_ref, o_ref):
    m = N_MEM_TOKENS
    s = _dot(qm_ref[...], mkT_ref[0])
    ps = []
    for h in range(N_MEM_HEADS):
        sh = s[:, h * m:(h + 1) * m]
        e = jnp.exp(sh - jnp.max(sh, axis=-1, keepdims=True))
        ps.append((e / jnp.sum(e, axis=-1, keepdims=True)).astype(BF16))
    p = jnp.concatenate(ps, axis=1)
    mo = _dot(p, mvb_ref[0]).astype(BF16)
    y = _dot(mix_ref[...], wo_ref[:D_MIX, :]) + _dot(mo, wo_ref[D_MIX:, :])
    r = DN_ALPHA * x_ref[...] + y
    o_ref[...] = _layer_norm(r, g_ref[...], b_ref[...])


def _tail(mix, qm, x2d, mkT, mvb, w_o, ln_g, ln_b, seq, tm):
    t = x2d.shape[0]
    m = N_MEM_TOKENS
    per_b = seq // tm
    return pl.pallas_call(
        _tail_kernel,
        grid=(t // tm,),
        in_specs=[
            pl.BlockSpec((tm, D_MIX), lambda i: (i, 0)),
            pl.BlockSpec((tm, D_MEM), lambda i: (i, 0)),
            pl.BlockSpec((tm, D_MODEL), lambda i: (i, 0)),
            pl.BlockSpec((1, D_MEM, N_MEM_HEADS * m), lambda i: (i // per_b, 0, 0)),
            pl.BlockSpec((1, N_MEM_HEADS * m, D_MEM), lambda i: (i // per_b, 0, 0)),
            _const_spec((D_MODEL, D_MODEL)),
            _const_spec((1, D_MODEL)),
            _const_spec((1, D_MODEL)),
        ],
        out_specs=pl.BlockSpec((tm, D_MODEL), lambda i: (i, 0)),
        out_shape=jax.ShapeDtypeStruct((t, D_MODEL), F32),
        compiler_params=pltpu.CompilerParams(
            dimension_semantics=("arbitrary",), vmem_limit_bytes=VMEM_LIMIT),
        name="mixer_tail",
    )(mix, qm, x2d, mkT, mvb, w_o, ln_g, ln_b)


def _mlp_kernel(x_ref, wu_ref, wd_ref, g_ref, b_ref, o_ref, h_ref, *, n_chunk):
    x = x_ref[...]
    xb = x.astype(BF16)
    for c in range(D_FF // n_chunk):
        h = _dot(xb, wu_ref[:, c * n_chunk:(c + 1) * n_chunk])
        h = jnp.maximum(h, 0.0)
        h_ref[:, c * n_chunk:(c + 1) * n_chunk] = (h * h).astype(BF16)
    y = _dot(h_ref[...], wd_ref[...])
    r = DN_ALPHA * x + y
    o_ref[...] = _layer_norm(r, g_ref[...], b_ref[...])


def _mlp(x2d, w_up, w_down, ln_g, ln_b, tm, n_chunk=1024):
    t = x2d.shape[0]
    return pl.pallas_call(
        functools.partial(_mlp_kernel, n_chunk=n_chunk),
        grid=(t // tm,),
        in_specs=[
            pl.BlockSpec((tm, D_MODEL), lambda i: (i, 0)),
            _const_spec((D_MODEL, D_FF)),
            _const_spec((D_FF, D_MODEL)),
            _const_spec((1, D_MODEL)),
            _const_spec((1, D_MODEL)),
        ],
        out_specs=pl.BlockSpec((tm, D_MODEL), lambda i: (i, 0)),
        out_shape=jax.ShapeDtypeStruct((t, D_MODEL), F32),
        scratch_shapes=[pltpu.VMEM((tm, D_FF), BF16)],
        compiler_params=pltpu.CompilerParams(
            dimension_semantics=("arbitrary",), vmem_limit_bytes=VMEM_LIMIT),
        name="relu2_mlp",
    )(x2d, w_up, w_down, ln_g, ln_b)


def _log_sigmoid(x):
    return jnp.minimum(x, 0.0) - jnp.log(1.0 + jnp.exp(-jnp.abs(x)))


def _bproj_kernel(x_ref, wnt_ref, wnn_ref, bf_ref,
                  qT_ref, qb_ref, ka_ref, vT_ref, qm_ref, carry_ref, *, tm):
    @pl.when(pl.program_id(1) == 0)
    def _():
        carry_ref[...] = jnp.zeros_like(carry_ref)

    xb = x_ref[...].astype(BF16)
    nt = _dot_nt(wnt_ref[...], xb)
    qT_ref[0] = nt[:D_MIX].astype(BF16)
    vrow = lax.broadcasted_iota(jnp.int32, (V_ROWS - HEAD_DIM, tm), 0)
    ones_rows = jnp.where(vrow == 0, 1.0, 0.0).astype(BF16)
    for h in range(N_MIX_HEADS):
        vT_ref[0, h, :HEAD_DIM, :] = nt[D_MIX + h * HEAD_DIM:D_MIX + (h + 1) * HEAD_DIM].astype(BF16)
        vT_ref[0, h, HEAD_DIM:, :] = ones_rows

    nn = _dot(xb, wnn_ref[...])
    qm_ref[...] = nn[:, D_MIX:D_MIX + D_MEM].astype(BF16)

    f3 = nn[:, D_MIX + D_MEM:] + bf_ref[...]
    lane = lax.broadcasted_iota(jnp.int32, (tm, LANES), 1)
    valid = (lane & 15) < N_MIX_HEADS
    valid = valid & (lane < 48)
    c = jnp.where(valid, _log_sigmoid(f3) * LOG2E, 0.0)
    rowi = lax.broadcasted_iota(jnp.int32, (tm, LANES), 0)
    d = 1
    while d < tm:
        c = c + jnp.where(rowi >= d, pltpu.roll(c, d, axis=0), 0.0)
        d *= 2
    c = c + carry_ref[...]
    carry_ref[...] = c[tm - 1:tm, :]

    hi, mid, lo = _split3(c)
    kb = jnp.where(lane < 16, -hi,
                   jnp.where(lane < 32, -mid,
                             jnp.where(lane < 48, -lo,
                                       jnp.where(lane < 51, 1.0, 0.0))))
    kb = kb.astype(BF16)
    for p in range(N_MIX_HEADS // 2):
        ka_ref[0, p, :, :LANES] = nn[:, p * LANES:(p + 1) * LANES].astype(BF16)
        ka_ref[0, p, :, LANES:] = kb

    cT = c.T
    brow = lax.broadcasted_iota(jnp.int32, (BIAS_ROWS, tm), 0)
    for h in range(N_MIX_HEADS):
        chi, cmid, clo = _split3(cT[h:h + 1, :])
        sel = (brow == h) | (brow == 16 + h) | (brow == 32 + h)
        tile = jnp.where(brow == 48, chi,
                         jnp.where(brow == 49, cmid,
                                   jnp.where(brow == 50, clo,
                                             jnp.where(sel, 1.0, 0.0))))
        qb_ref[0, h] = tile.astype(BF16)


def _bproj(x3d, wnt, wnn, bf3, tm):
    bsz, seq, _ = x3d.shape
    n_nn = wnn.shape[1]
    return pl.pallas_call(
        functools.partial(_bproj_kernel, tm=tm),
        grid=(bsz, seq // tm),
        in_specs=[
            pl.BlockSpec((None, tm, D_MODEL), lambda b, j: (b, j, 0)),
            _const_spec((2 * D_MIX, D_MODEL)),
            _const_spec((D_MODEL, n_nn)),
            _const_spec((1, LANES)),
        ],
        out_specs=[
            pl.BlockSpec((1, D_MIX, tm), lambda b, j: (b, 0, j)),
            pl.BlockSpec((1, N_MIX_HEADS, BIAS_ROWS, tm), lambda b, j: (b, 0, 0, j)),
            pl.BlockSpec((1, N_MIX_HEADS // 2, tm, 2 * LANES), lambda b, j: (b, 0, j, 0)),
            pl.BlockSpec((1, N_MIX_HEADS, V_ROWS, tm), lambda b, j: (b, 0, 0, j)),
            pl.BlockSpec((tm, D_MEM), lambda b, j: (b * (seq // tm) + j, 0)),
        ],
        out_shape=[
            jax.ShapeDtypeStruct((bsz, D_MIX, seq), BF16),
            jax.ShapeDtypeStruct((bsz, N_MIX_HEADS, BIAS_ROWS, seq), BF16),
            jax.ShapeDtypeStruct((bsz, N_MIX_HEADS // 2, seq, 2 * LANES), BF16),
            jax.ShapeDtypeStruct((bsz, N_MIX_HEADS, V_ROWS, seq), BF16),
            jax.ShapeDtypeStruct((bsz * seq, D_MEM), BF16),
        ],
        scratch_shapes=[pltpu.VMEM((1, LANES), F32)],
        compiler_params=pltpu.CompilerParams(
            dimension_semantics=("arbitrary", "arbitrary"), vmem_limit_bytes=VMEM_LIMIT),
        name="fox_proj",
    )(x3d, wnt, wnn, bf3)


def _fox_kernel(qT_ref, qb_ref, ka_ref, vT_ref, o_ref, qa_ref, m_ref, acc_ref, s_ref, *, tq, tk):
    qi = pl.program_id(2)
    n_sub = tq // tk
    assert n_sub % 2 == 0
    n_groups = 2 * n_sub

    srow = lax.broadcasted_iota(jnp.int32, (2 * HEAD_DIM, tk), 0)
    zpad = jnp.zeros((LANES - BIAS_ROWS, tk), BF16)
    for sub in range(n_sub):
        cols = slice(sub * tk, (sub + 1) * tk)
        sl = qT_ref[0, :, cols]
        for hh in range(2):
            g = 2 * sub + hh
            keep = (srow < HEAD_DIM) if hh == 0 else (srow >= HEAD_DIM)
            qa_ref[:, g * tk:(g + 1) * tk] = jnp.concatenate(
                [jnp.where(keep, sl, jnp.zeros_like(sl)), qb_ref[0, hh, :, cols], zpad], axis=0)

    m_ref[...] = jnp.full(m_ref.shape, NEG_BIG, F32)
    acc_ref[...] = jnp.zeros(acc_ref.shape, F32)

    tri = (lax.broadcasted_iota(jnp.int32, (tk, tk), 0)
           <= lax.broadcasted_iota(jnp.int32, (tk, tk), 1))

    def scores(i, slot, g):
        ks = pl.multiple_of(i * tk, tk)
        cols = slice(g * tk, (g + 1) * tk)
        s_ref[slot, :, cols] = _dot(ka_ref[0, 0, pl.ds(ks, tk), :], qa_ref[:, cols]).astype(BF16)

    def softmax_pv(i, slot, g, masked):
        ks = pl.multiple_of(i * tk, tk)
        cols = slice(g * tk, (g + 1) * tk)
        s = s_ref[slot, :, cols]
        if masked:
            s = jnp.where(tri, s, NEG_BIG)
        m_prev = m_ref[:, cols]
        m_new = jnp.maximum(m_prev, jnp.max(s, axis=0, keepdims=True).astype(F32))
        m_ref[:, cols] = m_new
        p = jnp.exp2(s - m_new.astype(BF16))
        pv = _dot(vT_ref[0, g % 2, :, pl.ds(ks, tk)], p)
        acc_ref[:, cols] = jnp.exp2(m_prev - m_new) * acc_ref[:, cols] + pv

    def block_pair(i_next, slot_next, i_cur, slot_cur, g0, diagonal):
        for g in range(g0, n_groups):
            if i_next is not None and g >= g0 + (2 if diagonal else 0):
                scores(i_next, slot_next, g)
            softmax_pv(i_cur, slot_cur, g, diagonal and g < g0 + 2)

    n_off = qi * n_sub
    for g in range(n_groups):
        scores(0, 0, g)

    def body(j, carry):
        for u in range(n_sub):
            block_pair(n_sub * j + u + 1, (u + 1) % 2, n_sub * j + u, u % 2, 0, False)
        return carry

    lax.fori_loop(0, qi, body, 0)
    for dj in range(n_sub):
        nxt = n_off + dj + 1 if dj + 1 < n_sub else None
        block_pair(nxt, (dj + 1) % 2, n_off + dj, dj % 2, 2 * dj, True)

    acc = acc_ref[...]
    o = acc[:HEAD_DIM] * (1.0 / acc[HEAD_DIM:HEAD_DIM + 1])
    o = jnp.concatenate(
        [jnp.concatenate([o[:, (2 * sub + hh) * tk:(2 * sub + hh + 1) * tk] for sub in range(n_sub)],
                         axis=1) for hh in range(2)], axis=0)
    o_ref[0] = o.T.astype(BF16)


def _fox(qT, qb, ka, vT, tq, tk):
    bsz, _, seq = qT.shape
    n_pair = N_MIX_HEADS // 2
    return pl.pallas_call(
        functools.partial(_fox_kernel, tq=tq, tk=tk),
        grid=(bsz, n_pair, seq // tq),
        in_specs=[
            pl.BlockSpec((1, 2 * HEAD_DIM, tq), lambda b, p, i: (b, p, i)),
            pl.BlockSpec((1, 2, BIAS_ROWS, tq), lambda b, p, i: (b, p, 0, i)),
            pl.BlockSpec((1, 1, seq, 2 * LANES), lambda b, p, i: (b, p, 0, 0)),
            pl.BlockSpec((1, 2, V_ROWS, seq), lambda b, p, i: (b, p, 0, 0)),
        ],
        out_specs=pl.BlockSpec((1, tq, LANES), lambda b, p, i: (b, i, p)),
        out_shape=jax.ShapeDtypeStruct((bsz, seq, D_MIX), BF16),
        scratch_shapes=[pltpu.VMEM((2 * LANES, 2 * tq), BF16),
                        pltpu.VMEM((1, 2 * tq), F32),
                        pltpu.VMEM((V_ROWS, 2 * tq), F32),
                        pltpu.VMEM((2, tk, 2 * tq), BF16)],
        compiler_params=pltpu.CompilerParams(
            dimension_semantics=("arbitrary", "arbitrary", "arbitrary"),
            vmem_limit_bytes=VMEM_LIMIT),
        name="fox_attention",
    )(qT, qb, ka, vT)


def kernel(x, mem, a_w_in, a_sgu_ln_g, a_sgu_ln_b, a_w_s, a_b_s, kv_w, kv_b_f, b_w_q,
           mem_w_kv, w_o, ln_g, ln_b, w_up, w_down):
    bsz, seq, d = x.shape
    t = bsz * seq
    scale = 1.0 / math.sqrt(HEAD_DIM)

    wkT = (jnp.swapaxes(mem_w_kv[:, :, :D_MEM], 1, 2) * scale).astype(BF16)
    wv = mem_w_kv[:, :, D_MEM:].astype(BF16)
    w_in = a_w_in[0].astype(BF16)
    ws_pair = (a_w_s[0].reshape(N_MIX_HEADS // 2, 2, CHUNK, CHUNK)
               .transpose(0, 2, 1, 3).reshape(N_MIX_HEADS // 2, CHUNK, 2 * CHUNK))
    bs_full = jnp.repeat(a_b_s[0].T, HEAD_DIM, axis=1)
    wq = b_w_q[0]
    wnt = jnp.concatenate([wq[:, :D_MIX].T * (scale * LOG2E), kv_w[:, D_MIX:2 * D_MIX].T], axis=0).astype(BF16)
    wf = kv_w[:, 2 * D_MIX:]
    zf = jnp.zeros((d, 16 - N_MIX_HEADS), F32)
    wf3 = jnp.concatenate([wf, zf, wf, zf, wf, zf, jnp.zeros((d, LANES - 48), F32)], axis=1)
    wnn = jnp.concatenate([kv_w[:, :D_MIX], wq[:, D_MIX:], wf3], axis=1).astype(BF16)
    zb = jnp.zeros((16 - N_MIX_HEADS,), F32)
    bf3 = jnp.concatenate([kv_b_f, zb, kv_b_f, zb, kv_b_f, zb, jnp.zeros((LANES - 48,), F32)])[None, :]
    w_o_b = w_o.astype(BF16)
    w_up_b = w_up.astype(BF16)
    w_down_b = w_down.astype(BF16)
    row = lambda v: v[None, :]

    mkT, mvb = _mem_kv(mem, wkT, wv)
    x2d = x.reshape(t, d)

    tm_a = min(1024, seq)
    mix, qm = _sgu(x2d, w_in, row(a_sgu_ln_g[0]), row(a_sgu_ln_b[0]), ws_pair, bs_full, tm_a)
    x2d = _tail(mix, qm, x2d, mkT[0], mvb[0], w_o_b[0], row(ln_g[0, 0]), row(ln_b[0, 0]), seq, tm_a)
    tm_m = min(1024, seq)
    x2d = _mlp(x2d, w_up_b[0], w_down_b[0], row(ln_g[0, 1]), row(ln_b[0, 1]), tm_m)

    tm_b = min(1024, seq)
    qT, qb, ka, vT, qm = _bproj(x2d.reshape(bsz, seq, d), wnt, wnn, bf3, tm_b)
    tq = min(1024, seq)
    att = _fox(qT, qb, ka, vT, tq, min(256, seq)).reshape(t, D_MIX)
    x2d = _tail(att, qm, x2d, mkT[1], mvb[1], w_o_b[1], row(ln_g[1, 0]), row(ln_b[1, 0]), seq, tm_a)
    x2d = _mlp(x2d, w_up_b[1], w_down_b[1], row(ln_g[1, 1]), row(ln_b[1, 1]), tm_m)
    return x2d.reshape(bsz, seq, d)
```

```python
import functools
import math

import jax
import jax.numpy as jnp
from jax import lax
from jax.experimental import pallas as pl
from jax.experimental.pallas import tpu as pltpu

D_MODEL = 1024
HEAD_DIM = 64
N_MIX_HEADS = 12
N_MEM_HEADS = 4
D_MIX = N_MIX_HEADS * HEAD_DIM
D_MEM = N_MEM_HEADS * HEAD_DIM
CHUNK = 128
N_MEM_TOKENS = 256
D_FF = 4 * D_MODEL
DEPTH = 2
DN_ALPHA = (2 * DEPTH) ** 0.25
LN_EPS = 1e-5

LANES = 128
V_ROWS = 80
BIAS_ROWS = 64
NEG_BIG = -1e30
LOG2E = math.log2(math.e)

F32 = jnp.float32
BF16 = jnp.bfloat16

VMEM_LIMIT = 56 * 1024 * 1024


def _const_spec(shape):
    nd = len(shape)
    return pl.BlockSpec(shape, lambda *_: (0,) * nd, pipeline_mode=pl.Buffered(1))


def _layer_spec(shape, layer):
    nd = len(shape)
    return pl.BlockSpec((None,) + tuple(shape), lambda *_: (layer,) + (0,) * nd,
                        pipeline_mode=pl.Buffered(1))


def _dot(a, b):
    return jnp.dot(a, b, preferred_element_type=F32)


def _dot_nt(a, b):
    return lax.dot_general(a, b, (((1,), (1,)), ((), ())), preferred_element_type=F32)


def _layer_norm(r, g, b):
    mu = jnp.mean(r, axis=-1, keepdims=True)
    d = r - mu
    var = jnp.mean(d * d, axis=-1, keepdims=True)
    return d * lax.rsqrt(var + LN_EPS) * g + b


def _split3(c):
    hi = c.astype(BF16).astype(F32)
    r1 = c - hi
    mid = r1.astype(BF16).astype(F32)
    lo = (r1 - mid).astype(BF16).astype(F32)
    return hi, mid, lo


def _mem_kv_kernel(mem_ref, wkT_ref, wv_ref, mkT_ref, mvb_ref):
    memb = mem_ref[0].astype(BF16)
    kT = _dot_nt(wkT_ref[0], memb)
    kT4 = jnp.concatenate([kT] * N_MEM_HEADS, axis=1)
    row = lax.broadcasted_iota(jnp.int32, kT4.shape, 0)
    col = lax.broadcasted_iota(jnp.int32, kT4.shape, 1)
    same = (row >> 6) == (col >> 8)
    mkT_ref[0, 0] = jnp.where(same, kT4, 0.0).astype(BF16)
    v = _dot(memb, wv_ref[0])
    v4 = jnp.concatenate([v] * N_MEM_HEADS, axis=0)
    row = lax.broadcasted_iota(jnp.int32, v4.shape, 0)
    col = lax.broadcasted_iota(jnp.int32, v4.shape, 1)
    same = (row >> 8) == (col >> 6)
    mvb_ref[0, 0] = jnp.where(same, v4, 0.0).astype(BF16)


def _mem_kv(mem, wkT, wv):
    bsz = mem.shape[0]
    m = N_MEM_TOKENS
    return pl.pallas_call(
        _mem_kv_kernel,
        grid=(DEPTH, bsz),
        in_specs=[
            pl.BlockSpec((1, m, D_MODEL), lambda l, b: (b, 0, 0)),
            pl.BlockSpec((1, D_MEM, D_MODEL), lambda l, b: (l, 0, 0)),
            pl.BlockSpec((1, D_MODEL, D_MEM), lambda l, b: (l, 0, 0)),
        ],
        out_specs=[
            pl.BlockSpec((1, 1, D_MEM, N_MEM_HEADS * m), lambda l, b: (l, b, 0, 0)),
            pl.BlockSpec((1, 1, N_MEM_HEADS * m, D_MEM), lambda l, b: (l, b, 0, 0)),
        ],
        out_shape=[
            jax.ShapeDtypeStruct((DEPTH, bsz, D_MEM, N_MEM_HEADS * m), BF16),
            jax.ShapeDtypeStruct((DEPTH, bsz, N_MEM_HEADS * m, D_MEM), BF16),
        ],
        name="mem_kv",
    )(mem, wkT, wv)


def _sgu_kernel(x_ref, win_ref, g_ref, b_ref, ws_ref, bs_ref, mix_ref, qm_ref, *, tm):
    xb = x_ref[...].astype(BF16)
    z = _dot(xb, win_ref[...])
    zu = jax.nn.gelu(z[:, :D_MIX], approximate=True)
    zv = jax.nn.gelu(z[:, D_MIX:2 * D_MIX], approximate=True)
    zv = _layer_norm(zv, g_ref[...], b_ref[...])
    qm_ref[...] = z[:, 2 * D_MIX:].astype(BF16)

    lane = lax.broadcasted_iota(jnp.int32, (CHUNK, LANES), 1)
    low = lane < HEAD_DIM
    wrow = lax.broadcasted_iota(jnp.int32, (CHUNK, 2 * CHUNK), 0)
    wcol = lax.broadcasted_iota(jnp.int32, (CHUNK, 2 * CHUNK), 1) & (CHUNK - 1)
    causal = wcol <= wrow
    for p in range(N_MIX_HEADS // 2):
        w = jnp.where(causal, ws_ref[p], 0.0).astype(BF16)
        bias = bs_ref[:, p * LANES:(p + 1) * LANES]
        for c0 in range(0, tm // CHUNK, 2):
            rhs = []
            for c in (c0, c0 + 1):
                slab = zv[c * CHUNK:(c + 1) * CHUNK, p * LANES:(p + 1) * LANES]
                rhs.append(jnp.concatenate(
                    [jnp.where(low, slab, 0.0), jnp.where(low, 0.0, slab)], axis=0))
            rhs = jnp.concatenate(rhs, axis=1).astype(BF16)
            mixed = _dot(w, rhs)
            for i, c in enumerate((c0, c0 + 1)):
                u = zu[c * CHUNK:(c + 1) * CHUNK, p * LANES:(p + 1) * LANES]
                out = u * (mixed[:, i * LANES:(i + 1) * LANES] + bias)
                mix_ref[c * CHUNK:(c + 1) * CHUNK, p * LANES:(p + 1) * LANES] = out.astype(BF16)


def _sgu(x2d, w_in, ln_g, ln_b, ws_pair, bs_full, tm):
    t = x2d.shape[0]
    n_in = w_in.shape[1]
    return pl.pallas_call(
        functools.partial(_sgu_kernel, tm=tm),
        grid=(t // tm,),
        in_specs=[
            pl.BlockSpec((tm, D_MODEL), lambda i: (i, 0)),
            _const_spec((D_MODEL, n_in)),
            _const_spec((1, D_MIX)),
            _const_spec((1, D_MIX)),
            _const_spec((N_MIX_HEADS // 2, CHUNK, 2 * CHUNK)),
            _const_spec((CHUNK, D_MIX)),
        ],
        out_specs=[pl.BlockSpec((tm, D_MIX), lambda i: (i, 0)),
                   pl.BlockSpec((tm, D_MEM), lambda i: (i, 0))],
        out_shape=[jax.ShapeDtypeStruct((t, D_MIX), BF16),
                   jax.ShapeDtypeStruct((t, D_MEM), BF16)],
        compiler_params=pltpu.CompilerParams(
            dimension_semantics=("arbitrary",), vmem_limit_bytes=VMEM_LIMIT),
        name="sgu_mixer",
    )(x2d, w_in, ln_g, ln_b, ws_pair, bs_full)


def _tail_kernel(mix_ref, qm_ref, x_ref, mkT_ref, mvb_ref, wo_ref, g_ref, b_ref, o_ref):
    m = N_MEM_TOKENS
    s = _dot(qm_ref[...], mkT_ref[0])
    ps = []
    for h in range(N_MEM_HEADS):
        sh = s[:, h * m:(h + 1) * m]
        e = jnp.exp(sh - jnp.max(sh, axis=-1, keepdims=True))
        ps.append((e / jnp.sum(e, axis=-1, keepdims=True)).astype(BF16))
    p = jnp.concatenate(ps, axis=1)
    mo = _dot(p, mvb_ref[0]).astype(BF16)
    y = _dot(mix_ref[...], wo_ref[:D_MIX, :]) + _dot(mo, wo_ref[D_MIX:, :])
    r = DN_ALPHA * x_ref[...] + y
    o_ref[...] = _layer_norm(r, g_ref[...], b_ref[...])


def _tail(mix, qm, x2d, mkT, mvb, w_o, ln_g, ln_b, seq, tm, layer):
    t = x2d.shape[0]
    m = N_MEM_TOKENS
    per_b = seq // tm
    return pl.pallas_call(
        _tail_kernel,
        grid=(t // tm,),
        in_specs=[
            pl.BlockSpec((tm, D_MIX), lambda i: (i, 0)),
            pl.BlockSpec((tm, D_MEM), lambda i: (i, 0)),
            pl.BlockSpec((tm, D_MODEL), lambda i: (i, 0)),
            pl.BlockSpec((None, 1, D_MEM, N_MEM_HEADS * m), lambda i: (layer, i // per_b, 0, 0)),
            pl.BlockSpec((None, 1, N_MEM_HEADS * m, D_MEM), lambda i: (layer, i // per_b, 0, 0)),
            _layer_spec((D_MODEL, D_MODEL), layer),
            _layer_spec((1, D_MODEL), 2 * layer),
            _layer_spec((1, D_MODEL), 2 * layer),
        ],
        out_specs=pl.BlockSpec((tm, D_MODEL), lambda i: (i, 0)),
        out_shape=jax.ShapeDtypeStruct((t, D_MODEL), F32),
        compiler_params=pltpu.CompilerParams(
            dimension_semantics=("arbitrary",), vmem_limit_bytes=VMEM_LIMIT),
        name="mixer_tail",
    )(mix, qm, x2d, mkT, mvb, w_o, ln_g, ln_b)


def _mlp_kernel(x_ref, wu_ref, wd_ref, g_ref, b_ref, o_ref, h_ref, *, n_chunk):
    x = x_ref[...]
    xb = x.astype(BF16)
    for c in range(D_FF // n_chunk):
        h = _dot(xb, wu_ref[:, c * n_chunk:(c + 1) * n_chunk])
        h = jnp.maximum(h, 0.0)
        h_ref[:, c * n_chunk:(c + 1) * n_chunk] = (h * h).astype(BF16)
    y = _dot(h_ref[...], wd_ref[...])
    r = DN_ALPHA * x + y
    o_ref[...] = _layer_norm(r, g_ref[...], b_ref[...])


def _mlp(x2d, w_up, w_down, ln_g, ln_b, tm, layer, n_chunk=1024):
    t = x2d.shape[0]
    return pl.pallas_call(
        functools.partial(_mlp_kernel, n_chunk=n_chunk),
        grid=(t // tm,),
        in_specs=[
            pl.BlockSpec((tm, D_MODEL), lambda i: (i, 0)),
            _layer_spec((D_MODEL, D_FF), layer),
            _layer_spec((D_FF, D_MODEL), layer),
            _layer_spec((1, D_MODEL), 2 * layer + 1),
            _layer_spec((1, D_MODEL), 2 * layer + 1),
        ],
        out_specs=pl.BlockSpec((tm, D_MODEL), lambda i: (i, 0)),
        out_shape=jax.ShapeDtypeStruct((t, D_MODEL), F32),
        scratch_shapes=[pltpu.VMEM((tm, D_FF), BF16)],
        compiler_params=pltpu.CompilerParams(
            dimension_semantics=("arbitrary",), vmem_limit_bytes=VMEM_LIMIT),
        name="relu2_mlp",
    )(x2d, w_up, w_down, ln_g, ln_b)


def _log_sigmoid(x):
    return jnp.minimum(x, 0.0) - jnp.log(1.0 + jnp.exp(-jnp.abs(x)))


def _bproj_kernel(x_ref, wnt_ref, wnn_ref, bf_ref,
                  qT_ref, qb_ref, ka_ref, vT_ref, qm_ref, carry_ref, *, tm):
    @pl.when(pl.program_id(1) == 0)
    def _():
        carry_ref[...] = jnp.zeros_like(carry_ref)

    xb = x_ref[...].astype(BF16)
    nt = _dot_nt(wnt_ref[...], xb)
    qT_ref[0] = nt[:D_MIX].astype(BF16)
    vrow = lax.broadcasted_iota(jnp.int32, (V_ROWS - HEAD_DIM, tm), 0)
    ones_rows = jnp.where(vrow == 0, 1.0, 0.0).astype(BF16)
    for h in range(N_MIX_HEADS):
        vT_ref[0, h, :HEAD_DIM, :] = nt[D_MIX + h * HEAD_DIM:D_MIX + (h + 1) * HEAD_DIM].astype(BF16)
        vT_ref[0, h, HEAD_DIM:, :] = ones_rows

    nn = _dot(xb, wnn_ref[...])
    qm_ref[...] = nn[:, D_MIX:D_MIX + D_MEM].astype(BF16)

    f3 = nn[:, D_MIX + D_MEM:] + bf_ref[...]
    lane = lax.broadcasted_iota(jnp.int32, (tm, LANES), 1)
    valid = (lane & 15) < N_MIX_HEADS
    valid = valid & (lane < 48)
    c = jnp.where(valid, _log_sigmoid(f3) * LOG2E, 0.0)
    rowi = lax.broadcasted_iota(jnp.int32, (tm, LANES), 0)
    d = 1
    while d < tm:
        c = c + jnp.where(rowi >= d, pltpu.roll(c, d, axis=0), 0.0)
        d *= 2
    c = c + carry_ref[...]
    carry_ref[...] = c[tm - 1:tm, :]

    hi, mid, lo = _split3(c)
    kb = jnp.where(lane < 16, -hi,
                   jnp.where(lane < 32, -mid,
                             jnp.where(lane < 48, -lo,
                                       jnp.where(lane < 51, 1.0, 0.0))))
    kb = kb.astype(BF16)
    for p in range(N_MIX_HEADS // 2):
        ka_ref[0, p, :, :LANES] = nn[:, p * LANES:(p + 1) * LANES].astype(BF16)
        ka_ref[0, p, :, LANES:] = kb

    cT = c.T
    brow = lax.broadcasted_iota(jnp.int32, (BIAS_ROWS, tm), 0)
    for h in range(N_MIX_HEADS):
        chi, cmid, clo = _split3(cT[h:h + 1, :])
        sel = (brow == h) | (brow == 16 + h) | (brow == 32 + h)
        tile = jnp.where(brow == 48, chi,
                         jnp.where(brow == 49, cmid,
                                   jnp.where(brow == 50, clo,
                                             jnp.where(sel, 1.0, 0.0))))
        qb_ref[0, h] = tile.astype(BF16)


def _bproj(x3d, wnt, wnn, bf3, tm):
    bsz, seq, _ = x3d.shape
    n_nn = wnn.shape[1]
    return pl.pallas_call(
        functools.partial(_bproj_kernel, tm=tm),
        grid=(bsz, seq // tm),
        in_specs=[
            pl.BlockSpec((None, tm, D_MODEL), lambda b, j: (b, j, 0)),
            _const_spec((2 * D_MIX, D_MODEL)),
            _const_spec((D_MODEL, n_nn)),
            _const_spec((1, LANES)),
        ],
        out_specs=[
            pl.BlockSpec((1, D_MIX, tm), lambda b, j: (b, 0, j)),
            pl.BlockSpec((1, N_MIX_HEADS, BIAS_ROWS, tm), lambda b, j: (b, 0, 0, j)),
            pl.BlockSpec((1, N_MIX_HEADS // 2, tm, 2 * LANES), lambda b, j: (b, 0, j, 0)),
            pl.BlockSpec((1, N_MIX_HEADS, V_ROWS, tm), lambda b, j: (b, 0, 0, j)),
            pl.BlockSpec((tm, D_MEM), lambda b, j: (b * (seq // tm) + j, 0)),
        ],
        out_shape=[
            jax.ShapeDtypeStruct((bsz, D_MIX, seq), BF16),
            jax.ShapeDtypeStruct((bsz, N_MIX_HEADS, BIAS_ROWS, seq), BF16),
            jax.ShapeDtypeStruct((bsz, N_MIX_HEADS // 2, seq, 2 * LANES), BF16),
            jax.ShapeDtypeStruct((bsz, N_MIX_HEADS, V_ROWS, seq), BF16),
            jax.ShapeDtypeStruct((bsz * seq, D_MEM), BF16),
        ],
        scratch_shapes=[pltpu.VMEM((1, LANES), F32)],
        compiler_params=pltpu.CompilerParams(
            dimension_semantics=("arbitrary", "arbitrary"), vmem_limit_bytes=VMEM_LIMIT),
        name="fox_proj",
    )(x3d, wnt, wnn, bf3)


def _fox_kernel(qT_ref, qb_ref, ka_ref, vT_ref, o_ref, qa_ref, m_ref, acc_ref, s_ref, *, tq, tk):
    qi = pl.program_id(2)
    n_sub = tq // tk
    assert n_sub % 2 == 0
    n_groups = 2 * n_sub

    srow = lax.broadcasted_iota(jnp.int32, (2 * HEAD_DIM, tk), 0)
    zpad = jnp.zeros((LANES - BIAS_ROWS, tk), BF16)
    for sub in range(n_sub):
        cols = slice(sub * tk, (sub + 1) * tk)
        sl = qT_ref[0, :, cols]
        for hh in range(2):
            g = 2 * sub + hh
            keep = (srow < HEAD_DIM) if hh == 0 else (srow >= HEAD_DIM)
            qa_ref[:, g * tk:(g + 1) * tk] = jnp.concatenate(
                [jnp.where(keep, sl, jnp.zeros_like(sl)), qb_ref[0, hh, :, cols], zpad], axis=0)

    m_ref[...] = jnp.full(m_ref.shape, NEG_BIG, F32)
    acc_ref[...] = jnp.zeros(acc_ref.shape, F32)

    tri = (lax.broadcasted_iota(jnp.int32, (tk, tk), 0)
           <= lax.broadcasted_iota(jnp.int32, (tk, tk), 1))

    def scores(i, slot, g):
        ks = pl.multiple_of(i * tk, tk)
        cols = slice(g * tk, (g + 1) * tk)
        s_ref[slot, :, cols] = _dot(ka_ref[0, 0, pl.ds(ks, tk), :], qa_ref[:, cols])

    def softmax_pv(i, slot, g, masked):
        ks = pl.multiple_of(i * tk, tk)
        cols = slice(g * tk, (g + 1) * tk)
        s = s_ref[slot, :, cols]
        if masked:
            s = jnp.where(tri, s, NEG_BIG)
        m_prev = m_ref[:, cols]
        m_new = jnp.maximum(m_prev, jnp.max(s, axis=0, keepdims=True))
        m_ref[:, cols] = m_new
        p = jnp.exp2((s - m_new).astype(BF16))
        pv = _dot(vT_ref[0, g % 2, :, pl.ds(ks, tk)], p)
        acc_ref[:, cols] = jnp.exp2(m_prev - m_new) * acc_ref[:, cols] + pv

    def block_pair(i_next, slot_next, i_cur, slot_cur, g0, diagonal):
        for g in range(g0, n_groups):
            if i_next is not None and g >= g0 + (2 if diagonal else 0):
                scores(i_next, slot_next, g)
            softmax_pv(i_cur, slot_cur, g, diagonal and g < g0 + 2)

    n_off = qi * n_sub
    for g in range(n_groups):
        scores(0, 0, g)

    def body(j, carry):
        for u in range(n_sub):
            block_pair(n_sub * j + u + 1, (u + 1) % 2, n_sub * j + u, u % 2, 0, False)
        return carry

    lax.fori_loop(0, qi, body, 0)
    for dj in range(n_sub):
        nxt = n_off + dj + 1 if dj + 1 < n_sub else None
        block_pair(nxt, (dj + 1) % 2, n_off + dj, dj % 2, 2 * dj, True)

    acc = acc_ref[...]
    o = acc[:HEAD_DIM] * (1.0 / acc[HEAD_DIM:HEAD_DIM + 1])
    o = jnp.concatenate(
        [jnp.concatenate([o[:, (2 * sub + hh) * tk:(2 * sub + hh + 1) * tk] for sub in range(n_sub)],
                         axis=1) for hh in range(2)], axis=0)
    o_ref[0] = o.T.astype(BF16)


def _fox(qT, qb, ka, vT, tq, tk):
    bsz, _, seq = qT.shape
    n_pair = N_MIX_HEADS // 2
    return pl.pallas_call(
        functools.partial(_fox_kernel, tq=tq, tk=tk),
        grid=(bsz, n_pair, seq // tq),
        in_specs=[
            pl.BlockSpec((1, 2 * HEAD_DIM, tq), lambda b, p, i: (b, p, i)),
            pl.BlockSpec((1, 2, BIAS_ROWS, tq), lambda b, p, i: (b, p, 0, i)),
            pl.BlockSpec((1, 1, seq, 2 * LANES), lambda b, p, i: (b, p, 0, 0)),
            pl.BlockSpec((1, 2, V_ROWS, seq), lambda b, p, i: (b, p, 0, 0)),
        ],
        out_specs=pl.BlockSpec((1, tq, LANES), lambda b, p, i: (b, i, p)),
        out_shape=jax.ShapeDtypeStruct((bsz, seq, D_MIX), BF16),
        scratch_shapes=[pltpu.VMEM((2 * LANES, 2 * tq), BF16),
                        pltpu.VMEM((1, 2 * tq), F32),
                        pltpu.VMEM((V_ROWS, 2 * tq), F32),
                        pltpu.VMEM((2, tk, 2 * tq), F32)],
        compiler_params=pltpu.CompilerParams(
            dimension_semantics=("arbitrary", "arbitrary", "arbitrary"),
            vmem_limit_bytes=VMEM_LIMIT),
        name="fox_attention",
    )(qT, qb, ka, vT)


def kernel(x, mem, a_w_in, a_sgu_ln_g, a_sgu_ln_b, a_w_s, a_b_s, kv_w, kv_b_f, b_w_q,
           mem_w_kv, w_o, ln_g, ln_b, w_up, w_down):
    bsz, seq, d = x.shape
    t = bsz * seq
    scale = 1.0 / math.sqrt(HEAD_DIM)

    wkT = (jnp.swapaxes(mem_w_kv[:, :, :D_MEM], 1, 2) * scale).astype(BF16)
    wv = mem_w_kv[:, :, D_MEM:].astype(BF16)
    w_in = a_w_in[0].astype(BF16)
    ws_pair = (a_w_s[0].reshape(N_MIX_HEADS // 2, 2, CHUNK, CHUNK)
               .transpose(0, 2, 1, 3).reshape(N_MIX_HEADS // 2, CHUNK, 2 * CHUNK))
    bs_full = jnp.repeat(a_b_s[0].T, HEAD_DIM, axis=1)
    wq = b_w_q[0]
    wnt = jnp.concatenate([wq[:, :D_MIX].T * (scale * LOG2E), kv_w[:, D_MIX:2 * D_MIX].T], axis=0).astype(BF16)
    wf = kv_w[:, 2 * D_MIX:]
    zf = jnp.zeros((d, 16 - N_MIX_HEADS), F32)
    wf3 = jnp.concatenate([wf, zf, wf, zf, wf, zf, jnp.zeros((d, LANES - 48), F32)], axis=1)
    wnn = jnp.concatenate([kv_w[:, :D_MIX], wq[:, D_MIX:], wf3], axis=1).astype(BF16)
    zb = jnp.zeros((16 - N_MIX_HEADS,), F32)
    bf3 = jnp.concatenate([kv_b_f, zb, kv_b_f, zb, kv_b_f, zb, jnp.zeros((LANES - 48,), F32)])[None, :]
    w_o_b = w_o.astype(BF16)
    w_up_b = w_up.astype(BF16)
    w_down_b = w_down.astype(BF16)
    row = lambda v: v[None, :]
    ln_g4 = ln_g.reshape(2 * DEPTH, 1, d)
    ln_b4 = ln_b.reshape(2 * DEPTH, 1, d)

    mkT, mvb = _mem_kv(mem, wkT, wv)
    x2d = x.reshape(t, d)

    tm_a = min(1024, seq)
    mix, qm = _sgu(x2d, w_in, row(a_sgu_ln_g[0]), row(a_sgu_ln_b[0]), ws_pair, bs_full, tm_a)
    x2d = _tail(mix, qm, x2d, mkT, mvb, w_o_b, ln_g4, ln_b4, seq, tm_a, 0)
    tm_m = min(1024, seq)
    x2d = _mlp(x2d, w_up_b, w_down_b, ln_g4, ln_b4, tm_m, 0)

    tm_b = min(1024, seq)
    qT, qb, ka, vT, qm = _bproj(x2d.reshape(bsz, seq, d), wnt, wnn, bf3, tm_b)
    tq = min(1024, seq)
    att = _fox(qT, qb, ka, vT, tq, min(256, seq)).reshape(t, D_MIX)
    x2d = _tail(att, qm, x2d, mkT, mvb, w_o_b, ln_g4, ln_b4, seq, tm_a, 1)
    x2d = _mlp(x2d, w_up_b, w_down_b, ln_g4, ln_b4, tm_m, 1)
    return x2d.reshape(bsz, seq, d)
```

```python
import functools
import math

import jax
import jax.numpy as jnp
from jax import lax
from jax.experimental import pallas as pl
from jax.experimental.pallas import tpu as pltpu

D_MODEL = 1024
HEAD_DIM = 64
N_MIX_HEADS = 12
N_MEM_HEADS = 4
D_MIX = N_MIX_HEADS * HEAD_DIM
D_MEM = N_MEM_HEADS * HEAD_DIM
CHUNK = 128
N_MEM_TOKENS = 256
D_FF = 4 * D_MODEL
DEPTH = 2
DN_ALPHA = (2 * DEPTH) ** 0.25
LN_EPS = 1e-5

LANES = 128
V_ROWS = 80
BIAS_ROWS = 64
NEG_BIG = -1e30
LOG2E = math.log2(math.e)

F32 = jnp.float32
BF16 = jnp.bfloat16

VMEM_LIMIT = 56 * 1024 * 1024


def _const_spec(shape):
    nd = len(shape)
    return pl.BlockSpec(shape, lambda *_: (0,) * nd, pipeline_mode=pl.Buffered(1))


def _layer_spec(shape, layer):
    nd = len(shape)
    return pl.BlockSpec((None,) + tuple(shape), lambda *_: (layer,) + (0,) * nd,
                        pipeline_mode=pl.Buffered(1))


def _dot(a, b):
    return jnp.dot(a, b, preferred_element_type=F32)


def _dot_nt(a, b):
    return lax.dot_general(a, b, (((1,), (1,)), ((), ())), preferred_element_type=F32)


def _layer_norm(r, g, b):
    mu = jnp.mean(r, axis=-1, keepdims=True)
    d = r - mu
    var = jnp.mean(d * d, axis=-1, keepdims=True)
    return d * lax.rsqrt(var + LN_EPS) * g + b


def _split3(c):
    hi = c.astype(BF16).astype(F32)
    r1 = c - hi
    mid = r1.astype(BF16).astype(F32)
    lo = (r1 - mid).astype(BF16).astype(F32)
    return hi, mid, lo


def _mem_kv_kernel(mem_ref, wkT_ref, wv_ref, mkT_ref, mvb_ref):
    memb = mem_ref[0].astype(BF16)
    kT = _dot_nt(wkT_ref[0], memb)
    kT4 = jnp.concatenate([kT] * N_MEM_HEADS, axis=1)
    row = lax.broadcasted_iota(jnp.int32, kT4.shape, 0)
    col = lax.broadcasted_iota(jnp.int32, kT4.shape, 1)
    same = (row >> 6) == (col >> 8)
    mkT_ref[0, 0] = jnp.where(same, kT4, 0.0).astype(BF16)
    v = _dot(memb, wv_ref[0])
    v4 = jnp.concatenate([v] * N_MEM_HEADS, axis=0)
    row = lax.broadcasted_iota(jnp.int32, v4.shape, 0)
    col = lax.broadcasted_iota(jnp.int32, v4.shape, 1)
    same = (row >> 8) == (col >> 6)
    mvb_ref[0, 0] = jnp.where(same, v4, 0.0).astype(BF16)


def _mem_kv(mem, wkT, wv):
    bsz = mem.shape[0]
    m = N_MEM_TOKENS
    return pl.pallas_call(
        _mem_kv_kernel,
        grid=(DEPTH, bsz),
        in_specs=[
            pl.BlockSpec((1, m, D_MODEL), lambda l, b: (b, 0, 0)),
            pl.BlockSpec((1, D_MEM, D_MODEL), lambda l, b: (l, 0, 0)),
            pl.BlockSpec((1, D_MODEL, D_MEM), lambda l, b: (l, 0, 0)),
        ],
        out_specs=[
            pl.BlockSpec((1, 1, D_MEM, N_MEM_HEADS * m), lambda l, b: (l, b, 0, 0)),
            pl.BlockSpec((1, 1, N_MEM_HEADS * m, D_MEM), lambda l, b: (l, b, 0, 0)),
        ],
        out_shape=[
            jax.ShapeDtypeStruct((DEPTH, bsz, D_MEM, N_MEM_HEADS * m), BF16),
            jax.ShapeDtypeStruct((DEPTH, bsz, N_MEM_HEADS * m, D_MEM), BF16),
        ],
        name="mem_kv",
    )(mem, wkT, wv)


def _sgu_kernel(x_ref, win_ref, g_ref, b_ref, ws_ref, bs_ref, mix_ref, qm_ref, *, tm):
    xb = x_ref[...].astype(BF16)
    z = _dot(xb, win_ref[...])
    zu = jax.nn.gelu(z[:, :D_MIX], approximate=True)
    zv = jax.nn.gelu(z[:, D_MIX:2 * D_MIX], approximate=True)
    zv = _layer_norm(zv, g_ref[...], b_ref[...])
    qm_ref[...] = z[:, 2 * D_MIX:].astype(BF16)

    lane = lax.broadcasted_iota(jnp.int32, (CHUNK, LANES), 1)
    low = lane < HEAD_DIM
    wrow = lax.broadcasted_iota(jnp.int32, (CHUNK, 2 * CHUNK), 0)
    wcol = lax.broadcasted_iota(jnp.int32, (CHUNK, 2 * CHUNK), 1) & (CHUNK - 1)
    causal = wcol <= wrow
    for p in range(N_MIX_HEADS // 2):
        w = jnp.where(causal, ws_ref[p], 0.0).astype(BF16)
        bias = bs_ref[:, p * LANES:(p + 1) * LANES]
        for c0 in range(0, tm // CHUNK, 2):
            rhs = []
            for c in (c0, c0 + 1):
                slab = zv[c * CHUNK:(c + 1) * CHUNK, p * LANES:(p + 1) * LANES]
                rhs.append(jnp.concatenate(
                    [jnp.where(low, slab, 0.0), jnp.where(low, 0.0, slab)], axis=0))
            rhs = jnp.concatenate(rhs, axis=1).astype(BF16)
            mixed = _dot(w, rhs)
            for i, c in enumerate((c0, c0 + 1)):
                u = zu[c * CHUNK:(c + 1) * CHUNK, p * LANES:(p + 1) * LANES]
                out = u * (mixed[:, i * LANES:(i + 1) * LANES] + bias)
                mix_ref[c * CHUNK:(c + 1) * CHUNK, p * LANES:(p + 1) * LANES] = out.astype(BF16)


def _sgu(x2d, w_in, ln_g, ln_b, ws_pair, bs_full, tm):
    t = x2d.shape[0]
    n_in = w_in.shape[1]
    return pl.pallas_call(
        functools.partial(_sgu_kernel, tm=tm),
        grid=(t // tm,),
        in_specs=[
            pl.BlockSpec((tm, D_MODEL), lambda i: (i, 0)),
            _const_spec((D_MODEL, n_in)),
            _const_spec((1, D_MIX)),
            _const_spec((1, D_MIX)),
            _const_spec((N_MIX_HEADS // 2, CHUNK, 2 * CHUNK)),
            _const_spec((CHUNK, D_MIX)),
        ],
        out_specs=[pl.BlockSpec((tm, D_MIX), lambda i: (i, 0)),
                   pl.BlockSpec((tm, D_MEM), lambda i: (i, 0))],
        out_shape=[jax.ShapeDtypeStruct((t, D_MIX), BF16),
                   jax.ShapeDtypeStruct((t, D_MEM), BF16)],
        compiler_params=pltpu.CompilerParams(
            dimension_semantics=("arbitrary",), vmem_limit_bytes=VMEM_LIMIT),
        name="sgu_mixer",
    )(x2d, w_in, ln_g, ln_b, ws_pair, bs_full)


def _tail_kernel(mix_ref, qm_ref, x_ref, mkT_ref, mvb_ref, wo_ref, g_ref, b_ref, o_ref):
    m = N_MEM_TOKENS
    s = _dot(qm_ref[...], mkT_ref[0])
    ps = []
    for h in range(N_MEM_HEADS):
        sh = s[:, h * m:(h + 1) * m]
        e = jnp.exp(sh - jnp.max(sh, axis=-1, keepdims=True))
        ps.append((e / jnp.sum(e, axis=-1, keepdims=True)).astype(BF16))
    p = jnp.concatenate(ps, axis=1)
    mo = _dot(p, mvb_ref[0]).astype(BF16)
    y = _dot(mix_ref[...], wo_ref[:D_MIX, :]) + _dot(mo, wo_ref[D_MIX:, :])
    r = DN_ALPHA * x_ref[...] + y
    o_ref[...] = _layer_norm(r, g_ref[...], b_ref[...])


def _tail(mix, qm, x2d, mkT, mvb, w_o, ln_g, ln_b, seq, tm, layer):
    t = x2d.shape[0]
    m = N_MEM_TOKENS
    per_b = seq // tm
    return pl.pallas_call(
        _tail_kernel,
        grid=(t // tm,),
        in_specs=[
            pl.BlockSpec((tm, D_MIX), lambda i: (i, 0)),
            pl.BlockSpec((tm, D_MEM), lambda i: (i, 0)),
            pl.BlockSpec((tm, D_MODEL), lambda i: (i, 0)),
            pl.BlockSpec((None, 1, D_MEM, N_MEM_HEADS * m), lambda i: (layer, i // per_b, 0, 0)),
            pl.BlockSpec((None, 1, N_MEM_HEADS * m, D_MEM), lambda i: (layer, i // per_b, 0, 0)),
            _layer_spec((D_MODEL, D_MODEL), layer),
            _layer_spec((1, D_MODEL), 2 * layer),
            _layer_spec((1, D_MODEL), 2 * layer),
        ],
        out_specs=pl.BlockSpec((tm, D_MODEL), lambda i: (i, 0)),
        out_shape=jax.ShapeDtypeStruct((t, D_MODEL), F32),
        compiler_params=pltpu.CompilerParams(
            dimension_semantics=("arbitrary",), vmem_limit_bytes=VMEM_LIMIT),
        name="mixer_tail",
    )(mix, qm, x2d, mkT, mvb, w_o, ln_g, ln_b)


def _mlp_kernel(x_ref, wu_ref, wd_ref, g_ref, b_ref, o_ref, h_ref, *, n_chunk):
    x = x_ref[...]
    xb = x.astype(BF16)
    for c in range(D_FF // n_chunk):
        h = _dot(xb, wu_ref[:, c * n_chunk:(c + 1) * n_chunk])
        h = jnp.maximum(h, 0.0)
        h_ref[:, c * n_chunk:(c + 1) * n_chunk] = (h * h).astype(BF16)
    y = _dot(h_ref[...], wd_ref[...])
    r = DN_ALPHA * x + y
    o_ref[...] = _layer_norm(r, g_ref[...], b_ref[...])


def _mlp(x2d, w_up, w_down, ln_g, ln_b, tm, layer, n_chunk=1024):
    t = x2d.shape[0]
    return pl.pallas_call(
        functools.partial(_mlp_kernel, n_chunk=n_chunk),
        grid=(t // tm,),
        in_specs=[
            pl.BlockSpec((tm, D_MODEL), lambda i: (i, 0)),
            _layer_spec((D_MODEL, D_FF), layer),
            _layer_spec((D_FF, D_MODEL), layer),
            _layer_spec((1, D_MODEL), 2 * layer + 1),
            _layer_spec((1, D_MODEL), 2 * layer + 1),
        ],
        out_specs=pl.BlockSpec((tm, D_MODEL), lambda i: (i, 0)),
        out_shape=jax.ShapeDtypeStruct((t, D_MODEL), F32),
        scratch_shapes=[pltpu.VMEM((tm, D_FF), BF16)],
        compiler_params=pltpu.CompilerParams(
            dimension_semantics=("arbitrary",), vmem_limit_bytes=VMEM_LIMIT),
        name="relu2_mlp",
    )(x2d, w_up, w_down, ln_g, ln_b)


def _log_sigmoid(x):
    return jnp.minimum(x, 0.0) - jnp.log(1.0 + jnp.exp(-jnp.abs(x)))


def _bproj_kernel(x_ref, wnt_ref, wnn_ref, bf_ref,
                  qT_ref, qb_ref, ka_ref, vT_ref, qm_ref, carry_ref, *, tm):
    @pl.when(pl.program_id(1) == 0)
    def _():
        carry_ref[...] = jnp.zeros_like(carry_ref)

    xb = x_ref[...].astype(BF16)
    nt = _dot_nt(wnt_ref[...], xb)
    qT_ref[0] = nt[:D_MIX].astype(BF16)
    vrow = lax.broadcasted_iota(jnp.int32, (V_ROWS - HEAD_DIM, tm), 0)
    ones_rows = jnp.where(vrow == 0, 1.0, 0.0).astype(BF16)
    for h in range(N_MIX_HEADS):
        vT_ref[0, h, :HEAD_DIM, :] = nt[D_MIX + h * HEAD_DIM:D_MIX + (h + 1) * HEAD_DIM].astype(BF16)
        vT_ref[0, h, HEAD_DIM:, :] = ones_rows

    nn = _dot(xb, wnn_ref[...])
    qm_ref[...] = nn[:, D_MIX:D_MIX + D_MEM].astype(BF16)

    f3 = nn[:, D_MIX + D_MEM:] + bf_ref[...]
    lane = lax.broadcasted_iota(jnp.int32, (tm, LANES), 1)
    valid = (lane & 15) < N_MIX_HEADS
    valid = valid & (lane < 48)
    c = jnp.where(valid, _log_sigmoid(f3) * LOG2E, 0.0)
    rowi = lax.broadcasted_iota(jnp.int32, (tm, LANES), 0)
    d = 1
    while d < tm:
        c = c + jnp.where(rowi >= d, pltpu.roll(c, d, axis=0), 0.0)
        d *= 2
    c = c + carry_ref[...]
    carry_ref[...] = c[tm - 1:tm, :]

    hi, mid, lo = _split3(c)
    kb = jnp.where(lane < 16, -hi,
                   jnp.where(lane < 32, -mid,
                             jnp.where(lane < 48, -lo,
                                       jnp.where(lane < 51, 1.0, 0.0))))
    kb = kb.astype(BF16)
    for p in range(N_MIX_HEADS // 2):
        ka_ref[0, p, :, :LANES] = nn[:, p * LANES:(p + 1) * LANES].astype(BF16)
        ka_ref[0, p, :, LANES:] = kb

    cT = c.T
    brow = lax.broadcasted_iota(jnp.int32, (BIAS_ROWS, tm), 0)
    for h in range(N_MIX_HEADS):
        chi, cmid, clo = _split3(cT[h:h + 1, :])
        sel = (brow == h) | (brow == 16 + h) | (brow == 32 + h)
        tile = jnp.where(brow == 48, chi,
                         jnp.where(brow == 49, cmid,
                                   jnp.where(brow == 50, clo,
                                             jnp.where(sel, 1.0, 0.0))))
        qb_ref[0, h] = tile.astype(BF16)


def _bproj(x3d, wnt, wnn, bf3, tm):
    bsz, seq, _ = x3d.shape
    n_nn = wnn.shape[1]
    return pl.pallas_call(
        functools.partial(_bproj_kernel, tm=tm),
        grid=(bsz, seq // tm),
        in_specs=[
            pl.BlockSpec((None, tm, D_MODEL), lambda b, j: (b, j, 0)),
            _const_spec((2 * D_MIX, D_MODEL)),
            _const_spec((D_MODEL, n_nn)),
            _const_spec((1, LANES)),
        ],
        out_specs=[
            pl.BlockSpec((1, D_MIX, tm), lambda b, j: (b, 0, j)),
            pl.BlockSpec((1, N_MIX_HEADS, BIAS_ROWS, tm), lambda b, j: (b, 0, 0, j)),
            pl.BlockSpec((1, N_MIX_HEADS // 2, tm, 2 * LANES), lambda b, j: (b, 0, j, 0)),
            pl.BlockSpec((1, N_MIX_HEADS, V_ROWS, tm), lambda b, j: (b, 0, 0, j)),
            pl.BlockSpec((tm, D_MEM), lambda b, j: (b * (seq // tm) + j, 0)),
        ],
        out_shape=[
            jax.ShapeDtypeStruct((bsz, D_MIX, seq), BF16),
            jax.ShapeDtypeStruct((bsz, N_MIX_HEADS, BIAS_ROWS, seq), BF16),
            jax.ShapeDtypeStruct((bsz, N_MIX_HEADS // 2, seq, 2 * LANES), BF16),
            jax.ShapeDtypeStruct((bsz, N_MIX_HEADS, V_ROWS, seq), BF16),
            jax.ShapeDtypeStruct((bsz * seq, D_MEM), BF16),
        ],
        scratch_shapes=[pltpu.VMEM((1, LANES), F32)],
        compiler_params=pltpu.CompilerParams(
            dimension_semantics=("arbitrary", "arbitrary"), vmem_limit_bytes=VMEM_LIMIT),
        name="fox_proj",
    )(x3d, wnt, wnn, bf3)


def _fox_kernel(qT_ref, qb_ref, ka_ref, vT_ref, o_ref, qa_ref, m_ref, acc_ref, s_ref, *, tq, tk):
    qi = pl.program_id(2)
    n_sub = tq // tk
    assert n_sub % 2 == 0
    n_groups = 2 * n_sub

    srow = lax.broadcasted_iota(jnp.int32, (2 * HEAD_DIM, tk), 0)
    zpad = jnp.zeros((LANES - BIAS_ROWS, tk), BF16)
    for sub in range(n_sub):
        cols = slice(sub * tk, (sub + 1) * tk)
        sl = qT_ref[0, :, cols]
        for hh in range(2):
            g = 2 * sub + hh
            keep = (srow < HEAD_DIM) if hh == 0 else (srow >= HEAD_DIM)
            qa_ref[g] = jnp.concatenate(
                [jnp.where(keep, sl, jnp.zeros_like(sl)), qb_ref[0, hh, :, cols], zpad], axis=0)

    m_ref[...] = jnp.full(m_ref.shape, NEG_BIG, F32)
    acc_ref[...] = jnp.zeros(acc_ref.shape, F32)

    tri = (lax.broadcasted_iota(jnp.int32, (tk, tk), 0)
           <= lax.broadcasted_iota(jnp.int32, (tk, tk), 1))

    def scores(i, slot, g):
        ks = pl.multiple_of(i * tk, tk)
        s_ref[slot, g] = _dot(ka_ref[0, 0, pl.ds(ks, tk), :], qa_ref[g])

    def softmax_pv(i, slot, g, masked):
        ks = pl.multiple_of(i * tk, tk)
        s = s_ref[slot, g]
        if masked:
            s = jnp.where(tri, s, NEG_BIG)
        m_prev = m_ref[g]
        m_new = jnp.maximum(m_prev, jnp.max(s, axis=0, keepdims=True))
        m_ref[g] = m_new
        p = jnp.exp2((s - m_new).astype(BF16))
        pv = _dot(vT_ref[0, g % 2, :, pl.ds(ks, tk)], p)
        acc_ref[g] = jnp.exp2(m_prev - m_new) * acc_ref[g] + pv

    def block_pair(i_next, slot_next, i_cur, slot_cur, g0, diagonal):
        for g in range(g0, n_groups):
            if i_next is not None and g >= g0 + (2 if diagonal else 0):
                scores(i_next, slot_next, g)
            softmax_pv(i_cur, slot_cur, g, diagonal and g < g0 + 2)

    n_off = qi * n_sub
    for g in range(n_groups):
        scores(0, 0, g)

    def body(j, carry):
        for u in range(n_sub):
            block_pair(n_sub * j + u + 1, (u + 1) % 2, n_sub * j + u, u % 2, 0, False)
        return carry

    lax.fori_loop(0, qi, body, 0)
    for dj in range(n_sub):
        nxt = n_off + dj + 1 if dj + 1 < n_sub else None
        block_pair(nxt, (dj + 1) % 2, n_off + dj, dj % 2, 2 * dj, True)

    def normalized(g):
        acc = acc_ref[g]
        return acc[:HEAD_DIM] * (1.0 / acc[HEAD_DIM:HEAD_DIM + 1])

    o = jnp.concatenate(
        [jnp.concatenate([normalized(2 * sub + hh) for sub in range(n_sub)], axis=1)
         for hh in range(2)], axis=0)
    o_ref[0] = o.T.astype(BF16)


def _fox(qT, qb, ka, vT, tq, tk):
    bsz, _, seq = qT.shape
    n_pair = N_MIX_HEADS // 2
    return pl.pallas_call(
        functools.partial(_fox_kernel, tq=tq, tk=tk),
        grid=(bsz, n_pair, seq // tq),
        in_specs=[
            pl.BlockSpec((1, 2 * HEAD_DIM, tq), lambda b, p, i: (b, p, i)),
            pl.BlockSpec((1, 2, BIAS_ROWS, tq), lambda b, p, i: (b, p, 0, i)),
            pl.BlockSpec((1, 1, seq, 2 * LANES), lambda b, p, i: (b, p, 0, 0)),
            pl.BlockSpec((1, 2, V_ROWS, seq), lambda b, p, i: (b, p, 0, 0)),
        ],
        out_specs=pl.BlockSpec((1, tq, LANES), lambda b, p, i: (b, i, p)),
        out_shape=jax.ShapeDtypeStruct((bsz, seq, D_MIX), BF16),
        scratch_shapes=[pltpu.VMEM((2 * tq // tk, 2 * LANES, tk), BF16),
                        pltpu.VMEM((2 * tq // tk, 1, tk), F32),
                        pltpu.VMEM((2 * tq // tk, V_ROWS, tk), F32),
                        pltpu.VMEM((2, 2 * tq // tk, tk, tk), F32)],
        compiler_params=pltpu.CompilerParams(
            dimension_semantics=("arbitrary", "arbitrary", "arbitrary"),
            vmem_limit_bytes=VMEM_LIMIT),
        name="fox_attention",
    )(qT, qb, ka, vT)


def kernel(x, mem, a_w_in, a_sgu_ln_g, a_sgu_ln_b, a_w_s, a_b_s, kv_w, kv_b_f, b_w_q,
           mem_w_kv, w_o, ln_g, ln_b, w_up, w_down):
    bsz, seq, d = x.shape
    t = bsz * seq
    scale = 1.0 / math.sqrt(HEAD_DIM)

    wkT = (jnp.swapaxes(mem_w_kv[:, :, :D_MEM], 1, 2) * scale).astype(BF16)
    wv = mem_w_kv[:, :, D_MEM:].astype(BF16)
    w_in = a_w_in[0].astype(BF16)
    ws_pair = (a_w_s[0].reshape(N_MIX_HEADS // 2, 2, CHUNK, CHUNK)
               .transpose(0, 2, 1, 3).reshape(N_MIX_HEADS // 2, CHUNK, 2 * CHUNK))
    bs_full = jnp.repeat(a_b_s[0].T, HEAD_DIM, axis=1)
    wq = b_w_q[0]
    wnt = jnp.concatenate([wq[:, :D_MIX].T * (scale * LOG2E), kv_w[:, D_MIX:2 * D_MIX].T], axis=0).astype(BF16)
    wf = kv_w[:, 2 * D_MIX:]
    zf = jnp.zeros((d, 16 - N_MIX_HEADS), F32)
    wf3 = jnp.concatenate([wf, zf, wf, zf, wf, zf, jnp.zeros((d, LANES - 48), F32)], axis=1)
    wnn = jnp.concatenate([kv_w[:, :D_MIX], wq[:, D_MIX:], wf3], axis=1).astype(BF16)
    zb = jnp.zeros((16 - N_MIX_HEADS,), F32)
    bf3 = jnp.concatenate([kv_b_f, zb, kv_b_f, zb, kv_b_f, zb, jnp.zeros((LANES - 48,), F32)])[None, :]
    w_o_b = w_o.astype(BF16)
    w_up_b = w_up.astype(BF16)
    w_down_b = w_down.astype(BF16)
    row = lambda v: v[None, :]
    ln_g4 = ln_g.reshape(2 * DEPTH, 1, d)
    ln_b4 = ln_b.reshape(2 * DEPTH, 1, d)

    mkT, mvb = _mem_kv(mem, wkT, wv)
    x2d = x.reshape(t, d)

    tm_a = min(1024, seq)
    mix, qm = _sgu(x2d, w_in, row(a_sgu_ln_g[0]), row(a_sgu_ln_b[0]), ws_pair, bs_full, tm_a)
    x2d = _tail(mix, qm, x2d, mkT, mvb, w_o_b, ln_g4, ln_b4, seq, tm_a, 0)
    tm_m = min(1024, seq)
    x2d = _mlp(x2d, w_up_b, w_down_b, ln_g4, ln_b4, tm_m, 0)

    tm_b = min(1024, seq)
    qT, qb, ka, vT, qm = _bproj(x2d.reshape(bsz, seq, d), wnt, wnn, bf3, tm_b)
    tq = min(1024, seq)
    att = _fox(qT, qb, ka, vT, tq, min(256, seq)).reshape(t, D_MIX)
    x2d = _tail(att, qm, x2d, mkT, mvb, w_o_b, ln_g4, ln_b4, seq, tm_a, 1)
    x2d = _mlp(x2d, w_up_b, w_down_b, ln_g4, ln_b4, tm_m, 1)
    return x2d.reshape(bsz, seq, d)
```

```python
import functools
import math

import jax
import jax.numpy as jnp
from jax import lax
from jax.experimental import pallas as pl
from jax.experimental.pallas import tpu as pltpu

D_MODEL = 1024
HEAD_DIM = 64
N_MIX_HEADS = 12
N_MEM_HEADS = 4
D_MIX = N_MIX_HEADS * HEAD_DIM
D_MEM = N_MEM_HEADS * HEAD_DIM
CHUNK = 128
N_MEM_TOKENS = 256
D_FF = 4 * D_MODEL
DEPTH = 2
DN_ALPHA = (2 * DEPTH) ** 0.25
LN_EPS = 1e-5

LANES = 128
V_ROWS = 80
BIAS_ROWS = 64
NEG_BIG = -1e30
LOG2E = math.log2(math.e)

F32 = jnp.float32
BF16 = jnp.bfloat16

VMEM_LIMIT = 56 * 1024 * 1024


def _const_spec(shape):
    nd = len(shape)
    return pl.BlockSpec(shape, lambda *_: (0,) * nd, pipeline_mode=pl.Buffered(1))


def _layer_spec(shape, layer):
    nd = len(shape)
    return pl.BlockSpec((None,) + tuple(shape), lambda *_: (layer,) + (0,) * nd,
                        pipeline_mode=pl.Buffered(1))


def _dot(a, b):
    return jnp.dot(a, b, preferred_element_type=F32)


def _dot_nt(a, b):
    return lax.dot_general(a, b, (((1,), (1,)), ((), ())), preferred_element_type=F32)


def _layer_norm(r, g, b):
    mu = jnp.mean(r, axis=-1, keepdims=True)
    d = r - mu
    var = jnp.mean(d * d, axis=-1, keepdims=True)
    return d * lax.rsqrt(var + LN_EPS) * g + b


def _split3(c):
    hi = c.astype(BF16).astype(F32)
    r1 = c - hi
    mid = r1.astype(BF16).astype(F32)
    lo = (r1 - mid).astype(BF16).astype(F32)
    return hi, mid, lo


def _mem_kv_kernel(mem_ref, wkT_ref, wv_ref, mkT_ref, mvb_ref):
    memb = mem_ref[0].astype(BF16)
    kT = _dot_nt(wkT_ref[0], memb)
    kT4 = jnp.concatenate([kT] * N_MEM_HEADS, axis=1)
    row = lax.broadcasted_iota(jnp.int32, kT4.shape, 0)
    col = lax.broadcasted_iota(jnp.int32, kT4.shape, 1)
    same = (row >> 6) == (col >> 8)
    mkT_ref[0, 0] = jnp.where(same, kT4, 0.0).astype(BF16)
    v = _dot(memb, wv_ref[0])
    v4 = jnp.concatenate([v] * N_MEM_HEADS, axis=0)
    row = lax.broadcasted_iota(jnp.int32, v4.shape, 0)
    col = lax.broadcasted_iota(jnp.int32, v4.shape, 1)
    same = (row >> 8) == (col >> 6)
    mvb_ref[0, 0] = jnp.where(same, v4, 0.0).astype(BF16)


def _mem_kv(mem, wkT, wv):
    bsz = mem.shape[0]
    m = N_MEM_TOKENS
    return pl.pallas_call(
        _mem_kv_kernel,
        grid=(DEPTH, bsz),
        in_specs=[
            pl.BlockSpec((1, m, D_MODEL), lambda l, b: (b, 0, 0)),
            pl.BlockSpec((1, D_MEM, D_MODEL), lambda l, b: (l, 0, 0)),
            pl.BlockSpec((1, D_MODEL, D_MEM), lambda l, b: (l, 0, 0)),
        ],
        out_specs=[
            pl.BlockSpec((1, 1, D_MEM, N_MEM_HEADS * m), lambda l, b: (l, b, 0, 0)),
            pl.BlockSpec((1, 1, N_MEM_HEADS * m, D_MEM), lambda l, b: (l, b, 0, 0)),
        ],
        out_shape=[
            jax.ShapeDtypeStruct((DEPTH, bsz, D_MEM, N_MEM_HEADS * m), BF16),
            jax.ShapeDtypeStruct((DEPTH, bsz, N_MEM_HEADS * m, D_MEM), BF16),
        ],
        name="mem_kv",
    )(mem, wkT, wv)


def _sgu_kernel(x_ref, win_ref, g_ref, b_ref, ws_ref, bs_ref, mix_ref, qm_ref, *, tm):
    xb = x_ref[...].astype(BF16)
    z = _dot(xb, win_ref[...])
    zu = jax.nn.gelu(z[:, :D_MIX], approximate=True)
    zv = jax.nn.gelu(z[:, D_MIX:2 * D_MIX], approximate=True)
    zv = _layer_norm(zv, g_ref[...], b_ref[...])
    qm_ref[...] = z[:, 2 * D_MIX:].astype(BF16)

    lane = lax.broadcasted_iota(jnp.int32, (CHUNK, LANES), 1)
    low = lane < HEAD_DIM
    wrow = lax.broadcasted_iota(jnp.int32, (CHUNK, 2 * CHUNK), 0)
    wcol = lax.broadcasted_iota(jnp.int32, (CHUNK, 2 * CHUNK), 1) & (CHUNK - 1)
    causal = wcol <= wrow
    for p in range(N_MIX_HEADS // 2):
        w = jnp.where(causal, ws_ref[p], 0.0).astype(BF16)
        bias = bs_ref[:, p * LANES:(p + 1) * LANES]
        for c0 in range(0, tm // CHUNK, 2):
            rhs = []
            for c in (c0, c0 + 1):
                slab = zv[c * CHUNK:(c + 1) * CHUNK, p * LANES:(p + 1) * LANES]
                rhs.append(jnp.concatenate(
                    [jnp.where(low, slab, 0.0), jnp.where(low, 0.0, slab)], axis=0))
            rhs = jnp.concatenate(rhs, axis=1).astype(BF16)
            mixed = _dot(w, rhs)
            for i, c in enumerate((c0, c0 + 1)):
                u = zu[c * CHUNK:(c + 1) * CHUNK, p * LANES:(p + 1) * LANES]
                out = u * (mixed[:, i * LANES:(i + 1) * LANES] + bias)
                mix_ref[c * CHUNK:(c + 1) * CHUNK, p * LANES:(p + 1) * LANES] = out.astype(BF16)


def _sgu(x2d, w_in, ln_g, ln_b, ws_pair, bs_full, tm):
    t = x2d.shape[0]
    n_in = w_in.shape[1]
    return pl.pallas_call(
        functools.partial(_sgu_kernel, tm=tm),
        grid=(t // tm,),
        in_specs=[
            pl.BlockSpec((tm, D_MODEL), lambda i: (i, 0)),
            _const_spec((D_MODEL, n_in)),
            _const_spec((1, D_MIX)),
            _const_spec((1, D_MIX)),
            _const_spec((N_MIX_HEADS // 2, CHUNK, 2 * CHUNK)),
            _const_spec((CHUNK, D_MIX)),
        ],
        out_specs=[pl.BlockSpec((tm, D_MIX), lambda i: (i, 0)),
                   pl.BlockSpec((tm, D_MEM), lambda i: (i, 0))],
        out_shape=[jax.ShapeDtypeStruct((t, D_MIX), BF16),
                   jax.ShapeDtypeStruct((t, D_MEM), BF16)],
        compiler_params=pltpu.CompilerParams(
            dimension_semantics=("arbitrary",), vmem_limit_bytes=VMEM_LIMIT),
        name="sgu_mixer",
    )(x2d, w_in, ln_g, ln_b, ws_pair, bs_full)


def _tail_kernel(mix_ref, qm_ref, x_ref, mkT_ref, mvb_ref, wo_ref, g_ref, b_ref, o_ref):
    m = N_MEM_TOKENS
    s = _dot(qm_ref[...], mkT_ref[0])
    ps = []
    for h in range(N_MEM_HEADS):
        sh = s[:, h * m:(h + 1) * m]
        e = jnp.exp(sh - jnp.max(sh, axis=-1, keepdims=True))
        ps.append((e / jnp.sum(e, axis=-1, keepdims=True)).astype(BF16))
    p = jnp.concatenate(ps, axis=1)
    mo = _dot(p, mvb_ref[0]).astype(BF16)
    y = _dot(mix_ref[...], wo_ref[:D_MIX, :]) + _dot(mo, wo_ref[D_MIX:, :])
    r = DN_ALPHA * x_ref[...] + y
    o_ref[...] = _layer_norm(r, g_ref[...], b_ref[...])


def _tail(mix, qm, x2d, mkT, mvb, w_o, ln_g, ln_b, seq, tm, layer):
    t = x2d.shape[0]
    m = N_MEM_TOKENS
    per_b = seq // tm
    return pl.pallas_call(
        _tail_kernel,
        grid=(t // tm,),
        in_specs=[
            pl.BlockSpec((tm, D_MIX), lambda i: (i, 0)),
            pl.BlockSpec((tm, D_MEM), lambda i: (i, 0)),
            pl.BlockSpec((tm, D_MODEL), lambda i: (i, 0)),
            pl.BlockSpec((None, 1, D_MEM, N_MEM_HEADS * m), lambda i: (layer, i // per_b, 0, 0)),
            pl.BlockSpec((None, 1, N_MEM_HEADS * m, D_MEM), lambda i: (layer, i // per_b, 0, 0)),
            _layer_spec((D_MODEL, D_MODEL), layer),
            _layer_spec((1, D_MODEL), 2 * layer),
            _layer_spec((1, D_MODEL), 2 * layer),
        ],
        out_specs=pl.BlockSpec((tm, D_MODEL), lambda i: (i, 0)),
        out_shape=jax.ShapeDtypeStruct((t, D_MODEL), F32),
        compiler_params=pltpu.CompilerParams(
            dimension_semantics=("arbitrary",), vmem_limit_bytes=VMEM_LIMIT),
        name="mixer_tail",
    )(mix, qm, x2d, mkT, mvb, w_o, ln_g, ln_b)


def _mlp_kernel(x_ref, wu_ref, wd_ref, g_ref, b_ref, o_ref, h_ref, *, n_chunk):
    x = x_ref[...]
    xb = x.astype(BF16)
    for c in range(D_FF // n_chunk):
        h = _dot(xb, wu_ref[:, c * n_chunk:(c + 1) * n_chunk])
        h = jnp.maximum(h, 0.0)
        h_ref[:, c * n_chunk:(c + 1) * n_chunk] = (h * h).astype(BF16)
    y = _dot(h_ref[...], wd_ref[...])
    r = DN_ALPHA * x + y
    o_ref[...] = _layer_norm(r, g_ref[...], b_ref[...])


def _mlp(x2d, w_up, w_down, ln_g, ln_b, tm, layer, n_chunk=1024):
    t = x2d.shape[0]
    return pl.pallas_call(
        functools.partial(_mlp_kernel, n_chunk=n_chunk),
        grid=(t // tm,),
        in_specs=[
            pl.BlockSpec((tm, D_MODEL), lambda i: (i, 0)),
            _layer_spec((D_MODEL, D_FF), layer),
            _layer_spec((D_FF, D_MODEL), layer),
            _layer_spec((1, D_MODEL), 2 * layer + 1),
            _layer_spec((1, D_MODEL), 2 * layer + 1),
        ],
        out_specs=pl.BlockSpec((tm, D_MODEL), lambda i: (i, 0)),
        out_shape=jax.ShapeDtypeStruct((t, D_MODEL), F32),
        scratch_shapes=[pltpu.VMEM((tm, D_FF), BF16)],
        compiler_params=pltpu.CompilerParams(
            dimension_semantics=("arbitrary",), vmem_limit_bytes=VMEM_LIMIT),
        name="relu2_mlp",
    )(x2d, w_up, w_down, ln_g, ln_b)


def _log_sigmoid(x):
    return jnp.minimum(x, 0.0) - jnp.log(1.0 + jnp.exp(-jnp.abs(x)))


def _bproj_kernel(x_ref, wnt_ref, wnn_ref, bf_ref,
                  qT_ref, qb_ref, ka_ref, vT_ref, qm_ref, carry_ref, *, tm):
    @pl.when(pl.program_id(1) == 0)
    def _():
        carry_ref[...] = jnp.zeros_like(carry_ref)

    xb = x_ref[...].astype(BF16)
    nt = _dot_nt(wnt_ref[...], xb)
    qT_ref[0] = nt[:D_MIX].astype(BF16)
    vrow = lax.broadcasted_iota(jnp.int32, (V_ROWS - HEAD_DIM, tm), 0)
    ones_rows = jnp.where(vrow == 0, 1.0, 0.0).astype(BF16)
    for h in range(N_MIX_HEADS):
        vT_ref[0, h, :HEAD_DIM, :] = nt[D_MIX + h * HEAD_DIM:D_MIX + (h + 1) * HEAD_DIM].astype(BF16)
        vT_ref[0, h, HEAD_DIM:, :] = ones_rows

    nn = _dot(xb, wnn_ref[...])
    qm_ref[...] = nn[:, D_MIX:D_MIX + D_MEM].astype(BF16)

    f3 = nn[:, D_MIX + D_MEM:] + bf_ref[...]
    lane = lax.broadcasted_iota(jnp.int32, (tm, LANES), 1)
    valid = (lane & 15) < N_MIX_HEADS
    valid = valid & (lane < 48)
    c = jnp.where(valid, _log_sigmoid(f3) * LOG2E, 0.0)
    rowi = lax.broadcasted_iota(jnp.int32, (tm, LANES), 0)
    d = 1
    while d < tm:
        c = c + jnp.where(rowi >= d, pltpu.roll(c, d, axis=0), 0.0)
        d *= 2
    c = c + carry_ref[...]
    carry_ref[...] = c[tm - 1:tm, :]

    hi, mid, lo = _split3(c)
    kb = jnp.where(lane < 16, -hi,
                   jnp.where(lane < 32, -mid,
                             jnp.where(lane < 48, -lo,
                                       jnp.where(lane < 51, 1.0, 0.0))))
    kb = kb.astype(BF16)
    for p in range(N_MIX_HEADS // 2):
        ka_ref[0, p, :, :LANES] = nn[:, p * LANES:(p + 1) * LANES].astype(BF16)
        ka_ref[0, p, :, LANES:] = kb

    cT = c.T
    brow = lax.broadcasted_iota(jnp.int32, (BIAS_ROWS, tm), 0)
    for h in range(N_MIX_HEADS):
        chi, cmid, clo = _split3(cT[h:h + 1, :])
        sel = (brow == h) | (brow == 16 + h) | (brow == 32 + h)
        tile = jnp.where(brow == 48, chi,
                         jnp.where(brow == 49, cmid,
                                   jnp.where(brow == 50, clo,
                                             jnp.where(sel, 1.0, 0.0))))
        qb_ref[0, h] = tile.astype(BF16)


def _bproj(x3d, wnt, wnn, bf3, tm):
    bsz, seq, _ = x3d.shape
    n_nn = wnn.shape[1]
    return pl.pallas_call(
        functools.partial(_bproj_kernel, tm=tm),
        grid=(bsz, seq // tm),
        in_specs=[
            pl.BlockSpec((None, tm, D_MODEL), lambda b, j: (b, j, 0)),
            _const_spec((2 * D_MIX, D_MODEL)),
            _const_spec((D_MODEL, n_nn)),
            _const_spec((1, LANES)),
        ],
        out_specs=[
            pl.BlockSpec((1, D_MIX, tm), lambda b, j: (b, 0, j)),
            pl.BlockSpec((1, N_MIX_HEADS, BIAS_ROWS, tm), lambda b, j: (b, 0, 0, j)),
            pl.BlockSpec((1, N_MIX_HEADS // 2, tm, 2 * LANES), lambda b, j: (b, 0, j, 0)),
            pl.BlockSpec((1, N_MIX_HEADS, V_ROWS, tm), lambda b, j: (b, 0, 0, j)),
            pl.BlockSpec((tm, D_MEM), lambda b, j: (b * (seq // tm) + j, 0)),
        ],
        out_shape=[
            jax.ShapeDtypeStruct((bsz, D_MIX, seq), BF16),
            jax.ShapeDtypeStruct((bsz, N_MIX_HEADS, BIAS_ROWS, seq), BF16),
            jax.ShapeDtypeStruct((bsz, N_MIX_HEADS // 2, seq, 2 * LANES), BF16),
            jax.ShapeDtypeStruct((bsz, N_MIX_HEADS, V_ROWS, seq), BF16),
            jax.ShapeDtypeStruct((bsz * seq, D_MEM), BF16),
        ],
        scratch_shapes=[pltpu.VMEM((1, LANES), F32)],
        compiler_params=pltpu.CompilerParams(
            dimension_semantics=("arbitrary", "arbitrary"), vmem_limit_bytes=VMEM_LIMIT),
        name="fox_proj",
    )(x3d, wnt, wnn, bf3)


def _fox_kernel(qT_ref, qb_ref, ka_ref, vT_ref, o_ref, m_ref, acc_ref, s_ref, *, tq, tk):
    qi = pl.program_id(2)
    n_sub = tq // tk
    assert n_sub % 2 == 0
    n_groups = 2 * n_sub

    srow = lax.broadcasted_iota(jnp.int32, (2 * HEAD_DIM, tk), 0)
    zpad = jnp.zeros((LANES - BIAS_ROWS, tk), BF16)
    qa = []
    for sub in range(n_sub):
        cols = slice(sub * tk, (sub + 1) * tk)
        sl = qT_ref[0, :, cols]
        for hh in range(2):
            keep = (srow < HEAD_DIM) if hh == 0 else (srow >= HEAD_DIM)
            qa.append(jnp.concatenate(
                [jnp.where(keep, sl, jnp.zeros_like(sl)), qb_ref[0, hh, :, cols], zpad], axis=0))

    m_ref[...] = jnp.full(m_ref.shape, NEG_BIG, F32)
    acc_ref[...] = jnp.zeros(acc_ref.shape, F32)

    tri = (lax.broadcasted_iota(jnp.int32, (tk, tk), 0)
           <= lax.broadcasted_iota(jnp.int32, (tk, tk), 1))

    def scores(i, slot, g):
        ks = pl.multiple_of(i * tk, tk)
        s_ref[slot, g] = _dot(ka_ref[0, 0, pl.ds(ks, tk), :], qa[g])

    def softmax_pv(i, slot, g, masked):
        ks = pl.multiple_of(i * tk, tk)
        s = s_ref[slot, g]
        if masked:
            s = jnp.where(tri, s, NEG_BIG)
        m_prev = m_ref[g]
        m_new = jnp.maximum(m_prev, jnp.max(s, axis=0, keepdims=True))
        m_ref[g] = m_new
        p = jnp.exp2((s - m_new).astype(BF16))
        pv = _dot(vT_ref[0, g % 2, :, pl.ds(ks, tk)], p)
        acc_ref[g] = jnp.exp2(m_prev - m_new) * acc_ref[g] + pv

    def block_pair(i_next, slot_next, i_cur, slot_cur, g0, diagonal):
        for g in range(g0, n_groups):
            if i_next is not None and g >= g0 + (2 if diagonal else 0):
                scores(i_next, slot_next, g)
            softmax_pv(i_cur, slot_cur, g, diagonal and g < g0 + 2)

    n_off = qi * n_sub
    for g in range(n_groups):
        scores(0, 0, g)

    def body(j, carry):
        for u in range(n_sub):
            block_pair(n_sub * j + u + 1, (u + 1) % 2, n_sub * j + u, u % 2, 0, False)
        return carry

    lax.fori_loop(0, qi, body, 0)
    for dj in range(n_sub):
        nxt = n_off + dj + 1 if dj + 1 < n_sub else None
        block_pair(nxt, (dj + 1) % 2, n_off + dj, dj % 2, 2 * dj, True)

    def normalized(g):
        acc = acc_ref[g]
        return acc[:HEAD_DIM] * (1.0 / acc[HEAD_DIM:HEAD_DIM + 1])

    o = jnp.concatenate(
        [jnp.concatenate([normalized(2 * sub + hh) for sub in range(n_sub)], axis=1)
         for hh in range(2)], axis=0)
    o_ref[0] = o.T.astype(BF16)


def _fox(qT, qb, ka, vT, tq, tk):
    bsz, _, seq = qT.shape
    n_pair = N_MIX_HEADS // 2
    return pl.pallas_call(
        functools.partial(_fox_kernel, tq=tq, tk=tk),
        grid=(bsz, n_pair, seq // tq),
        in_specs=[
            pl.BlockSpec((1, 2 * HEAD_DIM, tq), lambda b, p, i: (b, p, i)),
            pl.BlockSpec((1, 2, BIAS_ROWS, tq), lambda b, p, i: (b, p, 0, i)),
            pl.BlockSpec((1, 1, seq, 2 * LANES), lambda b, p, i: (b, p, 0, 0)),
            pl.BlockSpec((1, 2, V_ROWS, seq), lambda b, p, i: (b, p, 0, 0)),
        ],
        out_specs=pl.BlockSpec((1, tq, LANES), lambda b, p, i: (b, i, p)),
        out_shape=jax.ShapeDtypeStruct((bsz, seq, D_MIX), BF16),
        scratch_shapes=[pltpu.VMEM((2 * tq // tk, 1, tk), F32),
                        pltpu.VMEM((2 * tq // tk, V_ROWS, tk), F32),
                        pltpu.VMEM((2, 2 * tq // tk, tk, tk), F32)],
        compiler_params=pltpu.CompilerParams(
            dimension_semantics=("arbitrary", "arbitrary", "arbitrary"),
            vmem_limit_bytes=VMEM_LIMIT),
        name="fox_attention",
    )(qT, qb, ka, vT)


def kernel(x, mem, a_w_in, a_sgu_ln_g, a_sgu_ln_b, a_w_s, a_b_s, kv_w, kv_b_f, b_w_q,
           mem_w_kv, w_o, ln_g, ln_b, w_up, w_down):
    bsz, seq, d = x.shape
    t = bsz * seq
    scale = 1.0 / math.sqrt(HEAD_DIM)

    wkT = (jnp.swapaxes(mem_w_kv[:, :, :D_MEM], 1, 2) * scale).astype(BF16)
    wv = mem_w_kv[:, :, D_MEM:].astype(BF16)
    w_in = a_w_in[0].astype(BF16)
    ws_pair = (a_w_s[0].reshape(N_MIX_HEADS // 2, 2, CHUNK, CHUNK)
               .transpose(0, 2, 1, 3).reshape(N_MIX_HEADS // 2, CHUNK, 2 * CHUNK))
    bs_full = jnp.repeat(a_b_s[0].T, HEAD_DIM, axis=1)
    wq = b_w_q[0]
    wnt = jnp.concatenate([wq[:, :D_MIX].T * (scale * LOG2E), kv_w[:, D_MIX:2 * D_MIX].T], axis=0).astype(BF16)
    wf = kv_w[:, 2 * D_MIX:]
    zf = jnp.zeros((d, 16 - N_MIX_HEADS), F32)
    wf3 = jnp.concatenate([wf, zf, wf, zf, wf, zf, jnp.zeros((d, LANES - 48), F32)], axis=1)
    wnn = jnp.concatenate([kv_w[:, :D_MIX], wq[:, D_MIX:], wf3], axis=1).astype(BF16)
    zb = jnp.zeros((16 - N_MIX_HEADS,), F32)
    bf3 = jnp.concatenate([kv_b_f, zb, kv_b_f, zb, kv_b_f, zb, jnp.zeros((LANES - 48,), F32)])[None, :]
    w_o_b = w_o.astype(BF16)
    w_up_b = w_up.astype(BF16)
    w_down_b = w_down.astype(BF16)
    row = lambda v: v[None, :]
    ln_g4 = ln_g.reshape(2 * DEPTH, 1, d)
    ln_b4 = ln_b.reshape(2 * DEPTH, 1, d)

    mkT, mvb = _mem_kv(mem, wkT, wv)
    x2d = x.reshape(t, d)

    tm_a = min(1024, seq)
    mix, qm = _sgu(x2d, w_in, row(a_sgu_ln_g[0]), row(a_sgu_ln_b[0]), ws_pair, bs_full, tm_a)
    x2d = _tail(mix, qm, x2d, mkT, mvb, w_o_b, ln_g4, ln_b4, seq, tm_a, 0)
    tm_m = min(1024, seq)
    x2d = _mlp(x2d, w_up_b, w_down_b, ln_g4, ln_b4, tm_m, 0)

    tm_b = min(1024, seq)
    qT, qb, ka, vT, qm = _bproj(x2d.reshape(bsz, seq, d), wnt, wnn, bf3, tm_b)
    tq = min(1024, seq)
    att = _fox(qT, qb, ka, vT, tq, min(256, seq)).reshape(t, D_MIX)
    x2d = _tail(att, qm, x2d, mkT, mvb, w_o_b, ln_g4, ln_b4, seq, tm_a, 1)
    x2d = _mlp(x2d, w_up_b, w_down_b, ln_g4, ln_b4, tm_m, 1)
    return x2d.reshape(bsz, seq, d)
```

```python
import functools
import math

import jax
import jax.numpy as jnp
from jax import lax
from jax.experimental import pallas as pl
from jax.experimental.pallas import tpu as pltpu

D_MODEL = 1024
HEAD_DIM = 64
N_MIX_HEADS = 12
N_MEM_HEADS = 4
D_MIX = N_MIX_HEADS * HEAD_DIM
D_MEM = N_MEM_HEADS * HEAD_DIM
CHUNK = 128
N_MEM_TOKENS = 256
D_FF = 4 * D_MODEL
DEPTH = 2
DN_ALPHA = (2 * DEPTH) ** 0.25
LN_EPS = 1e-5

LANES = 128
V_ROWS = 80
BIAS_ROWS = 64
NEG_BIG = -1e30
LOG2E = math.log2(math.e)

F32 = jnp.float32
BF16 = jnp.bfloat16

VMEM_LIMIT = 56 * 1024 * 1024


def _const_spec(shape):
    nd = len(shape)
    return pl.BlockSpec(shape, lambda *_: (0,) * nd, pipeline_mode=pl.Buffered(1))


def _layer_spec(shape, layer):
    nd = len(shape)
    return pl.BlockSpec((None,) + tuple(shape), lambda *_: (layer,) + (0,) * nd,
                        pipeline_mode=pl.Buffered(1))


def _dot(a, b):
    return jnp.dot(a, b, preferred_element_type=F32)


def _dot_nt(a, b):
    return lax.dot_general(a, b, (((1,), (1,)), ((), ())), preferred_element_type=F32)


def _layer_norm(r, g, b):
    mu = jnp.mean(r, axis=-1, keepdims=True)
    d = r - mu
    var = jnp.mean(d * d, axis=-1, keepdims=True)
    return d * lax.rsqrt(var + LN_EPS) * g + b


def _split3(c):
    hi = c.astype(BF16).astype(F32)
    r1 = c - hi
    mid = r1.astype(BF16).astype(F32)
    lo = (r1 - mid).astype(BF16).astype(F32)
    return hi, mid, lo


def _mem_kv_kernel(mem_ref, wkT_ref, wv_ref, mkT_ref, mvb_ref):
    memb = mem_ref[0].astype(BF16)
    kT = _dot_nt(wkT_ref[0], memb)
    kT4 = jnp.concatenate([kT] * N_MEM_HEADS, axis=1)
    row = lax.broadcasted_iota(jnp.int32, kT4.shape, 0)
    col = lax.broadcasted_iota(jnp.int32, kT4.shape, 1)
    same = (row >> 6) == (col >> 8)
    mkT_ref[0, 0] = jnp.where(same, kT4, 0.0).astype(BF16)
    v = _dot(memb, wv_ref[0])
    v4 = jnp.concatenate([v] * N_MEM_HEADS, axis=0)
    row = lax.broadcasted_iota(jnp.int32, v4.shape, 0)
    col = lax.broadcasted_iota(jnp.int32, v4.shape, 1)
    same = (row >> 8) == (col >> 6)
    mvb_ref[0, 0] = jnp.where(same, v4, 0.0).astype(BF16)


def _mem_kv(mem, wkT, wv):
    bsz = mem.shape[0]
    m = N_MEM_TOKENS
    return pl.pallas_call(
        _mem_kv_kernel,
        grid=(DEPTH, bsz),
        in_specs=[
            pl.BlockSpec((1, m, D_MODEL), lambda l, b: (b, 0, 0)),
            pl.BlockSpec((1, D_MEM, D_MODEL), lambda l, b: (l, 0, 0)),
            pl.BlockSpec((1, D_MODEL, D_MEM), lambda l, b: (l, 0, 0)),
        ],
        out_specs=[
            pl.BlockSpec((1, 1, D_MEM, N_MEM_HEADS * m), lambda l, b: (l, b, 0, 0)),
            pl.BlockSpec((1, 1, N_MEM_HEADS * m, D_MEM), lambda l, b: (l, b, 0, 0)),
        ],
        out_shape=[
            jax.ShapeDtypeStruct((DEPTH, bsz, D_MEM, N_MEM_HEADS * m), BF16),
            jax.ShapeDtypeStruct((DEPTH, bsz, N_MEM_HEADS * m, D_MEM), BF16),
        ],
        name="mem_kv",
    )(mem, wkT, wv)


def _sgu_kernel(x_ref, win_ref, g_ref, b_ref, ws_ref, bs_ref, mix_ref, qm_ref, *, tm):
    xb = x_ref[...].astype(BF16)
    z = _dot(xb, win_ref[...])
    zu = jax.nn.gelu(z[:, :D_MIX], approximate=True)
    zv = jax.nn.gelu(z[:, D_MIX:2 * D_MIX], approximate=True)
    zv = _layer_norm(zv, g_ref[...], b_ref[...])
    qm_ref[...] = z[:, 2 * D_MIX:].astype(BF16)

    lane = lax.broadcasted_iota(jnp.int32, (CHUNK, LANES), 1)
    low = lane < HEAD_DIM
    wrow = lax.broadcasted_iota(jnp.int32, (CHUNK, 2 * CHUNK), 0)
    wcol = lax.broadcasted_iota(jnp.int32, (CHUNK, 2 * CHUNK), 1) & (CHUNK - 1)
    causal = wcol <= wrow
    for p in range(N_MIX_HEADS // 2):
        w = jnp.where(causal, ws_ref[p], 0.0).astype(BF16)
        bias = bs_ref[:, p * LANES:(p + 1) * LANES]
        for c0 in range(0, tm // CHUNK, 2):
            rhs = []
            for c in (c0, c0 + 1):
                slab = zv[c * CHUNK:(c + 1) * CHUNK, p * LANES:(p + 1) * LANES]
                rhs.append(jnp.concatenate(
                    [jnp.where(low, slab, 0.0), jnp.where(low, 0.0, slab)], axis=0))
            rhs = jnp.concatenate(rhs, axis=1).astype(BF16)
            mixed = _dot(w, rhs)
            for i, c in enumerate((c0, c0 + 1)):
                u = zu[c * CHUNK:(c + 1) * CHUNK, p * LANES:(p + 1) * LANES]
                out = u * (mixed[:, i * LANES:(i + 1) * LANES] + bias)
                mix_ref[c * CHUNK:(c + 1) * CHUNK, p * LANES:(p + 1) * LANES] = out.astype(BF16)


def _sgu(x2d, w_in, ln_g, ln_b, ws_pair, bs_full, tm):
    t = x2d.shape[0]
    n_in = w_in.shape[1]
    return pl.pallas_call(
        functools.partial(_sgu_kernel, tm=tm),
        grid=(t // tm,),
        in_specs=[
            pl.BlockSpec((tm, D_MODEL), lambda i: (i, 0)),
            _const_spec((D_MODEL, n_in)),
            _const_spec((1, D_MIX)),
            _const_spec((1, D_MIX)),
            _const_spec((N_MIX_HEADS // 2, CHUNK, 2 * CHUNK)),
            _const_spec((CHUNK, D_MIX)),
        ],
        out_specs=[pl.BlockSpec((tm, D_MIX), lambda i: (i, 0)),
                   pl.BlockSpec((tm, D_MEM), lambda i: (i, 0))],
        out_shape=[jax.ShapeDtypeStruct((t, D_MIX), BF16),
                   jax.ShapeDtypeStruct((t, D_MEM), BF16)],
        compiler_params=pltpu.CompilerParams(
            dimension_semantics=("arbitrary",), vmem_limit_bytes=VMEM_LIMIT),
        name="sgu_mixer",
    )(x2d, w_in, ln_g, ln_b, ws_pair, bs_full)


def _tail_kernel(mix_ref, qm_ref, x_ref, mkT_ref, mvb_ref, wo_ref, g_ref, b_ref, o_ref):
    m = N_MEM_TOKENS
    s = _dot(qm_ref[...], mkT_ref[0])
    ps = []
    for h in range(N_MEM_HEADS):
        sh = s[:, h * m:(h + 1) * m]
        e = jnp.exp(sh - jnp.max(sh, axis=-1, keepdims=True))
        ps.append((e / jnp.sum(e, axis=-1, keepdims=True)).astype(BF16))
    p = jnp.concatenate(ps, axis=1)
    mo = _dot(p, mvb_ref[0]).astype(BF16)
    y = _dot(mix_ref[...], wo_ref[:D_MIX, :]) + _dot(mo, wo_ref[D_MIX:, :])
    r = DN_ALPHA * x_ref[...] + y
    o_ref[...] = _layer_norm(r, g_ref[...], b_ref[...])


def _tail(mix, qm, x2d, mkT, mvb, w_o, ln_g, ln_b, seq, tm, layer):
    t = x2d.shape[0]
    m = N_MEM_TOKENS
    per_b = seq // tm
    return pl.pallas_call(
        _tail_kernel,
        grid=(t // tm,),
        in_specs=[
            pl.BlockSpec((tm, D_MIX), lambda i: (i, 0)),
            pl.BlockSpec((tm, D_MEM), lambda i: (i, 0)),
            pl.BlockSpec((tm, D_MODEL), lambda i: (i, 0)),
            pl.BlockSpec((None, 1, D_MEM, N_MEM_HEADS * m), lambda i: (layer, i // per_b, 0, 0)),
            pl.BlockSpec((None, 1, N_MEM_HEADS * m, D_MEM), lambda i: (layer, i // per_b, 0, 0)),
            _layer_spec((D_MODEL, D_MODEL), layer),
            _layer_spec((1, D_MODEL), 2 * layer),
            _layer_spec((1, D_MODEL), 2 * layer),
        ],
        out_specs=pl.BlockSpec((tm, D_MODEL), lambda i: (i, 0)),
        out_shape=jax.ShapeDtypeStruct((t, D_MODEL), F32),
        compiler_params=pltpu.CompilerParams(
            dimension_semantics=("arbitrary",), vmem_limit_bytes=VMEM_LIMIT),
        name="mixer_tail",
    )(mix, qm, x2d, mkT, mvb, w_o, ln_g, ln_b)


def _mlp_kernel(x_ref, wu_ref, wd_ref, g_ref, b_ref, o_ref, h_ref, *, n_chunk):
    x = x_ref[...]
    xb = x.astype(BF16)
    for c in range(D_FF // n_chunk):
        h = _dot(xb, wu_ref[:, c * n_chunk:(c + 1) * n_chunk])
        h = jnp.maximum(h, 0.0)
        h_ref[:, c * n_chunk:(c + 1) * n_chunk] = (h * h).astype(BF16)
    y = _dot(h_ref[...], wd_ref[...])
    r = DN_ALPHA * x + y
    o_ref[...] = _layer_norm(r, g_ref[...], b_ref[...])


def _mlp(x2d, w_up, w_down, ln_g, ln_b, tm, layer, n_chunk=1024):
    t = x2d.shape[0]
    return pl.pallas_call(
        functools.partial(_mlp_kernel, n_chunk=n_chunk),
        grid=(t // tm,),
        in_specs=[
            pl.BlockSpec((tm, D_MODEL), lambda i: (i, 0)),
            _layer_spec((D_MODEL, D_FF), layer),
            _layer_spec((D_FF, D_MODEL), layer),
            _layer_spec((1, D_MODEL), 2 * layer + 1),
            _layer_spec((1, D_MODEL), 2 * layer + 1),
        ],
        out_specs=pl.BlockSpec((tm, D_MODEL), lambda i: (i, 0)),
        out_shape=jax.ShapeDtypeStruct((t, D_MODEL), F32),
        scratch_shapes=[pltpu.VMEM((tm, D_FF), BF16)],
        compiler_params=pltpu.CompilerParams(
            dimension_semantics=("arbitrary",), vmem_limit_bytes=VMEM_LIMIT),
        name="relu2_mlp",
    )(x2d, w_up, w_down, ln_g, ln_b)


def _log_sigmoid(x):
    return jnp.minimum(x, 0.0) - jnp.log(1.0 + jnp.exp(-jnp.abs(x)))


def _bproj_kernel(x_ref, wnt_ref, wnn_ref, bf_ref,
                  qT_ref, qb_ref, ka_ref, vT_ref, qm_ref, carry_ref, *, tm):
    @pl.when(pl.program_id(1) == 0)
    def _():
        carry_ref[...] = jnp.zeros_like(carry_ref)

    xb = x_ref[...].astype(BF16)
    nt = _dot_nt(wnt_ref[...], xb)
    qT_ref[0] = nt[:D_MIX].astype(BF16)
    vrow = lax.broadcasted_iota(jnp.int32, (V_ROWS - HEAD_DIM, tm), 0)
    ones_rows = jnp.where(vrow == 0, 1.0, 0.0).astype(BF16)
    for h in range(N_MIX_HEADS):
        vT_ref[0, h, :HEAD_DIM, :] = nt[D_MIX + h * HEAD_DIM:D_MIX + (h + 1) * HEAD_DIM].astype(BF16)
        vT_ref[0, h, HEAD_DIM:, :] = ones_rows

    nn = _dot(xb, wnn_ref[...])
    qm_ref[...] = nn[:, D_MIX:D_MIX + D_MEM].astype(BF16)

    f3 = nn[:, D_MIX + D_MEM:] + bf_ref[...]
    lane = lax.broadcasted_iota(jnp.int32, (tm, LANES), 1)
    valid = (lane & 15) < N_MIX_HEADS
    valid = valid & (lane < 48)
    c = jnp.where(valid, _log_sigmoid(f3) * LOG2E, 0.0)
    rowi = lax.broadcasted_iota(jnp.int32, (tm, LANES), 0)
    d = 1
    while d < tm:
        c = c + jnp.where(rowi >= d, pltpu.roll(c, d, axis=0), 0.0)
        d *= 2
    c = c + carry_ref[...]
    carry_ref[...] = c[tm - 1:tm, :]

    hi, mid, lo = _split3(c)
    kb = jnp.where(lane < 16, -hi,
                   jnp.where(lane < 32, -mid,
                             jnp.where(lane < 48, -lo,
                                       jnp.where(lane < 51, 1.0, 0.0))))
    kb = kb.astype(BF16)
    for p in range(N_MIX_HEADS // 2):
        ka_ref[0, p, :, :LANES] = nn[:, p * LANES:(p + 1) * LANES].astype(BF16)
        ka_ref[0, p, :, LANES:] = kb

    cT = c.T
    brow = lax.broadcasted_iota(jnp.int32, (BIAS_ROWS, tm), 0)
    for h in range(N_MIX_HEADS):
        chi, cmid, clo = _split3(cT[h:h + 1, :])
        sel = (brow == h) | (brow == 16 + h) | (brow == 32 + h)
        tile = jnp.where(brow == 48, chi,
                         jnp.where(brow == 49, cmid,
                                   jnp.where(brow == 50, clo,
                                             jnp.where(sel, 1.0, 0.0))))
        qb_ref[0, h] = tile.astype(BF16)


def _bproj(x3d, wnt, wnn, bf3, tm):
    bsz, seq, _ = x3d.shape
    n_nn = wnn.shape[1]
    return pl.pallas_call(
        functools.partial(_bproj_kernel, tm=tm),
        grid=(bsz, seq // tm),
        in_specs=[
            pl.BlockSpec((None, tm, D_MODEL), lambda b, j: (b, j, 0)),
            _const_spec((2 * D_MIX, D_MODEL)),
            _const_spec((D_MODEL, n_nn)),
            _const_spec((1, LANES)),
        ],
        out_specs=[
            pl.BlockSpec((1, D_MIX, tm), lambda b, j: (b, 0, j)),
            pl.BlockSpec((1, N_MIX_HEADS, BIAS_ROWS, tm), lambda b, j: (b, 0, 0, j)),
            pl.BlockSpec((1, N_MIX_HEADS // 2, tm, 2 * LANES), lambda b, j: (b, 0, j, 0)),
            pl.BlockSpec((1, N_MIX_HEADS, V_ROWS, tm), lambda b, j: (b, 0, 0, j)),
            pl.BlockSpec((tm, D_MEM), lambda b, j: (b * (seq // tm) + j, 0)),
        ],
        out_shape=[
            jax.ShapeDtypeStruct((bsz, D_MIX, seq), BF16),
            jax.ShapeDtypeStruct((bsz, N_MIX_HEADS, BIAS_ROWS, seq), BF16),
            jax.ShapeDtypeStruct((bsz, N_MIX_HEADS // 2, seq, 2 * LANES), BF16),
            jax.ShapeDtypeStruct((bsz, N_MIX_HEADS, V_ROWS, seq), BF16),
            jax.ShapeDtypeStruct((bsz * seq, D_MEM), BF16),
        ],
        scratch_shapes=[pltpu.VMEM((1, LANES), F32)],
        compiler_params=pltpu.CompilerParams(
            dimension_semantics=("arbitrary", "arbitrary"), vmem_limit_bytes=VMEM_LIMIT),
        name="fox_proj",
    )(x3d, wnt, wnn, bf3)


def _fox_kernel(qT_ref, qb_ref, ka_ref, vT_ref, o_ref, m_ref, acc_ref, s_ref, *, tq, tk):
    qi = pl.program_id(2)
    n_sub = tq // tk
    assert n_sub % 2 == 0
    n_groups = 2 * n_sub

    srow = lax.broadcasted_iota(jnp.int32, (2 * HEAD_DIM, tk), 0)
    zpad = jnp.zeros((LANES - BIAS_ROWS, tk), BF16)
    qa = []
    for sub in range(n_sub):
        cols = slice(sub * tk, (sub + 1) * tk)
        sl = qT_ref[0, :, cols]
        for hh in range(2):
            keep = (srow < HEAD_DIM) if hh == 0 else (srow >= HEAD_DIM)
            qa.append(jnp.concatenate(
                [jnp.where(keep, sl, jnp.zeros_like(sl)), qb_ref[0, hh, :, cols], zpad], axis=0))

    m_ref[...] = jnp.full(m_ref.shape, NEG_BIG, F32)
    acc_ref[...] = jnp.zeros(acc_ref.shape, F32)

    tri = (lax.broadcasted_iota(jnp.int32, (tk, tk), 0)
           <= lax.broadcasted_iota(jnp.int32, (tk, tk), 1))

    def scores(i, slot, g):
        ks = pl.multiple_of(i * tk, tk)
        s_ref[slot, g] = _dot(ka_ref[0, 0, pl.ds(ks, tk), :], qa[g])

    def softmax_pv(i, slot, g, masked):
        ks = pl.multiple_of(i * tk, tk)
        s = s_ref[slot, g]
        if masked:
            s = jnp.where(tri, s, NEG_BIG)
        m_prev = m_ref[g]
        m_new = jnp.maximum(m_prev, jnp.max(s, axis=0, keepdims=True))
        m_ref[g] = m_new
        p = jnp.exp2((s - m_new).astype(BF16))
        pv = _dot(vT_ref[0, g % 2, :, pl.ds(ks, tk)], p)
        acc_ref[g] = jnp.exp2(m_prev - m_new) * acc_ref[g] + pv

    def block_pair(i_next, slot_next, i_cur, slot_cur, g0, diagonal):
        for g in range(g0, n_groups):
            if i_next is not None and g >= g0 + (2 if diagonal else 0):
                scores(i_next, slot_next, g)
            softmax_pv(i_cur, slot_cur, g, diagonal and g < g0 + 2)

    n_off = qi * n_sub
    for g in range(n_groups):
        scores(0, 0, g)

    def body(j, carry):
        for u in range(n_sub):
            block_pair(n_sub * j + u + 1, (u + 1) % 2, n_sub * j + u, u % 2, 0, False)
        return carry

    lax.fori_loop(0, qi, body, 0)
    for dj in range(n_sub):
        nxt = n_off + dj + 1 if dj + 1 < n_sub else None
        block_pair(nxt, (dj + 1) % 2, n_off + dj, dj % 2, 2 * dj, True)

    def normalized(g):
        acc = acc_ref[g]
        return acc[:HEAD_DIM] * (1.0 / acc[HEAD_DIM:HEAD_DIM + 1])

    o = jnp.concatenate(
        [jnp.concatenate([normalized(2 * sub + hh) for sub in range(n_sub)], axis=1)
         for hh in range(2)], axis=0)
    o_ref[0] = o.T.astype(BF16)


def _fox(qT, qb, ka, vT, tq, tk):
    bsz, _, seq = qT.shape
    n_pair = N_MIX_HEADS // 2
    return pl.pallas_call(
        functools.partial(_fox_kernel, tq=tq, tk=tk),
        grid=(bsz, n_pair, seq // tq),
        in_specs=[
            pl.BlockSpec((1, 2 * HEAD_DIM, tq), lambda b, p, i: (b, p, i)),
            pl.BlockSpec((1, 2, BIAS_ROWS, tq), lambda b, p, i: (b, p, 0, i)),
            pl.BlockSpec((1, 1, seq, 2 * LANES), lambda b, p, i: (b, p, 0, 0)),
            pl.BlockSpec((1, 2, V_ROWS, seq), lambda b, p, i: (b, p, 0, 0)),
        ],
        out_specs=pl.BlockSpec((1, tq, LANES), lambda b, p, i: (b, i, p)),
        out_shape=jax.ShapeDtypeStruct((bsz, seq, D_MIX), BF16),
        scratch_shapes=[pltpu.VMEM((2 * tq // tk, 1, tk), F32),
                        pltpu.VMEM((2 * tq // tk, V_ROWS, tk), F32),
                        pltpu.VMEM((2, 2 * tq // tk, tk, tk), F32)],
        compiler_params=pltpu.CompilerParams(
            dimension_semantics=("arbitrary", "arbitrary", "arbitrary"),
            vmem_limit_bytes=VMEM_LIMIT),
        name="fox_attention",
    )(qT, qb, ka, vT)


def kernel(x, mem, a_w_in, a_sgu_ln_g, a_sgu_ln_b, a_w_s, a_b_s, kv_w, kv_b_f, b_w_q,
           mem_w_kv, w_o, ln_g, ln_b, w_up, w_down):
    bsz, seq, d = x.shape
    t = bsz * seq
    scale = 1.0 / math.sqrt(HEAD_DIM)

    wkT = (jnp.swapaxes(mem_w_kv[:, :, :D_MEM], 1, 2) * scale).astype(BF16)
    wv = mem_w_kv[:, :, D_MEM:].astype(BF16)
    w_in = a_w_in[0].astype(BF16)
    ws_pair = (a_w_s[0].reshape(N_MIX_HEADS // 2, 2, CHUNK, CHUNK)
               .transpose(0, 2, 1, 3).reshape(N_MIX_HEADS // 2, CHUNK, 2 * CHUNK))
    bs_full = jnp.repeat(a_b_s[0].T, HEAD_DIM, axis=1)
    wq = b_w_q[0]
    wnt = jnp.concatenate([wq[:, :D_MIX].T * (scale * LOG2E), kv_w[:, D_MIX:2 * D_MIX].T], axis=0).astype(BF16)
    wf = kv_w[:, 2 * D_MIX:]
    zf = jnp.zeros((d, 16 - N_MIX_HEADS), F32)
    wf3 = jnp.concatenate([wf, zf, wf, zf, wf, zf, jnp.zeros((d, LANES - 48), F32)], axis=1)
    wnn = jnp.concatenate([kv_w[:, :D_MIX], wq[:, D_MIX:], wf3], axis=1).astype(BF16)
    zb = jnp.zeros((16 - N_MIX_HEADS,), F32)
    bf3 = jnp.concatenate([kv_b_f, zb, kv_b_f, zb, kv_b_f, zb, jnp.zeros((LANES - 48,), F32)])[None, :]
    w_o_b = w_o.astype(BF16)
    w_up_b = w_up.astype(BF16)
    w_down_b = w_down.astype(BF16)
    row = lambda v: v[None, :]
    ln_g4 = ln_g.reshape(2 * DEPTH, 1, d)
    ln_b4 = ln_b.reshape(2 * DEPTH, 1, d)

    mkT, mvb = _mem_kv(mem, wkT, wv)
    x2d = x.reshape(t, d)

    tm_a = min(1024, seq)
    mix, qm = _sgu(x2d, w_in, row(a_sgu_ln_g[0]), row(a_sgu_ln_b[0]), ws_pair, bs_full, tm_a)
    x2d = _tail(mix, qm, x2d, mkT, mvb, w_o_b, ln_g4, ln_b4, seq, tm_a, 0)
    tm_m = min(1024, seq)
    x2d = _mlp(x2d, w_up_b, w_down_b, ln_g4, ln_b4, tm_m, 0)

    tm_b = min(1024, seq)
    qT, qb, ka, vT, qm = _bproj(x2d.reshape(bsz, seq, d), wnt, wnn, bf3, tm_b)
    tq = min(2048, seq)
    att = _fox(qT, qb, ka, vT, tq, min(256, seq)).reshape(t, D_MIX)
    x2d = _tail(att, qm, x2d, mkT, mvb, w_o_b, ln_g4, ln_b4, seq, tm_a, 1)
    x2d = _mlp(x2d, w_up_b, w_down_b, ln_g4, ln_b4, tm_m, 1)
    return x2d.reshape(bsz, seq, d)
```

```python
import functools
import math

import jax
import jax.numpy as jnp
from jax import lax
from jax.experimental import pallas as pl
from jax.experimental.pallas import tpu as pltpu

D_MODEL = 1024
HEAD_DIM = 64
N_MIX_HEADS = 12
N_MEM_HEADS = 4
D_MIX = N_MIX_HEADS * HEAD_DIM
D_MEM = N_MEM_HEADS * HEAD_DIM
CHUNK = 128
N_MEM_TOKENS = 256
D_FF = 4 * D_MODEL
DEPTH = 2
DN_ALPHA = (2 * DEPTH) ** 0.25
LN_EPS = 1e-5

LANES = 128
V_ROWS = 80
BIAS_ROWS = 64
NEG_BIG = -1e30
LOG2E = math.log2(math.e)

F32 = jnp.float32
BF16 = jnp.bfloat16

VMEM_LIMIT = 56 * 1024 * 1024


def _const_spec(shape):
    nd = len(shape)
    return pl.BlockSpec(shape, lambda *_: (0,) * nd, pipeline_mode=pl.Buffered(1))


def _layer_spec(shape, layer):
    nd = len(shape)
    return pl.BlockSpec((None,) + tuple(shape), lambda *_: (layer,) + (0,) * nd,
                        pipeline_mode=pl.Buffered(1))


def _dot(a, b):
    return jnp.dot(a, b, preferred_element_type=F32)


def _dot_nt(a, b):
    return lax.dot_general(a, b, (((1,), (1,)), ((), ())), preferred_element_type=F32)


def _layer_norm(r, g, b):
    mu = jnp.mean(r, axis=-1, keepdims=True)
    d = r - mu
    var = jnp.mean(d * d, axis=-1, keepdims=True)
    return d * lax.rsqrt(var + LN_EPS) * g + b


def _split3(c):
    hi = c.astype(BF16).astype(F32)
    r1 = c - hi
    mid = r1.astype(BF16).astype(F32)
    lo = (r1 - mid).astype(BF16).astype(F32)
    return hi, mid, lo


def _mem_kv_kernel(mem_ref, wkT_ref, wv_ref, mkT_ref, mvb_ref):
    memb = mem_ref[0].astype(BF16)
    kT = _dot_nt(wkT_ref[0], memb)
    kT4 = jnp.concatenate([kT] * N_MEM_HEADS, axis=1)
    row = lax.broadcasted_iota(jnp.int32, kT4.shape, 0)
    col = lax.broadcasted_iota(jnp.int32, kT4.shape, 1)
    same = (row >> 6) == (col >> 8)
    mkT_ref[0, 0] = jnp.where(same, kT4, 0.0).astype(BF16)
    v = _dot(memb, wv_ref[0])
    v4 = jnp.concatenate([v] * N_MEM_HEADS, axis=0)
    row = lax.broadcasted_iota(jnp.int32, v4.shape, 0)
    col = lax.broadcasted_iota(jnp.int32, v4.shape, 1)
    same = (row >> 8) == (col >> 6)
    mvb_ref[0, 0] = jnp.where(same, v4, 0.0).astype(BF16)


def _mem_kv(mem, wkT, wv):
    bsz = mem.shape[0]
    m = N_MEM_TOKENS
    return pl.pallas_call(
        _mem_kv_kernel,
        grid=(DEPTH, bsz),
        in_specs=[
            pl.BlockSpec((1, m, D_MODEL), lambda l, b: (b, 0, 0)),
            pl.BlockSpec((1, D_MEM, D_MODEL), lambda l, b: (l, 0, 0)),
            pl.BlockSpec((1, D_MODEL, D_MEM), lambda l, b: (l, 0, 0)),
        ],
        out_specs=[
            pl.BlockSpec((1, 1, D_MEM, N_MEM_HEADS * m), lambda l, b: (l, b, 0, 0)),
            pl.BlockSpec((1, 1, N_MEM_HEADS * m, D_MEM), lambda l, b: (l, b, 0, 0)),
        ],
        out_shape=[
            jax.ShapeDtypeStruct((DEPTH, bsz, D_MEM, N_MEM_HEADS * m), BF16),
            jax.ShapeDtypeStruct((DEPTH, bsz, N_MEM_HEADS * m, D_MEM), BF16),
        ],
        name="mem_kv",
    )(mem, wkT, wv)


def _sgu_kernel(x_ref, win_ref, g_ref, b_ref, ws_ref, bs_ref, mix_ref, qm_ref, *, tm):
    lane = lax.broadcasted_iota(jnp.int32, (CHUNK, LANES), 1)
    low = lane < HEAD_DIM
    wrow = lax.broadcasted_iota(jnp.int32, (CHUNK, 2 * CHUNK), 0)
    wcol = lax.broadcasted_iota(jnp.int32, (CHUNK, 2 * CHUNK), 1) & (CHUNK - 1)
    causal = wcol <= wrow
    xb = x_ref[...].astype(BF16)
    z = _dot(xb, win_ref[...])
    zu = jax.nn.gelu(z[:, :D_MIX], approximate=True)
    zv = jax.nn.gelu(z[:, D_MIX:2 * D_MIX], approximate=True)
    zv = _layer_norm(zv, g_ref[...], b_ref[...])
    qm_ref[...] = z[:, 2 * D_MIX:].astype(BF16)
    for p in range(N_MIX_HEADS // 2):
        w = jnp.where(causal, ws_ref[p], 0.0).astype(BF16)
        bias = bs_ref[:, p * LANES:(p + 1) * LANES]
        for c0 in range(0, tm // CHUNK, 2):
            rhs = []
            for c in (c0, c0 + 1):
                slab = zv[c * CHUNK:(c + 1) * CHUNK, p * LANES:(p + 1) * LANES]
                rhs.append(jnp.concatenate(
                    [jnp.where(low, slab, 0.0), jnp.where(low, 0.0, slab)], axis=0))
            rhs = jnp.concatenate(rhs, axis=1).astype(BF16)
            mixed = _dot(w, rhs)
            for i, c in enumerate((c0, c0 + 1)):
                u = zu[c * CHUNK:(c + 1) * CHUNK, p * LANES:(p + 1) * LANES]
                out = u * (mixed[:, i * LANES:(i + 1) * LANES] + bias)
                mix_ref[c * CHUNK:(c + 1) * CHUNK, p * LANES:(p + 1) * LANES] = out.astype(BF16)


def _sgu(x2d, w_in, ln_g, ln_b, ws_pair, bs_full, tm):
    t = x2d.shape[0]
    n_in = w_in.shape[1]
    return pl.pallas_call(
        functools.partial(_sgu_kernel, tm=tm),
        grid=(t // tm,),
        in_specs=[
            pl.BlockSpec((tm, D_MODEL), lambda i: (i, 0)),
            _const_spec((D_MODEL, n_in)),
            _const_spec((1, D_MIX)),
            _const_spec((1, D_MIX)),
            _const_spec((N_MIX_HEADS // 2, CHUNK, 2 * CHUNK)),
            _const_spec((CHUNK, D_MIX)),
        ],
        out_specs=[pl.BlockSpec((tm, D_MIX), lambda i: (i, 0)),
                   pl.BlockSpec((tm, D_MEM), lambda i: (i, 0))],
        out_shape=[jax.ShapeDtypeStruct((t, D_MIX), BF16),
                   jax.ShapeDtypeStruct((t, D_MEM), BF16)],
        compiler_params=pltpu.CompilerParams(
            dimension_semantics=("arbitrary",), vmem_limit_bytes=VMEM_LIMIT),
        name="sgu_mixer",
    )(x2d, w_in, ln_g, ln_b, ws_pair, bs_full)


def _tail_kernel(mix_ref, qm_ref, x_ref, mkT_ref, mvb_ref, wo_ref, g_ref, b_ref, o_ref, *, rows):
    m = N_MEM_TOKENS
    for r0 in range(0, x_ref.shape[0], rows):
        rs = slice(r0, r0 + rows)
        s = _dot(qm_ref[rs, :], mkT_ref[0])
        ps = []
        for h in range(N_MEM_HEADS):
            sh = s[:, h * m:(h + 1) * m]
            e = jnp.exp(sh - jnp.max(sh, axis=-1, keepdims=True))
            ps.append((e / jnp.sum(e, axis=-1, keepdims=True)).astype(BF16))
        p = jnp.concatenate(ps, axis=1)
        mo = _dot(p, mvb_ref[0]).astype(BF16)
        y = _dot(mix_ref[rs, :], wo_ref[:D_MIX, :]) + _dot(mo, wo_ref[D_MIX:, :])
        r = DN_ALPHA * x_ref[rs, :] + y
        o_ref[rs, :] = _layer_norm(r, g_ref[...], b_ref[...])


def _tail(mix, qm, x2d, mkT, mvb, w_o, ln_g, ln_b, seq, tm, layer):
    t = x2d.shape[0]
    m = N_MEM_TOKENS
    per_b = seq // tm
    return pl.pallas_call(
        functools.partial(_tail_kernel, rows=tm),
        grid=(t // tm,),
        in_specs=[
            pl.BlockSpec((tm, D_MIX), lambda i: (i, 0)),
            pl.BlockSpec((tm, D_MEM), lambda i: (i, 0)),
            pl.BlockSpec((tm, D_MODEL), lambda i: (i, 0)),
            pl.BlockSpec((None, 1, D_MEM, N_MEM_HEADS * m), lambda i: (layer, i // per_b, 0, 0)),
            pl.BlockSpec((None, 1, N_MEM_HEADS * m, D_MEM), lambda i: (layer, i // per_b, 0, 0)),
            _layer_spec((D_MODEL, D_MODEL), layer),
            _layer_spec((1, D_MODEL), 2 * layer),
            _layer_spec((1, D_MODEL), 2 * layer),
        ],
        out_specs=pl.BlockSpec((tm, D_MODEL), lambda i: (i, 0)),
        out_shape=jax.ShapeDtypeStruct((t, D_MODEL), F32),
        compiler_params=pltpu.CompilerParams(
            dimension_semantics=("arbitrary",), vmem_limit_bytes=VMEM_LIMIT),
        name="mixer_tail",
    )(mix, qm, x2d, mkT, mvb, w_o, ln_g, ln_b)


def _mlp_kernel(x_ref, wu_ref, wd_ref, g_ref, b_ref, o_ref, h_ref, *, n_chunk, rows):
    for r0 in range(0, x_ref.shape[0], rows):
        rs = slice(r0, r0 + rows)
        x = x_ref[rs, :]
        xb = x.astype(BF16)
        for c in range(D_FF // n_chunk):
            h = _dot(xb, wu_ref[:, c * n_chunk:(c + 1) * n_chunk])
            h = jnp.maximum(h, 0.0)
            h_ref[rs, c * n_chunk:(c + 1) * n_chunk] = (h * h).astype(BF16)
        y = _dot(h_ref[rs, :], wd_ref[...])
        r = DN_ALPHA * x + y
        o_ref[rs, :] = _layer_norm(r, g_ref[...], b_ref[...])


def _mlp(x2d, w_up, w_down, ln_g, ln_b, tm, layer, n_chunk=1024, rows=256):
    t = x2d.shape[0]
    rows = min(rows, tm)
    return pl.pallas_call(
        functools.partial(_mlp_kernel, n_chunk=n_chunk, rows=rows),
        grid=(t // tm,),
        in_specs=[
            pl.BlockSpec((tm, D_MODEL), lambda i: (i, 0)),
            _layer_spec((D_MODEL, D_FF), layer),
            _layer_spec((D_FF, D_MODEL), layer),
            _layer_spec((1, D_MODEL), 2 * layer + 1),
            _layer_spec((1, D_MODEL), 2 * layer + 1),
        ],
        out_specs=pl.BlockSpec((tm, D_MODEL), lambda i: (i, 0)),
        out_shape=jax.ShapeDtypeStruct((t, D_MODEL), F32),
        scratch_shapes=[pltpu.VMEM((tm, D_FF), BF16)],
        compiler_params=pltpu.CompilerParams(
            dimension_semantics=("arbitrary",), vmem_limit_bytes=VMEM_LIMIT),
        name="relu2_mlp",
    )(x2d, w_up, w_down, ln_g, ln_b)


def _log_sigmoid(x):
    return jnp.minimum(x, 0.0) - jnp.log(1.0 + jnp.exp(-jnp.abs(x)))


def _bproj_kernel(x_ref, wnt_ref, wnn_ref, bf_ref,
                  qT_ref, qb_ref, ka_ref, vT_ref, qm_ref, carry_ref, *, tm):
    @pl.when(pl.program_id(1) == 0)
    def _():
        carry_ref[...] = jnp.zeros_like(carry_ref)

    xb = x_ref[...].astype(BF16)
    nt = _dot_nt(wnt_ref[...], xb)
    qT_ref[0] = nt[:D_MIX].astype(BF16)
    vrow = lax.broadcasted_iota(jnp.int32, (V_ROWS - HEAD_DIM, tm), 0)
    ones_rows = jnp.where(vrow == 0, 1.0, 0.0).astype(BF16)
    for h in range(N_MIX_HEADS):
        vT_ref[0, h, :HEAD_DIM, :] = nt[D_MIX + h * HEAD_DIM:D_MIX + (h + 1) * HEAD_DIM].astype(BF16)
        vT_ref[0, h, HEAD_DIM:, :] = ones_rows

    nn = _dot(xb, wnn_ref[...])
    qm_ref[...] = nn[:, D_MIX:D_MIX + D_MEM].astype(BF16)

    f3 = nn[:, D_MIX + D_MEM:] + bf_ref[...]
    lane = lax.broadcasted_iota(jnp.int32, (tm, LANES), 1)
    valid = (lane & 15) < N_MIX_HEADS
    valid = valid & (lane < 48)
    c = jnp.where(valid, _log_sigmoid(f3) * LOG2E, 0.0)
    rowi = lax.broadcasted_iota(jnp.int32, (tm, LANES), 0)
    d = 1
    while d < tm:
        c = c + jnp.where(rowi >= d, pltpu.roll(c, d, axis=0), 0.0)
        d *= 2
    c = c + carry_ref[...]
    carry_ref[...] = c[tm - 1:tm, :]

    hi, mid, lo = _split3(c)
    kb = jnp.where(lane < 16, -hi,
                   jnp.where(lane < 32, -mid,
                             jnp.where(lane < 48, -lo,
                                       jnp.where(lane < 51, 1.0, 0.0))))
    kb = kb.astype(BF16)
    for p in range(N_MIX_HEADS // 2):
        ka_ref[0, p, :, :LANES] = nn[:, p * LANES:(p + 1) * LANES].astype(BF16)
        ka_ref[0, p, :, LANES:] = kb

    cT = c.T
    brow = lax.broadcasted_iota(jnp.int32, (BIAS_ROWS, tm), 0)
    for h in range(N_MIX_HEADS):
        chi, cmid, clo = _split3(cT[h:h + 1, :])
        sel = (brow == h) | (brow == 16 + h) | (brow == 32 + h)
        tile = jnp.where(brow == 48, chi,
                         jnp.where(brow == 49, cmid,
                                   jnp.where(brow == 50, clo,
                                             jnp.where(sel, 1.0, 0.0))))
        qb_ref[0, h] = tile.astype(BF16)


def _bproj(x3d, wnt, wnn, bf3, tm):
    bsz, seq, _ = x3d.shape
    n_nn = wnn.shape[1]
    return pl.pallas_call(
        functools.partial(_bproj_kernel, tm=tm),
        grid=(bsz, seq // tm),
        in_specs=[
            pl.BlockSpec((None, tm, D_MODEL), lambda b, j: (b, j, 0)),
            _const_spec((2 * D_MIX, D_MODEL)),
            _const_spec((D_MODEL, n_nn)),
            _const_spec((1, LANES)),
        ],
        out_specs=[
            pl.BlockSpec((1, D_MIX, tm), lambda b, j: (b, 0, j)),
            pl.BlockSpec((1, N_MIX_HEADS, BIAS_ROWS, tm), lambda b, j: (b, 0, 0, j)),
            pl.BlockSpec((1, N_MIX_HEADS // 2, tm, 2 * LANES), lambda b, j: (b, 0, j, 0)),
            pl.BlockSpec((1, N_MIX_HEADS, V_ROWS, tm), lambda b, j: (b, 0, 0, j)),
            pl.BlockSpec((tm, D_MEM), lambda b, j: (b * (seq // tm) + j, 0)),
        ],
        out_shape=[
            jax.ShapeDtypeStruct((bsz, D_MIX, seq), BF16),
            jax.ShapeDtypeStruct((bsz, N_MIX_HEADS, BIAS_ROWS, seq), BF16),
            jax.ShapeDtypeStruct((bsz, N_MIX_HEADS // 2, seq, 2 * LANES), BF16),
            jax.ShapeDtypeStruct((bsz, N_MIX_HEADS, V_ROWS, seq), BF16),
            jax.ShapeDtypeStruct((bsz * seq, D_MEM), BF16),
        ],
        scratch_shapes=[pltpu.VMEM((1, LANES), F32)],
        compiler_params=pltpu.CompilerParams(
            dimension_semantics=("arbitrary", "arbitrary"), vmem_limit_bytes=VMEM_LIMIT),
        name="fox_proj",
    )(x3d, wnt, wnn, bf3)


def _fox_kernel(qT_ref, qb_ref, ka_ref, vT_ref, o_ref, m_ref, acc_ref, s_ref, *, tq, tk, lookahead):
    qi = pl.program_id(2)
    n_sub = tq // tk
    n_groups = 2 * n_sub

    srow = lax.broadcasted_iota(jnp.int32, (2 * HEAD_DIM, tk), 0)
    zpad = jnp.zeros((LANES - BIAS_ROWS, tk), BF16)
    qa = []
    for sub in range(n_sub):
        cols = slice(sub * tk, (sub + 1) * tk)
        sl = qT_ref[0, :, cols]
        for hh in range(2):
            keep = (srow < HEAD_DIM) if hh == 0 else (srow >= HEAD_DIM)
            qa.append(jnp.concatenate(
                [jnp.where(keep, sl, jnp.zeros_like(sl)), qb_ref[0, hh, :, cols], zpad], axis=0))

    m_ref[...] = jnp.full(m_ref.shape, NEG_BIG, F32)
    acc_ref[...] = jnp.zeros(acc_ref.shape, F32)

    tri = (lax.broadcasted_iota(jnp.int32, (tk, tk), 0)
           <= lax.broadcasted_iota(jnp.int32, (tk, tk), 1))
    assert lookahead < n_groups

    def scores(i, g, buf):
        ks = pl.multiple_of(i * tk, tk)
        s_ref[buf] = _dot(ka_ref[0, 0, pl.ds(ks, tk), :], qa[g])

    def softmax_pv(i, g, buf, masked):
        ks = pl.multiple_of(i * tk, tk)
        s = s_ref[buf]
        if masked:
            s = jnp.where(tri, s, NEG_BIG)
        m_prev = m_ref[g]
        m_new = jnp.maximum(m_prev, jnp.max(s, axis=0, keepdims=True))
        m_ref[g] = m_new
        p = jnp.exp2((s - m_new).astype(BF16))
        pv = _dot(vT_ref[0, g % 2, :, pl.ds(ks, tk)], p)
        acc_ref[g] = jnp.exp2(m_prev - m_new) * acc_ref[g] + pv

    n_off = qi * n_sub
    for g in range(lookahead):
        scores(0, g, g)

    def body(j, carry):
        for u in range(n_sub):
            i = n_sub * j + u
            for g in range(n_groups):
                ahead = g + lookahead
                if ahead < n_groups:
                    scores(i, ahead, ahead)
                else:
                    scores(i + 1, ahead - n_groups, ahead - n_groups)
                softmax_pv(i, g, g, False)
        return carry

    lax.fori_loop(0, qi, body, 0)
    diag_tiles = [(dj, g) for dj in range(n_sub) for g in range(2 * dj, n_groups)]
    for t, (dj, g) in enumerate(diag_tiles):
        if t + lookahead < len(diag_tiles):
            dj2, g2 = diag_tiles[t + lookahead]
            scores(n_off + dj2, g2, (t + lookahead) % n_groups)
        softmax_pv(n_off + dj, g, t % n_groups, g < 2 * dj + 2)

    def normalized(g):
        acc = acc_ref[g]
        return acc[:HEAD_DIM] * (1.0 / acc[HEAD_DIM:HEAD_DIM + 1])

    o = jnp.concatenate(
        [jnp.concatenate([normalized(2 * sub + hh) for sub in range(n_sub)], axis=1)
         for hh in range(2)], axis=0)
    o_ref[0] = o.T.astype(BF16)


def _fox(qT, qb, ka, vT, tq, tk):
    bsz, _, seq = qT.shape
    n_pair = N_MIX_HEADS // 2
    return pl.pallas_call(
        functools.partial(_fox_kernel, tq=tq, tk=tk, lookahead=min(6, 2 * tq // tk - 1)),
        grid=(bsz, n_pair, seq // tq),
        in_specs=[
            pl.BlockSpec((1, 2 * HEAD_DIM, tq), lambda b, p, i: (b, p, i)),
            pl.BlockSpec((1, 2, BIAS_ROWS, tq), lambda b, p, i: (b, p, 0, i)),
            pl.BlockSpec((1, 1, seq, 2 * LANES), lambda b, p, i: (b, p, 0, 0)),
            pl.BlockSpec((1, 2, V_ROWS, seq), lambda b, p, i: (b, p, 0, 0)),
        ],
        out_specs=pl.BlockSpec((1, tq, LANES), lambda b, p, i: (b, i, p)),
        out_shape=jax.ShapeDtypeStruct((bsz, seq, D_MIX), BF16),
        scratch_shapes=[pltpu.VMEM((2 * tq // tk, 1, tk), F32),
                        pltpu.VMEM((2 * tq // tk, V_ROWS, tk), F32),
                        pltpu.VMEM((2 * tq // tk, tk, tk), F32)],
        compiler_params=pltpu.CompilerParams(
            dimension_semantics=("arbitrary", "arbitrary", "arbitrary"),
            vmem_limit_bytes=VMEM_LIMIT),
        name="fox_attention",
    )(qT, qb, ka, vT)


def kernel(x, mem, a_w_in, a_sgu_ln_g, a_sgu_ln_b, a_w_s, a_b_s, kv_w, kv_b_f, b_w_q,
           mem_w_kv, w_o, ln_g, ln_b, w_up, w_down):
    bsz, seq, d = x.shape
    t = bsz * seq
    scale = 1.0 / math.sqrt(HEAD_DIM)

    wkT = (jnp.swapaxes(mem_w_kv[:, :, :D_MEM], 1, 2) * scale).astype(BF16)
    wv = mem_w_kv[:, :, D_MEM:].astype(BF16)
    w_in = a_w_in[0].astype(BF16)
    ws_pair = (a_w_s[0].reshape(N_MIX_HEADS // 2, 2, CHUNK, CHUNK)
               .transpose(0, 2, 1, 3).reshape(N_MIX_HEADS // 2, CHUNK, 2 * CHUNK))
    bs_full = jnp.repeat(a_b_s[0].T, HEAD_DIM, axis=1)
    wq = b_w_q[0]
    wnt = jnp.concatenate([wq[:, :D_MIX].T * (scale * LOG2E), kv_w[:, D_MIX:2 * D_MIX].T], axis=0).astype(BF16)
    wf = kv_w[:, 2 * D_MIX:]
    zf = jnp.zeros((d, 16 - N_MIX_HEADS), F32)
    wf3 = jnp.concatenate([wf, zf, wf, zf, wf, zf, jnp.zeros((d, LANES - 48), F32)], axis=1)
    wnn = jnp.concatenate([kv_w[:, :D_MIX], wq[:, D_MIX:], wf3], axis=1).astype(BF16)
    zb = jnp.zeros((16 - N_MIX_HEADS,), F32)
    bf3 = jnp.concatenate([kv_b_f, zb, kv_b_f, zb, kv_b_f, zb, jnp.zeros((LANES - 48,), F32)])[None, :]
    w_o_b = w_o.astype(BF16)
    w_up_b = w_up.astype(BF16)
    w_down_b = w_down.astype(BF16)
    row = lambda v: v[None, :]
    ln_g4 = ln_g.reshape(2 * DEPTH, 1, d)
    ln_b4 = ln_b.reshape(2 * DEPTH, 1, d)

    mkT, mvb = _mem_kv(mem, wkT, wv)
    x2d = x.reshape(t, d)

    tm_a = min(1024, seq)
    mix, qm = _sgu(x2d, w_in, row(a_sgu_ln_g[0]), row(a_sgu_ln_b[0]), ws_pair, bs_full, tm_a)
    x2d = _tail(mix, qm, x2d, mkT, mvb, w_o_b, ln_g4, ln_b4, seq, tm_a, 0)
    tm_m = min(1024, seq)
    x2d = _mlp(x2d, w_up_b, w_down_b, ln_g4, ln_b4, tm_m, 0)

    tm_b = min(1024, seq)
    qT, qb, ka, vT, qm = _bproj(x2d.reshape(bsz, seq, d), wnt, wnn, bf3, tm_b)
    tq = min(2048, seq)
    att = _fox(qT, qb, ka, vT, tq, min(256, seq)).reshape(t, D_MIX)
    x2d = _tail(att, qm, x2d, mkT, mvb, w_o_b, ln_g4, ln_b4, seq, tm_a, 1)
    x2d = _mlp(x2d, w_up_b, w_down_b, ln_g4, ln_b4, tm_m, 1)
    return x2d.reshape(bsz, seq, d)
```

```python
import functools
import math

import jax
import jax.numpy as jnp
from jax import lax
from jax.experimental import pallas as pl
from jax.experimental.pallas import tpu as pltpu

D_MODEL = 1024
HEAD_DIM = 64
N_MIX_HEADS = 12
N_MEM_HEADS = 4
D_MIX = N_MIX_HEADS * HEAD_DIM
D_MEM = N_MEM_HEADS * HEAD_DIM
CHUNK = 128
N_MEM_TOKENS = 256
D_FF = 4 * D_MODEL
DEPTH = 2
DN_ALPHA = (2 * DEPTH) ** 0.25
LN_EPS = 1e-5

LANES = 128
V_ROWS = 80
BIAS_ROWS = 64
NEG_BIG = -1e30
LOG2E = math.log2(math.e)

F32 = jnp.float32
BF16 = jnp.bfloat16

VMEM_LIMIT = 56 * 1024 * 1024


def _const_spec(shape):
    nd = len(shape)
    return pl.BlockSpec(shape, lambda *_: (0,) * nd, pipeline_mode=pl.Buffered(1))


def _layer_spec(shape, layer):
    nd = len(shape)
    return pl.BlockSpec((None,) + tuple(shape), lambda *_: (layer,) + (0,) * nd,
                        pipeline_mode=pl.Buffered(1))


def _dot(a, b):
    return jnp.dot(a, b, preferred_element_type=F32)


def _dot_nt(a, b):
    return lax.dot_general(a, b, (((1,), (1,)), ((), ())), preferred_element_type=F32)


def _layer_norm(r, g, b):
    mu = jnp.mean(r, axis=-1, keepdims=True)
    d = r - mu
    var = jnp.mean(d * d, axis=-1, keepdims=True)
    return d * lax.rsqrt(var + LN_EPS) * g + b


def _split3(c):
    hi = c.astype(BF16).astype(F32)
    r1 = c - hi
    mid = r1.astype(BF16).astype(F32)
    lo = (r1 - mid).astype(BF16).astype(F32)
    return hi, mid, lo


def _mem_kv_kernel(mem_ref, wkT_ref, wv_ref, mkT_ref, mvb_ref):
    memb = mem_ref[0].astype(BF16)
    kT = _dot_nt(wkT_ref[0], memb)
    kT4 = jnp.concatenate([kT] * N_MEM_HEADS, axis=1)
    row = lax.broadcasted_iota(jnp.int32, kT4.shape, 0)
    col = lax.broadcasted_iota(jnp.int32, kT4.shape, 1)
    same = (row >> 6) == (col >> 8)
    mkT_ref[0, 0] = jnp.where(same, kT4, 0.0).astype(BF16)
    v = _dot(memb, wv_ref[0])
    v4 = jnp.concatenate([v] * N_MEM_HEADS, axis=0)
    row = lax.broadcasted_iota(jnp.int32, v4.shape, 0)
    col = lax.broadcasted_iota(jnp.int32, v4.shape, 1)
    same = (row >> 8) == (col >> 6)
    mvb_ref[0, 0] = jnp.where(same, v4, 0.0).astype(BF16)


def _mem_kv(mem, wkT, wv):
    bsz = mem.shape[0]
    m = N_MEM_TOKENS
    return pl.pallas_call(
        _mem_kv_kernel,
        grid=(DEPTH, bsz),
        in_specs=[
            pl.BlockSpec((1, m, D_MODEL), lambda l, b: (b, 0, 0)),
            pl.BlockSpec((1, D_MEM, D_MODEL), lambda l, b: (l, 0, 0)),
            pl.BlockSpec((1, D_MODEL, D_MEM), lambda l, b: (l, 0, 0)),
        ],
        out_specs=[
            pl.BlockSpec((1, 1, D_MEM, N_MEM_HEADS * m), lambda l, b: (l, b, 0, 0)),
            pl.BlockSpec((1, 1, N_MEM_HEADS * m, D_MEM), lambda l, b: (l, b, 0, 0)),
        ],
        out_shape=[
            jax.ShapeDtypeStruct((DEPTH, bsz, D_MEM, N_MEM_HEADS * m), BF16),
            jax.ShapeDtypeStruct((DEPTH, bsz, N_MEM_HEADS * m, D_MEM), BF16),
        ],
        name="mem_kv",
    )(mem, wkT, wv)


def _sgu_kernel(x_ref, win_ref, g_ref, b_ref, ws_ref, bs_ref, mix_ref, qm_ref, *, tm):
    lane = lax.broadcasted_iota(jnp.int32, (CHUNK, LANES), 1)
    low = lane < HEAD_DIM
    wrow = lax.broadcasted_iota(jnp.int32, (CHUNK, 2 * CHUNK), 0)
    wcol = lax.broadcasted_iota(jnp.int32, (CHUNK, 2 * CHUNK), 1) & (CHUNK - 1)
    causal = wcol <= wrow
    xb = x_ref[...].astype(BF16)
    z = _dot(xb, win_ref[...])
    zu = jax.nn.gelu(z[:, :D_MIX], approximate=True)
    zv = jax.nn.gelu(z[:, D_MIX:2 * D_MIX], approximate=True)
    zv = _layer_norm(zv, g_ref[...], b_ref[...])
    qm_ref[...] = z[:, 2 * D_MIX:].astype(BF16)
    for p in range(N_MIX_HEADS // 2):
        w = jnp.where(causal, ws_ref[p], 0.0).astype(BF16)
        bias = bs_ref[:, p * LANES:(p + 1) * LANES]
        for c0 in range(0, tm // CHUNK, 2):
            rhs = []
            for c in (c0, c0 + 1):
                slab = zv[c * CHUNK:(c + 1) * CHUNK, p * LANES:(p + 1) * LANES]
                rhs.append(jnp.concatenate(
                    [jnp.where(low, slab, 0.0), jnp.where(low, 0.0, slab)], axis=0))
            rhs = jnp.concatenate(rhs, axis=1).astype(BF16)
            mixed = _dot(w, rhs)
            for i, c in enumerate((c0, c0 + 1)):
                u = zu[c * CHUNK:(c + 1) * CHUNK, p * LANES:(p + 1) * LANES]
                out = u * (mixed[:, i * LANES:(i + 1) * LANES] + bias)
                mix_ref[c * CHUNK:(c + 1) * CHUNK, p * LANES:(p + 1) * LANES] = out.astype(BF16)


def _sgu(x2d, w_in, ln_g, ln_b, ws_pair, bs_full, tm):
    t = x2d.shape[0]
    n_in = w_in.shape[1]
    return pl.pallas_call(
        functools.partial(_sgu_kernel, tm=tm),
        grid=(t // tm,),
        in_specs=[
            pl.BlockSpec((tm, D_MODEL), lambda i: (i, 0)),
            _const_spec((D_MODEL, n_in)),
            _const_spec((1, D_MIX)),
            _const_spec((1, D_MIX)),
            _const_spec((N_MIX_HEADS // 2, CHUNK, 2 * CHUNK)),
            _const_spec((CHUNK, D_MIX)),
        ],
        out_specs=[pl.BlockSpec((tm, D_MIX), lambda i: (i, 0)),
                   pl.BlockSpec((tm, D_MEM), lambda i: (i, 0))],
        out_shape=[jax.ShapeDtypeStruct((t, D_MIX), BF16),
                   jax.ShapeDtypeStruct((t, D_MEM), BF16)],
        compiler_params=pltpu.CompilerParams(
            dimension_semantics=("arbitrary",), vmem_limit_bytes=VMEM_LIMIT),
        name="sgu_mixer",
    )(x2d, w_in, ln_g, ln_b, ws_pair, bs_full)


def _tail_kernel(mix_ref, qm_ref, x_ref, mkT_ref, mvb_ref, wo_ref, g_ref, b_ref, o_ref, *, rows):
    m = N_MEM_TOKENS
    for r0 in range(0, x_ref.shape[0], rows):
        rs = slice(r0, r0 + rows)
        s = _dot(qm_ref[rs, :], mkT_ref[0])
        ps = []
        for h in range(N_MEM_HEADS):
            sh = s[:, h * m:(h + 1) * m]
            e = jnp.exp(sh - jnp.max(sh, axis=-1, keepdims=True))
            ps.append((e / jnp.sum(e, axis=-1, keepdims=True)).astype(BF16))
        p = jnp.concatenate(ps, axis=1)
        mo = _dot(p, mvb_ref[0]).astype(BF16)
        y = _dot(mix_ref[rs, :], wo_ref[:D_MIX, :]) + _dot(mo, wo_ref[D_MIX:, :])
        r = DN_ALPHA * x_ref[rs, :] + y
        o_ref[rs, :] = _layer_norm(r, g_ref[...], b_ref[...])


def _tail(mix, qm, x2d, mkT, mvb, w_o, ln_g, ln_b, seq, tm, layer):
    t = x2d.shape[0]
    m = N_MEM_TOKENS
    per_b = seq // tm
    return pl.pallas_call(
        functools.partial(_tail_kernel, rows=tm),
        grid=(t // tm,),
        in_specs=[
            pl.BlockSpec((tm, D_MIX), lambda i: (i, 0)),
            pl.BlockSpec((tm, D_MEM), lambda i: (i, 0)),
            pl.BlockSpec((tm, D_MODEL), lambda i: (i, 0)),
            pl.BlockSpec((None, 1, D_MEM, N_MEM_HEADS * m), lambda i: (layer, i // per_b, 0, 0)),
            pl.BlockSpec((None, 1, N_MEM_HEADS * m, D_MEM), lambda i: (layer, i // per_b, 0, 0)),
            _layer_spec((D_MODEL, D_MODEL), layer),
            _layer_spec((1, D_MODEL), 2 * layer),
            _layer_spec((1, D_MODEL), 2 * layer),
        ],
        out_specs=pl.BlockSpec((tm, D_MODEL), lambda i: (i, 0)),
        out_shape=jax.ShapeDtypeStruct((t, D_MODEL), F32),
        compiler_params=pltpu.CompilerParams(
            dimension_semantics=("arbitrary",), vmem_limit_bytes=VMEM_LIMIT),
        name="mixer_tail",
    )(mix, qm, x2d, mkT, mvb, w_o, ln_g, ln_b)


def _mlp_kernel(x_ref, wu_ref, wd_ref, g_ref, b_ref, o_ref, h_ref, *, n_chunk, rows):
    for r0 in range(0, x_ref.shape[0], rows):
        rs = slice(r0, r0 + rows)
        x = x_ref[rs, :]
        xb = x.astype(BF16)
        for c in range(D_FF // n_chunk):
            h = _dot(xb, wu_ref[:, c * n_chunk:(c + 1) * n_chunk])
            h = jnp.maximum(h, 0.0)
            h_ref[rs, c * n_chunk:(c + 1) * n_chunk] = (h * h).astype(BF16)
        y = _dot(h_ref[rs, :], wd_ref[...])
        r = DN_ALPHA * x + y
        o_ref[rs, :] = _layer_norm(r, g_ref[...], b_ref[...])


def _mlp(x2d, w_up, w_down, ln_g, ln_b, tm, layer, n_chunk=1024, rows=256):
    t = x2d.shape[0]
    rows = min(rows, tm)
    return pl.pallas_call(
        functools.partial(_mlp_kernel, n_chunk=n_chunk, rows=rows),
        grid=(t // tm,),
        in_specs=[
            pl.BlockSpec((tm, D_MODEL), lambda i: (i, 0)),
            _layer_spec((D_MODEL, D_FF), layer),
            _layer_spec((D_FF, D_MODEL), layer),
            _layer_spec((1, D_MODEL), 2 * layer + 1),
            _layer_spec((1, D_MODEL), 2 * layer + 1),
        ],
        out_specs=pl.BlockSpec((tm, D_MODEL), lambda i: (i, 0)),
        out_shape=jax.ShapeDtypeStruct((t, D_MODEL), F32),
        scratch_shapes=[pltpu.VMEM((tm, D_FF), BF16)],
        compiler_params=pltpu.CompilerParams(
            dimension_semantics=("arbitrary",), vmem_limit_bytes=VMEM_LIMIT),
        name="relu2_mlp",
    )(x2d, w_up, w_down, ln_g, ln_b)


def _log_sigmoid(x):
    return jnp.minimum(x, 0.0) - jnp.log(1.0 + jnp.exp(-jnp.abs(x)))


def _bproj_kernel(x_ref, wnt_ref, wnn_ref, bf_ref,
                  qT_ref, qb_ref, ka_ref, vT_ref, qm_ref, carry_ref, *, tm):
    @pl.when(pl.program_id(1) == 0)
    def _():
        carry_ref[...] = jnp.zeros_like(carry_ref)

    xb = x_ref[...].astype(BF16)
    nt = _dot_nt(wnt_ref[...], xb)
    qT_ref[0] = nt[:D_MIX].astype(BF16)
    vrow = lax.broadcasted_iota(jnp.int32, (V_ROWS - HEAD_DIM, tm), 0)
    ones_rows = jnp.where(vrow == 0, 1.0, 0.0).astype(BF16)
    for h in range(N_MIX_HEADS):
        vT_ref[0, h, :HEAD_DIM, :] = nt[D_MIX + h * HEAD_DIM:D_MIX + (h + 1) * HEAD_DIM].astype(BF16)
        vT_ref[0, h, HEAD_DIM:, :] = ones_rows

    nn = _dot(xb, wnn_ref[...])
    qm_ref[...] = nn[:, D_MIX:D_MIX + D_MEM].astype(BF16)

    f3 = nn[:, D_MIX + D_MEM:] + bf_ref[...]
    lane = lax.broadcasted_iota(jnp.int32, (tm, LANES), 1)
    valid = (lane & 15) < N_MIX_HEADS
    valid = valid & (lane < 48)
    c = jnp.where(valid, _log_sigmoid(f3) * LOG2E, 0.0)
    rowi = lax.broadcasted_iota(jnp.int32, (tm, LANES), 0)
    d = 1
    while d < tm:
        c = c + jnp.where(rowi >= d, pltpu.roll(c, d, axis=0), 0.0)
        d *= 2
    c = c + carry_ref[...]
    carry_ref[...] = c[tm - 1:tm, :]

    hi, mid, lo = _split3(c)
    kb = jnp.where(lane < 16, -hi,
                   jnp.where(lane < 32, -mid,
                             jnp.where(lane < 48, -lo,
                                       jnp.where(lane < 51, 1.0, 0.0))))
    kb = kb.astype(BF16)
    for p in range(N_MIX_HEADS // 2):
        ka_ref[0, p, :, :LANES] = nn[:, p * LANES:(p + 1) * LANES].astype(BF16)
        ka_ref[0, p, :, LANES:] = kb

    cT = c.T
    brow = lax.broadcasted_iota(jnp.int32, (BIAS_ROWS, tm), 0)
    for h in range(N_MIX_HEADS):
        chi, cmid, clo = _split3(cT[h:h + 1, :])
        sel = (brow == h) | (brow == 16 + h) | (brow == 32 + h)
        tile = jnp.where(brow == 48, chi,
                         jnp.where(brow == 49, cmid,
                                   jnp.where(brow == 50, clo,
                                             jnp.where(sel, 1.0, 0.0))))
        qb_ref[0, h] = tile.astype(BF16)


def _bproj(x3d, wnt, wnn, bf3, tm):
    bsz, seq, _ = x3d.shape
    n_nn = wnn.shape[1]
    return pl.pallas_call(
        functools.partial(_bproj_kernel, tm=tm),
        grid=(bsz, seq // tm),
        in_specs=[
            pl.BlockSpec((None, tm, D_MODEL), lambda b, j: (b, j, 0)),
            _const_spec((2 * D_MIX, D_MODEL)),
            _const_spec((D_MODEL, n_nn)),
            _const_spec((1, LANES)),
        ],
        out_specs=[
            pl.BlockSpec((1, D_MIX, tm), lambda b, j: (b, 0, j)),
            pl.BlockSpec((1, N_MIX_HEADS, BIAS_ROWS, tm), lambda b, j: (b, 0, 0, j)),
            pl.BlockSpec((1, N_MIX_HEADS // 2, tm, 2 * LANES), lambda b, j: (b, 0, j, 0)),
            pl.BlockSpec((1, N_MIX_HEADS, V_ROWS, tm), lambda b, j: (b, 0, 0, j)),
            pl.BlockSpec((tm, D_MEM), lambda b, j: (b * (seq // tm) + j, 0)),
        ],
        out_shape=[
            jax.ShapeDtypeStruct((bsz, D_MIX, seq), BF16),
            jax.ShapeDtypeStruct((bsz, N_MIX_HEADS, BIAS_ROWS, seq), BF16),
            jax.ShapeDtypeStruct((bsz, N_MIX_HEADS // 2, seq, 2 * LANES), BF16),
            jax.ShapeDtypeStruct((bsz, N_MIX_HEADS, V_ROWS, seq), BF16),
            jax.ShapeDtypeStruct((bsz * seq, D_MEM), BF16),
        ],
        scratch_shapes=[pltpu.VMEM((1, LANES), F32)],
        compiler_params=pltpu.CompilerParams(
            dimension_semantics=("arbitrary", "arbitrary"), vmem_limit_bytes=VMEM_LIMIT),
        name="fox_proj",
    )(x3d, wnt, wnn, bf3)


def _fox_kernel(qT_ref, qb_ref, ka_ref, vT_ref, o_ref, m_ref, acc_ref, s_ref, *, tq, tk):
    qi = pl.program_id(2)
    n_sub = tq // tk
    n_groups = 2 * n_sub

    srow = lax.broadcasted_iota(jnp.int32, (2 * HEAD_DIM, tk), 0)
    zpad = jnp.zeros((LANES - BIAS_ROWS, tk), BF16)
    qa = []
    for sub in range(n_sub):
        cols = slice(sub * tk, (sub + 1) * tk)
        sl = qT_ref[0, :, cols]
        for hh in range(2):
            keep = (srow < HEAD_DIM) if hh == 0 else (srow >= HEAD_DIM)
            qa.append(jnp.concatenate(
                [jnp.where(keep, sl, jnp.zeros_like(sl)), qb_ref[0, hh, :, cols], zpad], axis=0))

    m_ref[...] = jnp.full(m_ref.shape, NEG_BIG, F32)
    acc_ref[...] = jnp.zeros(acc_ref.shape, F32)

    assert n_sub % 2 == 0
    tri = (lax.broadcasted_iota(jnp.int32, (tk, tk), 0)
           <= lax.broadcasted_iota(jnp.int32, (tk, tk), 1))

    def scores(i, slot, g):
        ks = pl.multiple_of(i * tk, tk)
        s_ref[slot, g] = _dot(ka_ref[0, 0, pl.ds(ks, tk), :], qa[g]).astype(s_ref.dtype)

    def softmax_pv(i, slot, g, masked):
        ks = pl.multiple_of(i * tk, tk)
        s = s_ref[slot, g]
        if masked:
            s = jnp.where(tri, s, NEG_BIG)
        m_prev = m_ref[g]
        m_new = jnp.maximum(m_prev, jnp.max(s, axis=0, keepdims=True).astype(F32))
        m_ref[g] = m_new
        p = jnp.exp2((s - m_new.astype(s.dtype)).astype(BF16))
        pv = _dot(vT_ref[0, g % 2, :, pl.ds(ks, tk)], p)
        acc_ref[g] = jnp.exp2(m_prev - m_new) * acc_ref[g] + pv

    def block_pair(i_next, slot_next, i_cur, slot_cur, g0, diagonal):
        for g in range(g0, n_groups):
            if i_next is not None and g >= g0 + (2 if diagonal else 0):
                scores(i_next, slot_next, g)
            softmax_pv(i_cur, slot_cur, g, diagonal and g < g0 + 2)

    n_off = qi * n_sub
    for g in range(n_groups):
        scores(0, 0, g)

    def body(j, carry):
        for u in range(n_sub):
            block_pair(n_sub * j + u + 1, (u + 1) % 2, n_sub * j + u, u % 2, 0, False)
        return carry

    lax.fori_loop(0, qi, body, 0)
    for dj in range(n_sub):
        nxt = n_off + dj + 1 if dj + 1 < n_sub else None
        block_pair(nxt, (dj + 1) % 2, n_off + dj, dj % 2, 2 * dj, True)

    def normalized(g):
        acc = acc_ref[g]
        return acc[:HEAD_DIM] * (1.0 / acc[HEAD_DIM:HEAD_DIM + 1])

    o = jnp.concatenate(
        [jnp.concatenate([normalized(2 * sub + hh) for sub in range(n_sub)], axis=1)
         for hh in range(2)], axis=0)
    o_ref[0] = o.T.astype(BF16)


def _fox(qT, qb, ka, vT, tq, tk):
    bsz, _, seq = qT.shape
    n_pair = N_MIX_HEADS // 2
    return pl.pallas_call(
        functools.partial(_fox_kernel, tq=tq, tk=tk),
        grid=(bsz, n_pair, seq // tq),
        in_specs=[
            pl.BlockSpec((1, 2 * HEAD_DIM, tq), lambda b, p, i: (b, p, i)),
            pl.BlockSpec((1, 2, BIAS_ROWS, tq), lambda b, p, i: (b, p, 0, i)),
            pl.BlockSpec((1, 1, seq, 2 * LANES), lambda b, p, i: (b, p, 0, 0)),
            pl.BlockSpec((1, 2, V_ROWS, seq), lambda b, p, i: (b, p, 0, 0)),
        ],
        out_specs=pl.BlockSpec((1, tq, LANES), lambda b, p, i: (b, i, p)),
        out_shape=jax.ShapeDtypeStruct((bsz, seq, D_MIX), BF16),
        scratch_shapes=[pltpu.VMEM((2 * tq // tk, 1, tk), F32),
                        pltpu.VMEM((2 * tq // tk, V_ROWS, tk), F32),
                        pltpu.VMEM((2, 2 * tq // tk, tk, tk), BF16)],
        compiler_params=pltpu.CompilerParams(
            dimension_semantics=("arbitrary", "arbitrary", "arbitrary"),
            vmem_limit_bytes=VMEM_LIMIT),
        name="fox_attention",
    )(qT, qb, ka, vT)


def kernel(x, mem, a_w_in, a_sgu_ln_g, a_sgu_ln_b, a_w_s, a_b_s, kv_w, kv_b_f, b_w_q,
           mem_w_kv, w_o, ln_g, ln_b, w_up, w_down):
    bsz, seq, d = x.shape
    t = bsz * seq
    scale = 1.0 / math.sqrt(HEAD_DIM)

    wkT = (jnp.swapaxes(mem_w_kv[:, :, :D_MEM], 1, 2) * scale).astype(BF16)
    wv = mem_w_kv[:, :, D_MEM:].astype(BF16)
    w_in = a_w_in[0].astype(BF16)
    ws_pair = (a_w_s[0].reshape(N_MIX_HEADS // 2, 2, CHUNK, CHUNK)
               .transpose(0, 2, 1, 3).reshape(N_MIX_HEADS // 2, CHUNK, 2 * CHUNK))
    bs_full = jnp.repeat(a_b_s[0].T, HEAD_DIM, axis=1)
    wq = b_w_q[0]
    wnt = jnp.concatenate([wq[:, :D_MIX].T * (scale * LOG2E), kv_w[:, D_MIX:2 * D_MIX].T], axis=0).astype(BF16)
    wf = kv_w[:, 2 * D_MIX:]
    zf = jnp.zeros((d, 16 - N_MIX_HEADS), F32)
    wf3 = jnp.concatenate([wf, zf, wf, zf, wf, zf, jnp.zeros((d, LANES - 48), F32)], axis=1)
    wnn = jnp.concatenate([kv_w[:, :D_MIX], wq[:, D_MIX:], wf3], axis=1).astype(BF16)
    zb = jnp.zeros((16 - N_MIX_HEADS,), F32)
    bf3 = jnp.concatenate([kv_b_f, zb, kv_b_f, zb, kv_b_f, zb, jnp.zeros((LANES - 48,), F32)])[None, :]
    w_o_b = w_o.astype(BF16)
    w_up_b = w_up.astype(BF16)
    w_down_b = w_down.astype(BF16)
    row = lambda v: v[None, :]
    ln_g4 = ln_g.reshape(2 * DEPTH, 1, d)
    ln_b4 = ln_b.reshape(2 * DEPTH, 1, d)

    mkT, mvb = _mem_kv(mem, wkT, wv)
    x2d = x.reshape(t, d)

    tm_a = min(1024, seq)
    mix, qm = _sgu(x2d, w_in, row(a_sgu_ln_g[0]), row(a_sgu_ln_b[0]), ws_pair, bs_full, tm_a)
    x2d = _tail(mix, qm, x2d, mkT, mvb, w_o_b, ln_g4, ln_b4, seq, tm_a, 0)
    tm_m = min(1024, seq)
    x2d = _mlp(x2d, w_up_b, w_down_b, ln_g4, ln_b4, tm_m, 0)

    tm_b = min(1024, seq)
    qT, qb, ka, vT, qm = _bproj(x2d.reshape(bsz, seq, d), wnt, wnn, bf3, tm_b)
    tq = min(2048, seq)
    att = _fox(qT, qb, ka, vT, tq, min(256, seq)).reshape(t, D_MIX)
    x2d = _tail(att, qm, x2d, mkT, mvb, w_o_b, ln_g4, ln_b4, seq, tm_a, 1)
    x2d = _mlp(x2d, w_up_b, w_down_b, ln_g4, ln_b4, tm_m, 1)
    return x2d.reshape(bsz, seq, d)
```

```python
import functools
import math

import jax
import jax.numpy as jnp
from jax import lax
from jax.experimental import pallas as pl
from jax.experimental.pallas import tpu as pltpu

D_MODEL = 1024
HEAD_DIM = 64
N_MIX_HEADS = 12
N_MEM_HEADS = 4
D_MIX = N_MIX_HEADS * HEAD_DIM
D_MEM = N_MEM_HEADS * HEAD_DIM
CHUNK = 128
N_MEM_TOKENS = 256
D_FF = 4 * D_MODEL
DEPTH = 2
DN_ALPHA = (2 * DEPTH) ** 0.25
LN_EPS = 1e-5

LANES = 128
V_ROWS = 80
BIAS_ROWS = 64
NEG_BIG = -1e30
LOG2E = math.log2(math.e)

F32 = jnp.float32
BF16 = jnp.bfloat16

VMEM_LIMIT = 56 * 1024 * 1024


def _const_spec(shape):
    nd = len(shape)
    return pl.BlockSpec(shape, lambda *_: (0,) * nd, pipeline_mode=pl.Buffered(1))


def _layer_spec(shape, layer):
    nd = len(shape)
    return pl.BlockSpec((None,) + tuple(shape), lambda *_: (layer,) + (0,) * nd,
                        pipeline_mode=pl.Buffered(1))


def _dot(a, b):
    return jnp.dot(a, b, preferred_element_type=F32)


def _dot_nt(a, b):
    return lax.dot_general(a, b, (((1,), (1,)), ((), ())), preferred_element_type=F32)


def _layer_norm(r, g, b):
    mu = jnp.mean(r, axis=-1, keepdims=True)
    d = r - mu
    var = jnp.mean(d * d, axis=-1, keepdims=True)
    return d * lax.rsqrt(var + LN_EPS) * g + b


def _split3(c):
    hi = c.astype(BF16).astype(F32)
    r1 = c - hi
    mid = r1.astype(BF16).astype(F32)
    lo = (r1 - mid).astype(BF16).astype(F32)
    return hi, mid, lo


def _mem_kv_kernel(mem_ref, wkT_ref, wv_ref, mkT_ref, mvb_ref):
    memb = mem_ref[0].astype(BF16)
    kT = _dot_nt(wkT_ref[0], memb)
    kT4 = jnp.concatenate([kT] * N_MEM_HEADS, axis=1)
    row = lax.broadcasted_iota(jnp.int32, kT4.shape, 0)
    col = lax.broadcasted_iota(jnp.int32, kT4.shape, 1)
    same = (row >> 6) == (col >> 8)
    mkT_ref[0, 0] = jnp.where(same, kT4, 0.0).astype(BF16)
    v = _dot(memb, wv_ref[0])
    v4 = jnp.concatenate([v] * N_MEM_HEADS, axis=0)
    row = lax.broadcasted_iota(jnp.int32, v4.shape, 0)
    col = lax.broadcasted_iota(jnp.int32, v4.shape, 1)
    same = (row >> 8) == (col >> 6)
    mvb_ref[0, 0] = jnp.where(same, v4, 0.0).astype(BF16)


def _mem_kv(mem, wkT, wv):
    bsz = mem.shape[0]
    m = N_MEM_TOKENS
    return pl.pallas_call(
        _mem_kv_kernel,
        grid=(DEPTH, bsz),
        in_specs=[
            pl.BlockSpec((1, m, D_MODEL), lambda l, b: (b, 0, 0)),
            pl.BlockSpec((1, D_MEM, D_MODEL), lambda l, b: (l, 0, 0)),
            pl.BlockSpec((1, D_MODEL, D_MEM), lambda l, b: (l, 0, 0)),
        ],
        out_specs=[
            pl.BlockSpec((1, 1, D_MEM, N_MEM_HEADS * m), lambda l, b: (l, b, 0, 0)),
            pl.BlockSpec((1, 1, N_MEM_HEADS * m, D_MEM), lambda l, b: (l, b, 0, 0)),
        ],
        out_shape=[
            jax.ShapeDtypeStruct((DEPTH, bsz, D_MEM, N_MEM_HEADS * m), BF16),
            jax.ShapeDtypeStruct((DEPTH, bsz, N_MEM_HEADS * m, D_MEM), BF16),
        ],
        name="mem_kv",
    )(mem, wkT, wv)


def _sgu_kernel(x_ref, win_ref, g_ref, b_ref, ws_ref, bs_ref, mix_ref, qm_ref, *, tm):
    lane = lax.broadcasted_iota(jnp.int32, (CHUNK, LANES), 1)
    low = lane < HEAD_DIM
    wrow = lax.broadcasted_iota(jnp.int32, (CHUNK, 2 * CHUNK), 0)
    wcol = lax.broadcasted_iota(jnp.int32, (CHUNK, 2 * CHUNK), 1) & (CHUNK - 1)
    causal = wcol <= wrow
    xb = x_ref[...].astype(BF16)
    z = _dot(xb, win_ref[...])
    zu = jax.nn.gelu(z[:, :D_MIX], approximate=True)
    zv = jax.nn.gelu(z[:, D_MIX:2 * D_MIX], approximate=True)
    zv = _layer_norm(zv, g_ref[...], b_ref[...])
    qm_ref[...] = z[:, 2 * D_MIX:].astype(BF16)
    for p in range(N_MIX_HEADS // 2):
        w = jnp.where(causal, ws_ref[p], 0.0).astype(BF16)
        bias = bs_ref[:, p * LANES:(p + 1) * LANES]
        for c0 in range(0, tm // CHUNK, 2):
            rhs = []
            for c in (c0, c0 + 1):
                slab = zv[c * CHUNK:(c + 1) * CHUNK, p * LANES:(p + 1) * LANES]
                rhs.append(jnp.concatenate(
                    [jnp.where(low, slab, 0.0), jnp.where(low, 0.0, slab)], axis=0))
            rhs = jnp.concatenate(rhs, axis=1).astype(BF16)
            mixed = _dot(w, rhs)
            for i, c in enumerate((c0, c0 + 1)):
                u = zu[c * CHUNK:(c + 1) * CHUNK, p * LANES:(p + 1) * LANES]
                out = u * (mixed[:, i * LANES:(i + 1) * LANES] + bias)
                mix_ref[c * CHUNK:(c + 1) * CHUNK, p * LANES:(p + 1) * LANES] = out.astype(BF16)


def _sgu(x2d, w_in, ln_g, ln_b, ws_pair, bs_full, tm):
    t = x2d.shape[0]
    n_in = w_in.shape[1]
    return pl.pallas_call(
        functools.partial(_sgu_kernel, tm=tm),
        grid=(t // tm,),
        in_specs=[
            pl.BlockSpec((tm, D_MODEL), lambda i: (i, 0)),
            _const_spec((D_MODEL, n_in)),
            _const_spec((1, D_MIX)),
            _const_spec((1, D_MIX)),
            _const_spec((N_MIX_HEADS // 2, CHUNK, 2 * CHUNK)),
            _const_spec((CHUNK, D_MIX)),
        ],
        out_specs=[pl.BlockSpec((tm, D_MIX), lambda i: (i, 0)),
                   pl.BlockSpec((tm, D_MEM), lambda i: (i, 0))],
        out_shape=[jax.ShapeDtypeStruct((t, D_MIX), BF16),
                   jax.ShapeDtypeStruct((t, D_MEM), BF16)],
        compiler_params=pltpu.CompilerParams(
            dimension_semantics=("arbitrary",), vmem_limit_bytes=VMEM_LIMIT),
        name="sgu_mixer",
    )(x2d, w_in, ln_g, ln_b, ws_pair, bs_full)


def _tail_kernel(mix_ref, qm_ref, x_ref, mkT_ref, mvb_ref, wo_ref, g_ref, b_ref, o_ref, *, rows):
    m = N_MEM_TOKENS
    for r0 in range(0, x_ref.shape[0], rows):
        rs = slice(r0, r0 + rows)
        s = _dot(qm_ref[rs, :], mkT_ref[0])
        ps = []
        for h in range(N_MEM_HEADS):
            sh = s[:, h * m:(h + 1) * m]
            e = jnp.exp(sh - jnp.max(sh, axis=-1, keepdims=True))
            ps.append((e / jnp.sum(e, axis=-1, keepdims=True)).astype(BF16))
        p = jnp.concatenate(ps, axis=1)
        mo = _dot(p, mvb_ref[0]).astype(BF16)
        y = _dot(mix_ref[rs, :], wo_ref[:D_MIX, :]) + _dot(mo, wo_ref[D_MIX:, :])
        r = DN_ALPHA * x_ref[rs, :] + y
        o_ref[rs, :] = _layer_norm(r, g_ref[...], b_ref[...])


def _tail(mix, qm, x2d, mkT, mvb, w_o, ln_g, ln_b, seq, tm, layer):
    t = x2d.shape[0]
    m = N_MEM_TOKENS
    per_b = seq // tm
    return pl.pallas_call(
        functools.partial(_tail_kernel, rows=tm),
        grid=(t // tm,),
        in_specs=[
            pl.BlockSpec((tm, D_MIX), lambda i: (i, 0)),
            pl.BlockSpec((tm, D_MEM), lambda i: (i, 0)),
            pl.BlockSpec((tm, D_MODEL), lambda i: (i, 0)),
            pl.BlockSpec((None, 1, D_MEM, N_MEM_HEADS * m), lambda i: (layer, i // per_b, 0, 0)),
            pl.BlockSpec((None, 1, N_MEM_HEADS * m, D_MEM), lambda i: (layer, i // per_b, 0, 0)),
            _layer_spec((D_MODEL, D_MODEL), layer),
            _layer_spec((1, D_MODEL), 2 * layer),
            _layer_spec((1, D_MODEL), 2 * layer),
        ],
        out_specs=pl.BlockSpec((tm, D_MODEL), lambda i: (i, 0)),
        out_shape=jax.ShapeDtypeStruct((t, D_MODEL), F32),
        compiler_params=pltpu.CompilerParams(
            dimension_semantics=("arbitrary",), vmem_limit_bytes=VMEM_LIMIT),
        name="mixer_tail",
    )(mix, qm, x2d, mkT, mvb, w_o, ln_g, ln_b)


def _mlp_kernel(x_ref, wu32_ref, wd32_ref, g_ref, b_ref, o_ref, wu_ref, wd_ref, h_ref,
                *, n_chunk, rows, n_w, wc):
    step = pl.program_id(0)

    @pl.when(step < n_w)
    def _():
        c0 = pl.multiple_of(step * wc, wc)
        wu_ref[:, pl.ds(c0, wc)] = wu32_ref[...].astype(BF16)
        wd_ref[pl.ds(c0, wc), :] = wd32_ref[...].astype(BF16)

    @pl.when(step >= n_w - 1)
    def _():
        for r0 in range(0, x_ref.shape[0], rows):
            rs = slice(r0, r0 + rows)
            x = x_ref[rs, :]
            xb = x.astype(BF16)
            for c in range(D_FF // n_chunk):
                h = _dot(xb, wu_ref[:, c * n_chunk:(c + 1) * n_chunk])
                h = jnp.maximum(h, 0.0)
                h_ref[rs, c * n_chunk:(c + 1) * n_chunk] = (h * h).astype(BF16)
            y = _dot(h_ref[rs, :], wd_ref[...])
            r = DN_ALPHA * x + y
            o_ref[rs, :] = _layer_norm(r, g_ref[...], b_ref[...])


def _mlp(x2d, w_up, w_down, ln_g, ln_b, tm, layer, n_chunk=1024, rows=256, wc=512):
    t = x2d.shape[0]
    rows = min(rows, tm)
    n_w = D_FF // wc
    tile = lambda s: (jnp.maximum(s - (n_w - 1), 0), 0)
    return pl.pallas_call(
        functools.partial(_mlp_kernel, n_chunk=n_chunk, rows=rows, n_w=n_w, wc=wc),
        grid=(n_w - 1 + t // tm,),
        in_specs=[
            pl.BlockSpec((tm, D_MODEL), tile),
            pl.BlockSpec((None, D_MODEL, wc), lambda s: (layer, 0, jnp.minimum(s, n_w - 1))),
            pl.BlockSpec((None, wc, D_MODEL), lambda s: (layer, jnp.minimum(s, n_w - 1), 0)),
            _layer_spec((1, D_MODEL), 2 * layer + 1),
            _layer_spec((1, D_MODEL), 2 * layer + 1),
        ],
        out_specs=pl.BlockSpec((tm, D_MODEL), tile),
        out_shape=jax.ShapeDtypeStruct((t, D_MODEL), F32),
        scratch_shapes=[pltpu.VMEM((D_MODEL, D_FF), BF16),
                        pltpu.VMEM((D_FF, D_MODEL), BF16),
                        pltpu.VMEM((tm, D_FF), BF16)],
        compiler_params=pltpu.CompilerParams(
            dimension_semantics=("arbitrary",), vmem_limit_bytes=VMEM_LIMIT),
        name="relu2_mlp",
    )(x2d, w_up, w_down, ln_g, ln_b)


def _log_sigmoid(x):
    return jnp.minimum(x, 0.0) - jnp.log(1.0 + jnp.exp(-jnp.abs(x)))


def _bproj_kernel(x_ref, wnt_ref, wnn_ref, bf_ref,
                  qT_ref, qb_ref, ka_ref, vT_ref, qm_ref, carry_ref, *, tm):
    @pl.when(pl.program_id(1) == 0)
    def _():
        carry_ref[...] = jnp.zeros_like(carry_ref)

    xb = x_ref[...].astype(BF16)
    nt = _dot_nt(wnt_ref[...], xb)
    qT_ref[0] = nt[:D_MIX].astype(BF16)
    vrow = lax.broadcasted_iota(jnp.int32, (V_ROWS - HEAD_DIM, tm), 0)
    ones_rows = jnp.where(vrow == 0, 1.0, 0.0).astype(BF16)
    for h in range(N_MIX_HEADS):
        vT_ref[0, h, :HEAD_DIM, :] = nt[D_MIX + h * HEAD_DIM:D_MIX + (h + 1) * HEAD_DIM].astype(BF16)
        vT_ref[0, h, HEAD_DIM:, :] = ones_rows

    nn = _dot(xb, wnn_ref[...])
    qm_ref[...] = nn[:, D_MIX:D_MIX + D_MEM].astype(BF16)

    f3 = nn[:, D_MIX + D_MEM:] + bf_ref[...]
    lane = lax.broadcasted_iota(jnp.int32, (tm, LANES), 1)
    valid = (lane & 15) < N_MIX_HEADS
    valid = valid & (lane < 48)
    c = jnp.where(valid, _log_sigmoid(f3) * LOG2E, 0.0)
    rowi = lax.broadcasted_iota(jnp.int32, (tm, LANES), 0)
    d = 1
    while d < tm:
        c = c + jnp.where(rowi >= d, pltpu.roll(c, d, axis=0), 0.0)
        d *= 2
    c = c + carry_ref[...]
    carry_ref[...] = c[tm - 1:tm, :]

    hi, mid, lo = _split3(c)
    kb = jnp.where(lane < 16, -hi,
                   jnp.where(lane < 32, -mid,
                             jnp.where(lane < 48, -lo,
                                       jnp.where(lane < 51, 1.0, 0.0))))
    kb = kb.astype(BF16)
    for p in range(N_MIX_HEADS // 2):
        ka_ref[0, p, :, :LANES] = nn[:, p * LANES:(p + 1) * LANES].astype(BF16)
        ka_ref[0, p, :, LANES:] = kb

    cT = c.T
    brow = lax.broadcasted_iota(jnp.int32, (BIAS_ROWS, tm), 0)
    for h in range(N_MIX_HEADS):
        chi, cmid, clo = _split3(cT[h:h + 1, :])
        sel = (brow == h) | (brow == 16 + h) | (brow == 32 + h)
        tile = jnp.where(brow == 48, chi,
                         jnp.where(brow == 49, cmid,
                                   jnp.where(brow == 50, clo,
                                             jnp.where(sel, 1.0, 0.0))))
        qb_ref[0, h] = tile.astype(BF16)


def _bproj(x3d, wnt, wnn, bf3, tm):
    bsz, seq, _ = x3d.shape
    n_nn = wnn.shape[1]
    return pl.pallas_call(
        functools.partial(_bproj_kernel, tm=tm),
        grid=(bsz, seq // tm),
        in_specs=[
            pl.BlockSpec((None, tm, D_MODEL), lambda b, j: (b, j, 0)),
            _const_spec((2 * D_MIX, D_MODEL)),
            _const_spec((D_MODEL, n_nn)),
            _const_spec((1, LANES)),
        ],
        out_specs=[
            pl.BlockSpec((1, D_MIX, tm), lambda b, j: (b, 0, j)),
            pl.BlockSpec((1, N_MIX_HEADS, BIAS_ROWS, tm), lambda b, j: (b, 0, 0, j)),
            pl.BlockSpec((1, N_MIX_HEADS // 2, tm, 2 * LANES), lambda b, j: (b, 0, j, 0)),
            pl.BlockSpec((1, N_MIX_HEADS, V_ROWS, tm), lambda b, j: (b, 0, 0, j)),
            pl.BlockSpec((tm, D_MEM), lambda b, j: (b * (seq // tm) + j, 0)),
        ],
        out_shape=[
            jax.ShapeDtypeStruct((bsz, D_MIX, seq), BF16),
            jax.ShapeDtypeStruct((bsz, N_MIX_HEADS, BIAS_ROWS, seq), BF16),
            jax.ShapeDtypeStruct((bsz, N_MIX_HEADS // 2, seq, 2 * LANES), BF16),
            jax.ShapeDtypeStruct((bsz, N_MIX_HEADS, V_ROWS, seq), BF16),
            jax.ShapeDtypeStruct((bsz * seq, D_MEM), BF16),
        ],
        scratch_shapes=[pltpu.VMEM((1, LANES), F32)],
        compiler_params=pltpu.CompilerParams(
            dimension_semantics=("arbitrary", "arbitrary"), vmem_limit_bytes=VMEM_LIMIT),
        name="fox_proj",
    )(x3d, wnt, wnn, bf3)


def _fox_kernel(qT_ref, qb_ref, ka_ref, vT_ref, o_ref, m_ref, acc_ref, s_ref, *, tq, tk):
    qi = pl.program_id(2)
    n_sub = tq // tk
    n_groups = 2 * n_sub

    srow = lax.broadcasted_iota(jnp.int32, (2 * HEAD_DIM, tk), 0)
    zpad = jnp.zeros((LANES - BIAS_ROWS, tk), BF16)
    qa = []
    for sub in range(n_sub):
        cols = slice(sub * tk, (sub + 1) * tk)
        sl = qT_ref[0, :, cols]
        for hh in range(2):
            keep = (srow < HEAD_DIM) if hh == 0 else (srow >= HEAD_DIM)
            qa.append(jnp.concatenate(
                [jnp.where(keep, sl, jnp.zeros_like(sl)), qb_ref[0, hh, :, cols], zpad], axis=0))

    m_ref[...] = jnp.full(m_ref.shape, NEG_BIG, F32)
    acc_ref[...] = jnp.zeros(acc_ref.shape, F32)

    assert n_sub % 2 == 0
    tri = (lax.broadcasted_iota(jnp.int32, (tk, tk), 0)
           <= lax.broadcasted_iota(jnp.int32, (tk, tk), 1))

    def scores(i, slot, g):
        ks = pl.multiple_of(i * tk, tk)
        s_ref[slot, g] = _dot(ka_ref[0, 0, pl.ds(ks, tk), :], qa[g]).astype(s_ref.dtype)

    def softmax_pv(i, slot, g, masked):
        ks = pl.multiple_of(i * tk, tk)
        s = s_ref[slot, g]
        if masked:
            s = jnp.where(tri, s, NEG_BIG)
        m_prev = m_ref[g]
        m_new = jnp.maximum(m_prev, jnp.max(s, axis=0, keepdims=True).astype(F32))
        m_ref[g] = m_new
        p = jnp.exp2((s - m_new.astype(s.dtype)).astype(BF16))
        pv = _dot(vT_ref[0, g % 2, :, pl.ds(ks, tk)], p)
        acc_ref[g] = jnp.exp2(m_prev - m_new) * acc_ref[g] + pv

    def block_pair(i_next, slot_next, i_cur, slot_cur, g0, diagonal):
        for g in range(g0, n_groups):
            if i_next is not None and g >= g0 + (2 if diagonal else 0):
                scores(i_next, slot_next, g)
            softmax_pv(i_cur, slot_cur, g, diagonal and g < g0 + 2)

    n_off = qi * n_sub
    for g in range(n_groups):
        scores(0, 0, g)

    def body(j, carry):
        for u in range(n_sub):
            block_pair(n_sub * j + u + 1, (u + 1) % 2, n_sub * j + u, u % 2, 0, False)
        return carry

    lax.fori_loop(0, qi, body, 0)
    for dj in range(n_sub):
        nxt = n_off + dj + 1 if dj + 1 < n_sub else None
        block_pair(nxt, (dj + 1) % 2, n_off + dj, dj % 2, 2 * dj, True)

    def normalized(g):
        acc = acc_ref[g]
        return acc[:HEAD_DIM] * (1.0 / acc[HEAD_DIM:HEAD_DIM + 1])

    o = jnp.concatenate(
        [jnp.concatenate([normalized(2 * sub + hh) for sub in range(n_sub)], axis=1)
         for hh in range(2)], axis=0)
    o_ref[0] = o.T.astype(BF16)


def _fox(qT, qb, ka, vT, tq, tk):
    bsz, _, seq = qT.shape
    n_pair = N_MIX_HEADS // 2
    return pl.pallas_call(
        functools.partial(_fox_kernel, tq=tq, tk=tk),
        grid=(bsz, n_pair, seq // tq),
        in_specs=[
            pl.BlockSpec((1, 2 * HEAD_DIM, tq), lambda b, p, i: (b, p, i)),
            pl.BlockSpec((1, 2, BIAS_ROWS, tq), lambda b, p, i: (b, p, 0, i)),
            pl.BlockSpec((1, 1, seq, 2 * LANES), lambda b, p, i: (b, p, 0, 0)),
            pl.BlockSpec((1, 2, V_ROWS, seq), lambda b, p, i: (b, p, 0, 0)),
        ],
        out_specs=pl.BlockSpec((1, tq, LANES), lambda b, p, i: (b, i, p)),
        out_shape=jax.ShapeDtypeStruct((bsz, seq, D_MIX), BF16),
        scratch_shapes=[pltpu.VMEM((2 * tq // tk, 1, tk), F32),
                        pltpu.VMEM((2 * tq // tk, V_ROWS, tk), F32),
                        pltpu.VMEM((2, 2 * tq // tk, tk, tk), BF16)],
        compiler_params=pltpu.CompilerParams(
            dimension_semantics=("arbitrary", "arbitrary", "arbitrary"),
            vmem_limit_bytes=VMEM_LIMIT),
        name="fox_attention",
    )(qT, qb, ka, vT)


def kernel(x, mem, a_w_in, a_sgu_ln_g, a_sgu_ln_b, a_w_s, a_b_s, kv_w, kv_b_f, b_w_q,
           mem_w_kv, w_o, ln_g, ln_b, w_up, w_down):
    bsz, seq, d = x.shape
    t = bsz * seq
    scale = 1.0 / math.sqrt(HEAD_DIM)

    wkT = (jnp.swapaxes(mem_w_kv[:, :, :D_MEM], 1, 2) * scale).astype(BF16)
    wv = mem_w_kv[:, :, D_MEM:].astype(BF16)
    w_in = a_w_in[0].astype(BF16)
    ws_pair = (a_w_s[0].reshape(N_MIX_HEADS // 2, 2, CHUNK, CHUNK)
               .transpose(0, 2, 1, 3).reshape(N_MIX_HEADS // 2, CHUNK, 2 * CHUNK))
    bs_full = jnp.repeat(a_b_s[0].T, HEAD_DIM, axis=1)
    wq = b_w_q[0]
    wnt = jnp.concatenate([wq[:, :D_MIX].T * (scale * LOG2E), kv_w[:, D_MIX:2 * D_MIX].T], axis=0).astype(BF16)
    wf = kv_w[:, 2 * D_MIX:]
    zf = jnp.zeros((d, 16 - N_MIX_HEADS), F32)
    wf3 = jnp.concatenate([wf, zf, wf, zf, wf, zf, jnp.zeros((d, LANES - 48), F32)], axis=1)
    wnn = jnp.concatenate([kv_w[:, :D_MIX], wq[:, D_MIX:], wf3], axis=1).astype(BF16)
    zb = jnp.zeros((16 - N_MIX_HEADS,), F32)
    bf3 = jnp.concatenate([kv_b_f, zb, kv_b_f, zb, kv_b_f, zb, jnp.zeros((LANES - 48,), F32)])[None, :]
    w_o_b = w_o.astype(BF16)
    row = lambda v: v[None, :]
    ln_g4 = ln_g.reshape(2 * DEPTH, 1, d)
    ln_b4 = ln_b.reshape(2 * DEPTH, 1, d)

    mkT, mvb = _mem_kv(mem, wkT, wv)
    x2d = x.reshape(t, d)

    tm_a = min(1024, seq)
    mix, qm = _sgu(x2d, w_in, row(a_sgu_ln_g[0]), row(a_sgu_ln_b[0]), ws_pair, bs_full, tm_a)
    x2d = _tail(mix, qm, x2d, mkT, mvb, w_o_b, ln_g4, ln_b4, seq, tm_a, 0)
    tm_m = min(1024, seq)
    x2d = _mlp(x2d, w_up, w_down, ln_g4, ln_b4, tm_m,0)

    tm_b = min(1024, seq)
    qT, qb, ka, vT, qm = _bproj(x2d.reshape(bsz, seq, d), wnt, wnn, bf3, tm_b)
    tq = min(2048, seq)
    att = _fox(qT, qb, ka, vT, tq, min(256, seq)).reshape(t, D_MIX)
    x2d = _tail(att, qm, x2d, mkT, mvb, w_o_b, ln_g4, ln_b4, seq, tm_a, 1)
    x2d = _mlp(x2d, w_up, w_down, ln_g4, ln_b4, tm_m,1)
    return x2d.reshape(bsz, seq, d)
```

```python
import functools
import math

import jax
import jax.numpy as jnp
from jax import lax
from jax.experimental import pallas as pl
from jax.experimental.pallas import tpu as pltpu

D_MODEL = 1024
HEAD_DIM = 64
N_MIX_HEADS = 12
N_MEM_HEADS = 4
D_MIX = N_MIX_HEADS * HEAD_DIM
D_MEM = N_MEM_HEADS * HEAD_DIM
CHUNK = 128
N_MEM_TOKENS = 256
D_FF = 4 * D_MODEL
DEPTH = 2
DN_ALPHA = (2 * DEPTH) ** 0.25
LN_EPS = 1e-5

LANES = 128
V_ROWS = 80
BIAS_ROWS = 64
NEG_BIG = -1e30
LOG2E = math.log2(math.e)

F32 = jnp.float32
BF16 = jnp.bfloat16

VMEM_LIMIT = 56 * 1024 * 1024


def _const_spec(shape):
    nd = len(shape)
    return pl.BlockSpec(shape, lambda *_: (0,) * nd, pipeline_mode=pl.Buffered(1))


def _layer_spec(shape, layer):
    nd = len(shape)
    return pl.BlockSpec((None,) + tuple(shape), lambda *_: (layer,) + (0,) * nd,
                        pipeline_mode=pl.Buffered(1))


def _dot(a, b):
    return jnp.dot(a, b, preferred_element_type=F32)


def _dot_nt(a, b):
    return lax.dot_general(a, b, (((1,), (1,)), ((), ())), preferred_element_type=F32)


def _layer_norm(r, g, b):
    mu = jnp.mean(r, axis=-1, keepdims=True)
    d = r - mu
    var = jnp.mean(d * d, axis=-1, keepdims=True)
    return d * lax.rsqrt(var + LN_EPS) * g + b


def _split3(c):
    hi = c.astype(BF16).astype(F32)
    r1 = c - hi
    mid = r1.astype(BF16).astype(F32)
    lo = (r1 - mid).astype(BF16).astype(F32)
    return hi, mid, lo


def _mem_kv_kernel(mem_ref, wkT_ref, wv_ref, mkT_ref, mvb_ref):
    memb = mem_ref[0].astype(BF16)
    kT = _dot_nt(wkT_ref[0], memb)
    kT4 = jnp.concatenate([kT] * N_MEM_HEADS, axis=1)
    row = lax.broadcasted_iota(jnp.int32, kT4.shape, 0)
    col = lax.broadcasted_iota(jnp.int32, kT4.shape, 1)
    same = (row >> 6) == (col >> 8)
    mkT_ref[0, 0] = jnp.where(same, kT4, 0.0).astype(BF16)
    v = _dot(memb, wv_ref[0])
    v4 = jnp.concatenate([v] * N_MEM_HEADS, axis=0)
    row = lax.broadcasted_iota(jnp.int32, v4.shape, 0)
    col = lax.broadcasted_iota(jnp.int32, v4.shape, 1)
    same = (row >> 8) == (col >> 6)
    mvb_ref[0, 0] = jnp.where(same, v4, 0.0).astype(BF16)


def _mem_kv(mem, wkT, wv):
    bsz = mem.shape[0]
    m = N_MEM_TOKENS
    return pl.pallas_call(
        _mem_kv_kernel,
        grid=(DEPTH, bsz),
        in_specs=[
            pl.BlockSpec((1, m, D_MODEL), lambda l, b: (b, 0, 0)),
            pl.BlockSpec((1, D_MEM, D_MODEL), lambda l, b: (l, 0, 0)),
            pl.BlockSpec((1, D_MODEL, D_MEM), lambda l, b: (l, 0, 0)),
        ],
        out_specs=[
            pl.BlockSpec((1, 1, D_MEM, N_MEM_HEADS * m), lambda l, b: (l, b, 0, 0)),
            pl.BlockSpec((1, 1, N_MEM_HEADS * m, D_MEM), lambda l, b: (l, b, 0, 0)),
        ],
        out_shape=[
            jax.ShapeDtypeStruct((DEPTH, bsz, D_MEM, N_MEM_HEADS * m), BF16),
            jax.ShapeDtypeStruct((DEPTH, bsz, N_MEM_HEADS * m, D_MEM), BF16),
        ],
        name="mem_kv",
    )(mem, wkT, wv)


def _sgu_kernel(x_ref, win32_ref, g_ref, b_ref, ws_ref, bs_ref, mix_ref, qm_ref, win_ref, *, tm):
    @pl.when(pl.program_id(0) == 0)
    def _():
        win_ref[...] = win32_ref[...].astype(BF16)

    lane = lax.broadcasted_iota(jnp.int32, (CHUNK, LANES), 1)
    low = lane < HEAD_DIM
    wrow = lax.broadcasted_iota(jnp.int32, (CHUNK, 2 * CHUNK), 0)
    wcol = lax.broadcasted_iota(jnp.int32, (CHUNK, 2 * CHUNK), 1) & (CHUNK - 1)
    causal = wcol <= wrow
    xb = x_ref[...].astype(BF16)
    z = _dot(xb, win_ref[...])
    zu = jax.nn.gelu(z[:, :D_MIX], approximate=True)
    zv = jax.nn.gelu(z[:, D_MIX:2 * D_MIX], approximate=True)
    zv = _layer_norm(zv, g_ref[...], b_ref[...])
    qm_ref[...] = z[:, 2 * D_MIX:].astype(BF16)
    for p in range(N_MIX_HEADS // 2):
        w = jnp.where(causal, ws_ref[p], 0.0).astype(BF16)
        bias = bs_ref[:, p * LANES:(p + 1) * LANES]
        for c0 in range(0, tm // CHUNK, 2):
            rhs = []
            for c in (c0, c0 + 1):
                slab = zv[c * CHUNK:(c + 1) * CHUNK, p * LANES:(p + 1) * LANES]
                rhs.append(jnp.concatenate(
                    [jnp.where(low, slab, 0.0), jnp.where(low, 0.0, slab)], axis=0))
            rhs = jnp.concatenate(rhs, axis=1).astype(BF16)
            mixed = _dot(w, rhs)
            for i, c in enumerate((c0, c0 + 1)):
                u = zu[c * CHUNK:(c + 1) * CHUNK, p * LANES:(p + 1) * LANES]
                out = u * (mixed[:, i * LANES:(i + 1) * LANES] + bias)
                mix_ref[c * CHUNK:(c + 1) * CHUNK, p * LANES:(p + 1) * LANES] = out.astype(BF16)


def _sgu(x2d, w_in, ln_g, ln_b, ws_pair, bs_full, tm):
    t = x2d.shape[0]
    n_in = w_in.shape[-1]
    return pl.pallas_call(
        functools.partial(_sgu_kernel, tm=tm),
        grid=(t // tm,),
        in_specs=[
            pl.BlockSpec((tm, D_MODEL), lambda i: (i, 0)),
            _layer_spec((D_MODEL, n_in), 0),
            _const_spec((1, D_MIX)),
            _const_spec((1, D_MIX)),
            _const_spec((N_MIX_HEADS // 2, CHUNK, 2 * CHUNK)),
            _const_spec((CHUNK, D_MIX)),
        ],
        out_specs=[pl.BlockSpec((tm, D_MIX), lambda i: (i, 0)),
                   pl.BlockSpec((tm, D_MEM), lambda i: (i, 0))],
        out_shape=[jax.ShapeDtypeStruct((t, D_MIX), BF16),
                   jax.ShapeDtypeStruct((t, D_MEM), BF16)],
        scratch_shapes=[pltpu.VMEM((D_MODEL, n_in), BF16)],
        compiler_params=pltpu.CompilerParams(
            dimension_semantics=("arbitrary",), vmem_limit_bytes=VMEM_LIMIT),
        name="sgu_mixer",
    )(x2d, w_in, ln_g, ln_b, ws_pair, bs_full)


def _tail_kernel(mix_ref, qm_ref, x_ref, mkT_ref, mvb_ref, wo32_ref, g_ref, b_ref, o_ref, wo_ref,
                 *, rows):
    @pl.when(pl.program_id(0) == 0)
    def _():
        wo_ref[...] = wo32_ref[...].astype(BF16)

    m = N_MEM_TOKENS
    for r0 in range(0, x_ref.shape[0], rows):
        rs = slice(r0, r0 + rows)
        s = _dot(qm_ref[rs, :], mkT_ref[0])
        ps = []
        for h in range(N_MEM_HEADS):
            sh = s[:, h * m:(h + 1) * m]
            e = jnp.exp(sh - jnp.max(sh, axis=-1, keepdims=True))
            ps.append((e / jnp.sum(e, axis=-1, keepdims=True)).astype(BF16))
        p = jnp.concatenate(ps, axis=1)
        mo = _dot(p, mvb_ref[0]).astype(BF16)
        y = _dot(mix_ref[rs, :], wo_ref[:D_MIX, :]) + _dot(mo, wo_ref[D_MIX:, :])
        r = DN_ALPHA * x_ref[rs, :] + y
        o_ref[rs, :] = _layer_norm(r, g_ref[...], b_ref[...])


def _tail(mix, qm, x2d, mkT, mvb, w_o, ln_g, ln_b, seq, tm, layer):
    t = x2d.shape[0]
    m = N_MEM_TOKENS
    per_b = seq // tm
    return pl.pallas_call(
        functools.partial(_tail_kernel, rows=tm),
        grid=(t // tm,),
        in_specs=[
            pl.BlockSpec((tm, D_MIX), lambda i: (i, 0)),
            pl.BlockSpec((tm, D_MEM), lambda i: (i, 0)),
            pl.BlockSpec((tm, D_MODEL), lambda i: (i, 0)),
            pl.BlockSpec((None, 1, D_MEM, N_MEM_HEADS * m), lambda i: (layer, i // per_b, 0, 0)),
            pl.BlockSpec((None, 1, N_MEM_HEADS * m, D_MEM), lambda i: (layer, i // per_b, 0, 0)),
            _layer_spec((D_MODEL, D_MODEL), layer),
            _layer_spec((1, D_MODEL), 2 * layer),
            _layer_spec((1, D_MODEL), 2 * layer),
        ],
        out_specs=pl.BlockSpec((tm, D_MODEL), lambda i: (i, 0)),
        out_shape=jax.ShapeDtypeStruct((t, D_MODEL), F32),
        scratch_shapes=[pltpu.VMEM((D_MODEL, D_MODEL), BF16)],
        compiler_params=pltpu.CompilerParams(
            dimension_semantics=("arbitrary",), vmem_limit_bytes=VMEM_LIMIT),
        name="mixer_tail",
    )(mix, qm, x2d, mkT, mvb, w_o, ln_g, ln_b)


def _mlp_kernel(x_ref, wu32_ref, wd32_ref, g_ref, b_ref, o_ref, wu_ref, wd_ref, h_ref,
                *, n_chunk, rows, n_w, wc):
    step = pl.program_id(0)

    @pl.when(step < n_w)
    def _():
        c0 = pl.multiple_of(step * wc, wc)
        wu_ref[:, pl.ds(c0, wc)] = wu32_ref[...].astype(BF16)
        wd_ref[pl.ds(c0, wc), :] = wd32_ref[...].astype(BF16)

    @pl.when(step >= n_w - 1)
    def _():
        for r0 in range(0, x_ref.shape[0], rows):
            rs = slice(r0, r0 + rows)
            x = x_ref[rs, :]
            xb = x.astype(BF16)
            for c in range(D_FF // n_chunk):
                h = _dot(xb, wu_ref[:, c * n_chunk:(c + 1) * n_chunk])
                h = jnp.maximum(h, 0.0)
                h_ref[rs, c * n_chunk:(c + 1) * n_chunk] = (h * h).astype(BF16)
            y = _dot(h_ref[rs, :], wd_ref[...])
            r = DN_ALPHA * x + y
            o_ref[rs, :] = _layer_norm(r, g_ref[...], b_ref[...])


def _mlp(x2d, w_up, w_down, ln_g, ln_b, tm, layer, n_chunk=1024, rows=256, wc=512):
    t = x2d.shape[0]
    rows = min(rows, tm)
    n_w = D_FF // wc
    tile = lambda s: (jnp.maximum(s - (n_w - 1), 0), 0)
    return pl.pallas_call(
        functools.partial(_mlp_kernel, n_chunk=n_chunk, rows=rows, n_w=n_w, wc=wc),
        grid=(n_w - 1 + t // tm,),
        in_specs=[
            pl.BlockSpec((tm, D_MODEL), tile),
            pl.BlockSpec((None, D_MODEL, wc), lambda s: (layer, 0, jnp.minimum(s, n_w - 1))),
            pl.BlockSpec((None, wc, D_MODEL), lambda s: (layer, jnp.minimum(s, n_w - 1), 0)),
            _layer_spec((1, D_MODEL), 2 * layer + 1),
            _layer_spec((1, D_MODEL), 2 * layer + 1),
        ],
        out_specs=pl.BlockSpec((tm, D_MODEL), tile),
        out_shape=jax.ShapeDtypeStruct((t, D_MODEL), F32),
        scratch_shapes=[pltpu.VMEM((D_MODEL, D_FF), BF16),
                        pltpu.VMEM((D_FF, D_MODEL), BF16),
                        pltpu.VMEM((tm, D_FF), BF16)],
        compiler_params=pltpu.CompilerParams(
            dimension_semantics=("arbitrary",), vmem_limit_bytes=VMEM_LIMIT),
        name="relu2_mlp",
    )(x2d, w_up, w_down, ln_g, ln_b)


def _log_sigmoid(x):
    return jnp.minimum(x, 0.0) - jnp.log(1.0 + jnp.exp(-jnp.abs(x)))


def _bproj_kernel(x_ref, wq32_ref, kvw32_ref, wf3_ref, bf_ref,
                  qT_ref, qb_ref, ka_ref, vT_ref, qm_ref, wnt_ref, wnn_ref, carry_ref, *, tm, q_scale):
    @pl.when((pl.program_id(0) == 0) & (pl.program_id(1) == 0))
    def _():
        wq = wq32_ref[...]
        wnt_ref[:D_MIX, :] = (wq[:, :D_MIX] * q_scale).T.astype(BF16)
        wnt_ref[D_MIX:, :] = kvw32_ref[:, D_MIX:2 * D_MIX].T.astype(BF16)
        wnn_ref[:, :D_MIX] = kvw32_ref[:, :D_MIX].astype(BF16)
        wnn_ref[:, D_MIX:D_MIX + D_MEM] = wq[:, D_MIX:].astype(BF16)
        wnn_ref[:, D_MIX + D_MEM:] = wf3_ref[...].astype(BF16)

    @pl.when(pl.program_id(1) == 0)
    def _():
        carry_ref[...] = jnp.zeros_like(carry_ref)

    xb = x_ref[...].astype(BF16)
    nt = _dot_nt(wnt_ref[...], xb)
    qT_ref[0] = nt[:D_MIX].astype(BF16)
    vrow = lax.broadcasted_iota(jnp.int32, (V_ROWS - HEAD_DIM, tm), 0)
    ones_rows = jnp.where(vrow == 0, 1.0, 0.0).astype(BF16)
    for h in range(N_MIX_HEADS):
        vT_ref[0, h, :HEAD_DIM, :] = nt[D_MIX + h * HEAD_DIM:D_MIX + (h + 1) * HEAD_DIM].astype(BF16)
        vT_ref[0, h, HEAD_DIM:, :] = ones_rows

    nn = _dot(xb, wnn_ref[...])
    qm_ref[...] = nn[:, D_MIX:D_MIX + D_MEM].astype(BF16)

    f3 = nn[:, D_MIX + D_MEM:] + bf_ref[...]
    lane = lax.broadcasted_iota(jnp.int32, (tm, LANES), 1)
    valid = (lane & 15) < N_MIX_HEADS
    valid = valid & (lane < 48)
    c = jnp.where(valid, _log_sigmoid(f3) * LOG2E, 0.0)
    rowi = lax.broadcasted_iota(jnp.int32, (tm, LANES), 0)
    d = 1
    while d < tm:
        c = c + jnp.where(rowi >= d, pltpu.roll(c, d, axis=0), 0.0)
        d *= 2
    c = c + carry_ref[...]
    carry_ref[...] = c[tm - 1:tm, :]

    hi, mid, lo = _split3(c)
    kb = jnp.where(lane < 16, -hi,
                   jnp.where(lane < 32, -mid,
                             jnp.where(lane < 48, -lo,
                                       jnp.where(lane < 51, 1.0, 0.0))))
    kb = kb.astype(BF16)
    for p in range(N_MIX_HEADS // 2):
        ka_ref[0, p, :, :LANES] = nn[:, p * LANES:(p + 1) * LANES].astype(BF16)
        ka_ref[0, p, :, LANES:] = kb

    cT = c.T
    brow = lax.broadcasted_iota(jnp.int32, (BIAS_ROWS, tm), 0)
    for h in range(N_MIX_HEADS):
        chi, cmid, clo = _split3(cT[h:h + 1, :])
        sel = (brow == h) | (brow == 16 + h) | (brow == 32 + h)
        tile = jnp.where(brow == 48, chi,
                         jnp.where(brow == 49, cmid,
                                   jnp.where(brow == 50, clo,
                                             jnp.where(sel, 1.0, 0.0))))
        qb_ref[0, h] = tile.astype(BF16)


def _bproj(x3d, b_w_q, kv_w, wf3, bf3, tm, q_scale):
    bsz, seq, _ = x3d.shape
    n_nn = D_MIX + D_MEM + LANES
    return pl.pallas_call(
        functools.partial(_bproj_kernel, tm=tm, q_scale=q_scale),
        grid=(bsz, seq // tm),
        in_specs=[
            pl.BlockSpec((None, tm, D_MODEL), lambda b, j: (b, j, 0)),
            _layer_spec((D_MODEL, D_MODEL), 0),
            _const_spec(kv_w.shape),
            _const_spec((D_MODEL, LANES)),
            _const_spec((1, LANES)),
        ],
        out_specs=[
            pl.BlockSpec((1, D_MIX, tm), lambda b, j: (b, 0, j)),
            pl.BlockSpec((1, N_MIX_HEADS, BIAS_ROWS, tm), lambda b, j: (b, 0, 0, j)),
            pl.BlockSpec((1, N_MIX_HEADS // 2, tm, 2 * LANES), lambda b, j: (b, 0, j, 0)),
            pl.BlockSpec((1, N_MIX_HEADS, V_ROWS, tm), lambda b, j: (b, 0, 0, j)),
            pl.BlockSpec((tm, D_MEM), lambda b, j: (b * (seq // tm) + j, 0)),
        ],
        out_shape=[
            jax.ShapeDtypeStruct((bsz, D_MIX, seq), BF16),
            jax.ShapeDtypeStruct((bsz, N_MIX_HEADS, BIAS_ROWS, seq), BF16),
            jax.ShapeDtypeStruct((bsz, N_MIX_HEADS // 2, seq, 2 * LANES), BF16),
            jax.ShapeDtypeStruct((bsz, N_MIX_HEADS, V_ROWS, seq), BF16),
            jax.ShapeDtypeStruct((bsz * seq, D_MEM), BF16),
        ],
        scratch_shapes=[pltpu.VMEM((2 * D_MIX, D_MODEL), BF16),
                        pltpu.VMEM((D_MODEL, n_nn), BF16),
                        pltpu.VMEM((1, LANES), F32)],
        compiler_params=pltpu.CompilerParams(
            dimension_semantics=("arbitrary", "arbitrary"), vmem_limit_bytes=VMEM_LIMIT),
        name="fox_proj",
    )(x3d, b_w_q, kv_w, wf3, bf3)


def _fox_kernel(qT_ref, qb_ref, ka_ref, vT_ref, o_ref, m_ref, acc_ref, s_ref, *, tq, tk):
    qi = pl.program_id(2)
    n_sub = tq // tk
    n_groups = 2 * n_sub

    srow = lax.broadcasted_iota(jnp.int32, (2 * HEAD_DIM, tk), 0)
    zpad = jnp.zeros((LANES - BIAS_ROWS, tk), BF16)
    qa = []
    for sub in range(n_sub):
        cols = slice(sub * tk, (sub + 1) * tk)
        sl = qT_ref[0, :, cols]
        for hh in range(2):
            keep = (srow < HEAD_DIM) if hh == 0 else (srow >= HEAD_DIM)
            qa.append(jnp.concatenate(
                [jnp.where(keep, sl, jnp.zeros_like(sl)), qb_ref[0, hh, :, cols], zpad], axis=0))

    m_ref[...] = jnp.full(m_ref.shape, NEG_BIG, F32)
    acc_ref[...] = jnp.zeros(acc_ref.shape, F32)

    assert n_sub % 2 == 0
    tri = (lax.broadcasted_iota(jnp.int32, (tk, tk), 0)
           <= lax.broadcasted_iota(jnp.int32, (tk, tk), 1))

    def scores(i, slot, g):
        ks = pl.multiple_of(i * tk, tk)
        s_ref[slot, g] = _dot(ka_ref[0, 0, pl.ds(ks, tk), :], qa[g]).astype(s_ref.dtype)

    def softmax_pv(i, slot, g, masked):
        ks = pl.multiple_of(i * tk, tk)
        s = s_ref[slot, g]
        if masked:
            s = jnp.where(tri, s, NEG_BIG)
        m_prev = m_ref[g]
        m_new = jnp.maximum(m_prev, jnp.max(s, axis=0, keepdims=True).astype(F32))
        m_ref[g] = m_new
        p = jnp.exp2((s - m_new.astype(s.dtype)).astype(BF16))
        pv = _dot(vT_ref[0, g % 2, :, pl.ds(ks, tk)], p)
        acc_ref[g] = jnp.exp2(m_prev - m_new) * acc_ref[g] + pv

    def block_pair(i_next, slot_next, i_cur, slot_cur, g0, diagonal):
        for g in range(g0, n_groups):
            if i_next is not None and g >= g0 + (2 if diagonal else 0):
                scores(i_next, slot_next, g)
            softmax_pv(i_cur, slot_cur, g, diagonal and g < g0 + 2)

    n_off = qi * n_sub
    for g in range(n_groups):
        scores(0, 0, g)

    def body(j, carry):
        for u in range(n_sub):
            block_pair(n_sub * j + u + 1, (u + 1) % 2, n_sub * j + u, u % 2, 0, False)
        return carry

    lax.fori_loop(0, qi, body, 0)
    for dj in range(n_sub):
        nxt = n_off + dj + 1 if dj + 1 < n_sub else None
        block_pair(nxt, (dj + 1) % 2, n_off + dj, dj % 2, 2 * dj, True)

    def normalized(g):
        acc = acc_ref[g]
        return acc[:HEAD_DIM] * (1.0 / acc[HEAD_DIM:HEAD_DIM + 1])

    o = jnp.concatenate(
        [jnp.concatenate([normalized(2 * sub + hh) for sub in range(n_sub)], axis=1)
         for hh in range(2)], axis=0)
    o_ref[0] = o.T.astype(BF16)


def _fox(qT, qb, ka, vT, tq, tk):
    bsz, _, seq = qT.shape
    n_pair = N_MIX_HEADS // 2
    return pl.pallas_call(
        functools.partial(_fox_kernel, tq=tq, tk=tk),
        grid=(bsz, n_pair, seq // tq),
        in_specs=[
            pl.BlockSpec((1, 2 * HEAD_DIM, tq), lambda b, p, i: (b, p, i)),
            pl.BlockSpec((1, 2, BIAS_ROWS, tq), lambda b, p, i: (b, p, 0, i)),
            pl.BlockSpec((1, 1, seq, 2 * LANES), lambda b, p, i: (b, p, 0, 0)),
            pl.BlockSpec((1, 2, V_ROWS, seq), lambda b, p, i: (b, p, 0, 0)),
        ],
        out_specs=pl.BlockSpec((1, tq, LANES), lambda b, p, i: (b, i, p)),
        out_shape=jax.ShapeDtypeStruct((bsz, seq, D_MIX), BF16),
        scratch_shapes=[pltpu.VMEM((2 * tq // tk, 1, tk), F32),
                        pltpu.VMEM((2 * tq // tk, V_ROWS, tk), F32),
                        pltpu.VMEM((2, 2 * tq // tk, tk, tk), BF16)],
        compiler_params=pltpu.CompilerParams(
            dimension_semantics=("arbitrary", "arbitrary", "arbitrary"),
            vmem_limit_bytes=VMEM_LIMIT),
        name="fox_attention",
    )(qT, qb, ka, vT)


def kernel(x, mem, a_w_in, a_sgu_ln_g, a_sgu_ln_b, a_w_s, a_b_s, kv_w, kv_b_f, b_w_q,
           mem_w_kv, w_o, ln_g, ln_b, w_up, w_down):
    bsz, seq, d = x.shape
    t = bsz * seq
    scale = 1.0 / math.sqrt(HEAD_DIM)

    wkT = (jnp.swapaxes(mem_w_kv[:, :, :D_MEM], 1, 2) * scale).astype(BF16)
    wv = mem_w_kv[:, :, D_MEM:].astype(BF16)
    ws_pair = (a_w_s[0].reshape(N_MIX_HEADS // 2, 2, CHUNK, CHUNK)
               .transpose(0, 2, 1, 3).reshape(N_MIX_HEADS // 2, CHUNK, 2 * CHUNK))
    bs_full = jnp.repeat(a_b_s[0].T, HEAD_DIM, axis=1)
    wf = kv_w[:, 2 * D_MIX:]
    zf = jnp.zeros((d, 16 - N_MIX_HEADS), F32)
    wf3 = jnp.concatenate([wf, zf, wf, zf, wf, zf, jnp.zeros((d, LANES - 48), F32)], axis=1)
    zb = jnp.zeros((16 - N_MIX_HEADS,), F32)
    bf3 = jnp.concatenate([kv_b_f, zb, kv_b_f, zb, kv_b_f, zb, jnp.zeros((LANES - 48,), F32)])[None, :]
    row = lambda v: v[None, :]
    ln_g4 = ln_g.reshape(2 * DEPTH, 1, d)
    ln_b4 = ln_b.reshape(2 * DEPTH, 1, d)

    mkT, mvb = _mem_kv(mem, wkT, wv)
    x2d = x.reshape(t, d)

    tm_a = min(1024, seq)
    mix, qm = _sgu(x2d, a_w_in, row(a_sgu_ln_g[0]), row(a_sgu_ln_b[0]), ws_pair, bs_full, tm_a)
    x2d = _tail(mix, qm, x2d, mkT, mvb, w_o, ln_g4, ln_b4, seq, tm_a, 0)
    tm_m = min(1024, seq)
    x2d = _mlp(x2d, w_up, w_down, ln_g4, ln_b4, tm_m,0)

    tm_b = min(1024, seq)
    qT, qb, ka, vT, qm = _bproj(x2d.reshape(bsz, seq, d), b_w_q, kv_w, wf3, bf3, tm_b, scale * LOG2E)
    tq = min(2048, seq)
    att = _fox(qT, qb, ka, vT, tq, min(256, seq)).reshape(t, D_MIX)
    x2d = _tail(att, qm, x2d, mkT, mvb, w_o, ln_g4, ln_b4, seq, tm_a, 1)
    x2d = _mlp(x2d, w_up, w_down, ln_g4, ln_b4, tm_m,1)
    return x2d.reshape(bsz, seq, d)
```

```python
import functools
import math

import jax
import jax.numpy as jnp
from jax import lax
from jax.experimental import pallas as pl
from jax.experimental.pallas import tpu as pltpu

D_MODEL = 1024
HEAD_DIM = 64
N_MIX_HEADS = 12
N_MEM_HEADS = 4
D_MIX = N_MIX_HEADS * HEAD_DIM
D_MEM = N_MEM_HEADS * HEAD_DIM
CHUNK = 128
N_MEM_TOKENS = 256
D_FF = 4 * D_MODEL
DEPTH = 2
DN_ALPHA = (2 * DEPTH) ** 0.25
LN_EPS = 1e-5

LANES = 128
V_ROWS = 80
BIAS_ROWS = 64
NEG_BIG = -1e30
LOG2E = math.log2(math.e)

F32 = jnp.float32
BF16 = jnp.bfloat16

VMEM_LIMIT = 56 * 1024 * 1024


def _const_spec(shape):
    nd = len(shape)
    return pl.BlockSpec(shape, lambda *_: (0,) * nd, pipeline_mode=pl.Buffered(1))


def _layer_spec(shape, layer):
    nd = len(shape)
    return pl.BlockSpec((None,) + tuple(shape), lambda *_: (layer,) + (0,) * nd,
                        pipeline_mode=pl.Buffered(1))


def _dot(a, b):
    return jnp.dot(a, b, preferred_element_type=F32)


def _dot_nt(a, b):
    return lax.dot_general(a, b, (((1,), (1,)), ((), ())), preferred_element_type=F32)


def _layer_norm(r, g, b):
    mu = jnp.mean(r, axis=-1, keepdims=True)
    d = r - mu
    var = jnp.mean(d * d, axis=-1, keepdims=True)
    return d * lax.rsqrt(var + LN_EPS) * g + b


def _split3(c):
    hi = c.astype(BF16).astype(F32)
    r1 = c - hi
    mid = r1.astype(BF16).astype(F32)
    lo = (r1 - mid).astype(BF16).astype(F32)
    return hi, mid, lo


def _mem_kv_kernel(mem_ref, wkT_ref, wv_ref, mkT_ref, mvb_ref):
    memb = mem_ref[0].astype(BF16)
    kT = _dot_nt(wkT_ref[0], memb)
    kT4 = jnp.concatenate([kT] * N_MEM_HEADS, axis=1)
    row = lax.broadcasted_iota(jnp.int32, kT4.shape, 0)
    col = lax.broadcasted_iota(jnp.int32, kT4.shape, 1)
    same = (row >> 6) == (col >> 8)
    mkT_ref[0, 0] = jnp.where(same, kT4, 0.0).astype(BF16)
    v = _dot(memb, wv_ref[0])
    v4 = jnp.concatenate([v] * N_MEM_HEADS, axis=0)
    row = lax.broadcasted_iota(jnp.int32, v4.shape, 0)
    col = lax.broadcasted_iota(jnp.int32, v4.shape, 1)
    same = (row >> 8) == (col >> 6)
    mvb_ref[0, 0] = jnp.where(same, v4, 0.0).astype(BF16)


def _mem_kv(mem, wkT, wv):
    bsz = mem.shape[0]
    m = N_MEM_TOKENS
    return pl.pallas_call(
        _mem_kv_kernel,
        grid=(DEPTH, bsz),
        in_specs=[
            pl.BlockSpec((1, m, D_MODEL), lambda l, b: (b, 0, 0)),
            pl.BlockSpec((1, D_MEM, D_MODEL), lambda l, b: (l, 0, 0)),
            pl.BlockSpec((1, D_MODEL, D_MEM), lambda l, b: (l, 0, 0)),
        ],
        out_specs=[
            pl.BlockSpec((1, 1, D_MEM, N_MEM_HEADS * m), lambda l, b: (l, b, 0, 0)),
            pl.BlockSpec((1, 1, N_MEM_HEADS * m, D_MEM), lambda l, b: (l, b, 0, 0)),
        ],
        out_shape=[
            jax.ShapeDtypeStruct((DEPTH, bsz, D_MEM, N_MEM_HEADS * m), BF16),
            jax.ShapeDtypeStruct((DEPTH, bsz, N_MEM_HEADS * m, D_MEM), BF16),
        ],
        name="mem_kv",
    )(mem, wkT, wv)


def _sgu_kernel(x_ref, win32_ref, g_ref, b_ref, ws_ref, bs_ref, mix_ref, qm_ref, win_ref, *, tm):
    @pl.when(pl.program_id(0) == 0)
    def _():
        win_ref[...] = win32_ref[...].astype(BF16)

    lane = lax.broadcasted_iota(jnp.int32, (CHUNK, LANES), 1)
    low = lane < HEAD_DIM
    wrow = lax.broadcasted_iota(jnp.int32, (CHUNK, 2 * CHUNK), 0)
    wcol = lax.broadcasted_iota(jnp.int32, (CHUNK, 2 * CHUNK), 1) & (CHUNK - 1)
    causal = wcol <= wrow
    xb = x_ref[...].astype(BF16)
    z = _dot(xb, win_ref[...])
    zu = jax.nn.gelu(z[:, :D_MIX], approximate=True)
    zv = jax.nn.gelu(z[:, D_MIX:2 * D_MIX], approximate=True)
    zv = _layer_norm(zv, g_ref[...], b_ref[...])
    qm_ref[...] = z[:, 2 * D_MIX:].astype(BF16)
    for p in range(N_MIX_HEADS // 2):
        w = jnp.where(causal, ws_ref[p], 0.0).astype(BF16)
        bias = bs_ref[:, p * LANES:(p + 1) * LANES]
        for c0 in range(0, tm // CHUNK, 2):
            rhs = []
            for c in (c0, c0 + 1):
                slab = zv[c * CHUNK:(c + 1) * CHUNK, p * LANES:(p + 1) * LANES]
                rhs.append(jnp.concatenate(
                    [jnp.where(low, slab, 0.0), jnp.where(low, 0.0, slab)], axis=0))
            rhs = jnp.concatenate(rhs, axis=1).astype(BF16)
            mixed = _dot(w, rhs)
            for i, c in enumerate((c0, c0 + 1)):
                u = zu[c * CHUNK:(c + 1) * CHUNK, p * LANES:(p + 1) * LANES]
                out = u * (mixed[:, i * LANES:(i + 1) * LANES] + bias)
                mix_ref[c * CHUNK:(c + 1) * CHUNK, p * LANES:(p + 1) * LANES] = out.astype(BF16)


def _sgu(x2d, w_in, ln_g, ln_b, ws_pair, bs_full, tm):
    t = x2d.shape[0]
    n_in = w_in.shape[-1]
    return pl.pallas_call(
        functools.partial(_sgu_kernel, tm=tm),
        grid=(t // tm,),
        in_specs=[
            pl.BlockSpec((tm, D_MODEL), lambda i: (i, 0)),
            _layer_spec((D_MODEL, n_in), 0),
            _const_spec((1, D_MIX)),
            _const_spec((1, D_MIX)),
            _const_spec((N_MIX_HEADS // 2, CHUNK, 2 * CHUNK)),
            _const_spec((CHUNK, D_MIX)),
        ],
        out_specs=[pl.BlockSpec((tm, D_MIX), lambda i: (i, 0)),
                   pl.BlockSpec((tm, D_MEM), lambda i: (i, 0))],
        out_shape=[jax.ShapeDtypeStruct((t, D_MIX), BF16),
                   jax.ShapeDtypeStruct((t, D_MEM), BF16)],
        scratch_shapes=[pltpu.VMEM((D_MODEL, n_in), BF16)],
        compiler_params=pltpu.CompilerParams(
            dimension_semantics=("arbitrary",), vmem_limit_bytes=VMEM_LIMIT),
        name="sgu_mixer",
    )(x2d, w_in, ln_g, ln_b, ws_pair, bs_full)


def _tail_kernel(mix_ref, qm_ref, x_ref, mkT_ref, mvb_ref, wo32_ref, g_ref, b_ref, o_ref, wo_ref,
                 *, rows):
    @pl.when(pl.program_id(0) == 0)
    def _():
        wo_ref[...] = wo32_ref[...].astype(BF16)

    m = N_MEM_TOKENS
    for r0 in range(0, x_ref.shape[0], rows):
        rs = slice(r0, r0 + rows)
        s = _dot(qm_ref[rs, :], mkT_ref[0])
        ps = []
        for h in range(N_MEM_HEADS):
            sh = s[:, h * m:(h + 1) * m]
            e = jnp.exp(sh - jnp.max(sh, axis=-1, keepdims=True))
            ps.append((e / jnp.sum(e, axis=-1, keepdims=True)).astype(BF16))
        p = jnp.concatenate(ps, axis=1)
        mo = _dot(p, mvb_ref[0]).astype(BF16)
        y = _dot(mix_ref[rs, :], wo_ref[:D_MIX, :]) + _dot(mo, wo_ref[D_MIX:, :])
        r = DN_ALPHA * x_ref[rs, :] + y
        o_ref[rs, :] = _layer_norm(r, g_ref[...], b_ref[...])


def _tail(mix, qm, x2d, mkT, mvb, w_o, ln_g, ln_b, seq, tm, layer):
    t = x2d.shape[0]
    m = N_MEM_TOKENS
    per_b = seq // tm
    return pl.pallas_call(
        functools.partial(_tail_kernel, rows=tm),
        grid=(t // tm,),
        in_specs=[
            pl.BlockSpec((tm, D_MIX), lambda i: (i, 0)),
            pl.BlockSpec((tm, D_MEM), lambda i: (i, 0)),
            pl.BlockSpec((tm, D_MODEL), lambda i: (i, 0)),
            pl.BlockSpec((None, 1, D_MEM, N_MEM_HEADS * m), lambda i: (layer, i // per_b, 0, 0)),
            pl.BlockSpec((None, 1, N_MEM_HEADS * m, D_MEM), lambda i: (layer, i // per_b, 0, 0)),
            _layer_spec((D_MODEL, D_MODEL), layer),
            _layer_spec((1, D_MODEL), 2 * layer),
            _layer_spec((1, D_MODEL), 2 * layer),
        ],
        out_specs=pl.BlockSpec((tm, D_MODEL), lambda i: (i, 0)),
        out_shape=jax.ShapeDtypeStruct((t, D_MODEL), F32),
        scratch_shapes=[pltpu.VMEM((D_MODEL, D_MODEL), BF16)],
        compiler_params=pltpu.CompilerParams(
            dimension_semantics=("arbitrary",), vmem_limit_bytes=VMEM_LIMIT),
        name="mixer_tail",
    )(mix, qm, x2d, mkT, mvb, w_o, ln_g, ln_b)


def _mlp_kernel(x_ref, wu32_ref, wd32_ref, g_ref, b_ref, o_ref, wu_ref, wd_ref, h_ref,
                *, n_chunk, rows, n_w, wc):
    step = pl.program_id(0)

    @pl.when(step < n_w)
    def _():
        c0 = pl.multiple_of(step * wc, wc)
        wu_ref[:, pl.ds(c0, wc)] = wu32_ref[...].astype(BF16)
        wd_ref[pl.ds(c0, wc), :] = wd32_ref[...].astype(BF16)

    @pl.when(step >= n_w - 1)
    def _():
        for r0 in range(0, x_ref.shape[0], rows):
            rs = slice(r0, r0 + rows)
            x = x_ref[rs, :]
            xb = x.astype(BF16)
            for c in range(D_FF // n_chunk):
                h = _dot(xb, wu_ref[:, c * n_chunk:(c + 1) * n_chunk])
                h = jnp.maximum(h, 0.0)
                h_ref[rs, c * n_chunk:(c + 1) * n_chunk] = (h * h).astype(BF16)
            y = _dot(h_ref[rs, :], wd_ref[...])
            r = DN_ALPHA * x + y
            o_ref[rs, :] = _layer_norm(r, g_ref[...], b_ref[...])


def _mlp(x2d, w_up, w_down, ln_g, ln_b, tm, layer, n_chunk=1024, rows=256, wc=1024):
    t = x2d.shape[0]
    rows = min(rows, tm)
    n_w = D_FF // wc
    tile = lambda s: (jnp.maximum(s - (n_w - 1), 0), 0)
    return pl.pallas_call(
        functools.partial(_mlp_kernel, n_chunk=n_chunk, rows=rows, n_w=n_w, wc=wc),
        grid=(n_w - 1 + t // tm,),
        in_specs=[
            pl.BlockSpec((tm, D_MODEL), tile),
            pl.BlockSpec((None, D_MODEL, wc), lambda s: (layer, 0, jnp.minimum(s, n_w - 1))),
            pl.BlockSpec((None, wc, D_MODEL), lambda s: (layer, jnp.minimum(s, n_w - 1), 0)),
            _layer_spec((1, D_MODEL), 2 * layer + 1),
            _layer_spec((1, D_MODEL), 2 * layer + 1),
        ],
        out_specs=pl.BlockSpec((tm, D_MODEL), tile),
        out_shape=jax.ShapeDtypeStruct((t, D_MODEL), F32),
        scratch_shapes=[pltpu.VMEM((D_MODEL, D_FF), BF16),
                        pltpu.VMEM((D_FF, D_MODEL), BF16),
                        pltpu.VMEM((tm, D_FF), BF16)],
        compiler_params=pltpu.CompilerParams(
            dimension_semantics=("arbitrary",), vmem_limit_bytes=VMEM_LIMIT),
        name="relu2_mlp",
    )(x2d, w_up, w_down, ln_g, ln_b)


def _log_sigmoid(x):
    return jnp.minimum(x, 0.0) - jnp.log(1.0 + jnp.exp(-jnp.abs(x)))


def _bproj_kernel(x_ref, wq32_ref, kvw32_ref, wf3_ref, bf_ref,
                  qT_ref, qb_ref, ka_ref, vT_ref, qm_ref, wnt_ref, wnn_ref, carry_ref, *, tm, q_scale):
    @pl.when((pl.program_id(0) == 0) & (pl.program_id(1) == 0))
    def _():
        wq = wq32_ref[...]
        wnt_ref[:D_MIX, :] = (wq[:, :D_MIX] * q_scale).T.astype(BF16)
        wnt_ref[D_MIX:, :] = kvw32_ref[:, D_MIX:2 * D_MIX].T.astype(BF16)
        wnn_ref[:, :D_MIX] = kvw32_ref[:, :D_MIX].astype(BF16)
        wnn_ref[:, D_MIX:D_MIX + D_MEM] = wq[:, D_MIX:].astype(BF16)
        wnn_ref[:, D_MIX + D_MEM:] = wf3_ref[...].astype(BF16)

    @pl.when(pl.program_id(1) == 0)
    def _():
        carry_ref[...] = jnp.zeros_like(carry_ref)

    xb = x_ref[...].astype(BF16)
    nt = _dot_nt(wnt_ref[...], xb)
    qT_ref[0] = nt[:D_MIX].astype(BF16)
    vrow = lax.broadcasted_iota(jnp.int32, (V_ROWS - HEAD_DIM, tm), 0)
    ones_rows = jnp.where(vrow == 0, 1.0, 0.0).astype(BF16)
    for h in range(N_MIX_HEADS):
        vT_ref[0, h, :HEAD_DIM, :] = nt[D_MIX + h * HEAD_DIM:D_MIX + (h + 1) * HEAD_DIM].astype(BF16)
        vT_ref[0, h, HEAD_DIM:, :] = ones_rows

    nn = _dot(xb, wnn_ref[...])
    qm_ref[...] = nn[:, D_MIX:D_MIX + D_MEM].astype(BF16)

    f3 = nn[:, D_MIX + D_MEM:] + bf_ref[...]
    lane = lax.broadcasted_iota(jnp.int32, (tm, LANES), 1)
    valid = (lane & 15) < N_MIX_HEADS
    valid = valid & (lane < 48)
    c = jnp.where(valid, _log_sigmoid(f3) * LOG2E, 0.0)
    rowi = lax.broadcasted_iota(jnp.int32, (tm, LANES), 0)
    d = 1
    while d < tm:
        c = c + jnp.where(rowi >= d, pltpu.roll(c, d, axis=0), 0.0)
        d *= 2
    c = c + carry_ref[...]
    carry_ref[...] = c[tm - 1:tm, :]

    hi, mid, lo = _split3(c)
    kb = jnp.where(lane < 16, -hi,
                   jnp.where(lane < 32, -mid,
                             jnp.where(lane < 48, -lo,
                                       jnp.where(lane < 51, 1.0, 0.0))))
    kb = kb.astype(BF16)
    for p in range(N_MIX_HEADS // 2):
        ka_ref[0, p, :, :LANES] = nn[:, p * LANES:(p + 1) * LANES].astype(BF16)
        ka_ref[0, p, :, LANES:] = kb

    cT = c.T
    brow = lax.broadcasted_iota(jnp.int32, (BIAS_ROWS, tm), 0)
    for h in range(N_MIX_HEADS):
        chi, cmid, clo = _split3(cT[h:h + 1, :])
        sel = (brow == h) | (brow == 16 + h) | (brow == 32 + h)
        tile = jnp.where(brow == 48, chi,
                         jnp.where(brow == 49, cmid,
                                   jnp.where(brow == 50, clo,
                                             jnp.where(sel, 1.0, 0.0))))
        qb_ref[0, h] = tile.astype(BF16)


def _bproj(x3d, b_w_q, kv_w, wf3, bf3, tm, q_scale):
    bsz, seq, _ = x3d.shape
    n_nn = D_MIX + D_MEM + LANES
    return pl.pallas_call(
        functools.partial(_bproj_kernel, tm=tm, q_scale=q_scale),
        grid=(bsz, seq // tm),
        in_specs=[
            pl.BlockSpec((None, tm, D_MODEL), lambda b, j: (b, j, 0)),
            _layer_spec((D_MODEL, D_MODEL), 0),
            _const_spec(kv_w.shape),
            _const_spec((D_MODEL, LANES)),
            _const_spec((1, LANES)),
        ],
        out_specs=[
            pl.BlockSpec((1, D_MIX, tm), lambda b, j: (b, 0, j)),
            pl.BlockSpec((1, N_MIX_HEADS, BIAS_ROWS, tm), lambda b, j: (b, 0, 0, j)),
            pl.BlockSpec((1, N_MIX_HEADS // 2, tm, 2 * LANES), lambda b, j: (b, 0, j, 0)),
            pl.BlockSpec((1, N_MIX_HEADS, V_ROWS, tm), lambda b, j: (b, 0, 0, j)),
            pl.BlockSpec((tm, D_MEM), lambda b, j: (b * (seq // tm) + j, 0)),
        ],
        out_shape=[
            jax.ShapeDtypeStruct((bsz, D_MIX, seq), BF16),
            jax.ShapeDtypeStruct((bsz, N_MIX_HEADS, BIAS_ROWS, seq), BF16),
            jax.ShapeDtypeStruct((bsz, N_MIX_HEADS // 2, seq, 2 * LANES), BF16),
            jax.ShapeDtypeStruct((bsz, N_MIX_HEADS, V_ROWS, seq), BF16),
            jax.ShapeDtypeStruct((bsz * seq, D_MEM), BF16),
        ],
        scratch_shapes=[pltpu.VMEM((2 * D_MIX, D_MODEL), BF16),
                        pltpu.VMEM((D_MODEL, n_nn), BF16),
                        pltpu.VMEM((1, LANES), F32)],
        compiler_params=pltpu.CompilerParams(
            dimension_semantics=("arbitrary", "arbitrary"), vmem_limit_bytes=VMEM_LIMIT),
        name="fox_proj",
    )(x3d, b_w_q, kv_w, wf3, bf3)


def _fox_kernel(qT_ref, qb_ref, ka_ref, vT_ref, o_ref, m_ref, acc_ref, s_ref, *, tq, tk):
    qi = pl.program_id(2)
    n_sub = tq // tk
    n_groups = 2 * n_sub

    srow = lax.broadcasted_iota(jnp.int32, (2 * HEAD_DIM, tk), 0)
    zpad = jnp.zeros((LANES - BIAS_ROWS, tk), BF16)
    qa = []
    for sub in range(n_sub):
        cols = slice(sub * tk, (sub + 1) * tk)
        sl = qT_ref[0, :, cols]
        for hh in range(2):
            keep = (srow < HEAD_DIM) if hh == 0 else (srow >= HEAD_DIM)
            qa.append(jnp.concatenate(
                [jnp.where(keep, sl, jnp.zeros_like(sl)), qb_ref[0, hh, :, cols], zpad], axis=0))

    m_ref[...] = jnp.full(m_ref.shape, NEG_BIG, F32)
    acc_ref[...] = jnp.zeros(acc_ref.shape, F32)

    assert n_sub % 2 == 0
    tri = (lax.broadcasted_iota(jnp.int32, (tk, tk), 0)
           <= lax.broadcasted_iota(jnp.int32, (tk, tk), 1))

    def scores(i, slot, g):
        ks = pl.multiple_of(i * tk, tk)
        s_ref[slot, g] = _dot(ka_ref[0, 0, pl.ds(ks, tk), :], qa[g]).astype(s_ref.dtype)

    def softmax_pv(i, slot, g, masked):
        ks = pl.multiple_of(i * tk, tk)
        s = s_ref[slot, g]
        if masked:
            s = jnp.where(tri, s, NEG_BIG)
        m_prev = m_ref[g]
        m_new = jnp.maximum(m_prev, jnp.max(s, axis=0, keepdims=True).astype(F32))
        m_ref[g] = m_new
        p = jnp.exp2((s - m_new.astype(s.dtype)).astype(BF16))
        pv = _dot(vT_ref[0, g % 2, :, pl.ds(ks, tk)], p)
        acc_ref[g] = jnp.exp2(m_prev - m_new) * acc_ref[g] + pv

    def block_pair(i_next, slot_next, i_cur, slot_cur, g0, diagonal):
        for g in range(g0, n_groups):
            if i_next is not None and g >= g0 + (2 if diagonal else 0):
                scores(i_next, slot_next, g)
            softmax_pv(i_cur, slot_cur, g, diagonal and g < g0 + 2)

    n_off = qi * n_sub
    for g in range(n_groups):
        scores(0, 0, g)

    def body(j, carry):
        for u in range(n_sub):
            block_pair(n_sub * j + u + 1, (u + 1) % 2, n_sub * j + u, u % 2, 0, False)
        return carry

    lax.fori_loop(0, qi, body, 0)
    for dj in range(n_sub):
        nxt = n_off + dj + 1 if dj + 1 < n_sub else None
        block_pair(nxt, (dj + 1) % 2, n_off + dj, dj % 2, 2 * dj, True)

    def normalized(g):
        acc = acc_ref[g]
        return acc[:HEAD_DIM] * (1.0 / acc[HEAD_DIM:HEAD_DIM + 1])

    o = jnp.concatenate(
        [jnp.concatenate([normalized(2 * sub + hh) for sub in range(n_sub)], axis=1)
         for hh in range(2)], axis=0)
    o_ref[0] = o.T.astype(BF16)


def _fox(qT, qb, ka, vT, tq, tk):
    bsz, _, seq = qT.shape
    n_pair = N_MIX_HEADS // 2
    return pl.pallas_call(
        functools.partial(_fox_kernel, tq=tq, tk=tk),
        grid=(bsz, n_pair, seq // tq),
        in_specs=[
            pl.BlockSpec((1, 2 * HEAD_DIM, tq), lambda b, p, i: (b, p, i)),
            pl.BlockSpec((1, 2, BIAS_ROWS, tq), lambda b, p, i: (b, p, 0, i)),
            pl.BlockSpec((1, 1, seq, 2 * LANES), lambda b, p, i: (b, p, 0, 0)),
            pl.BlockSpec((1, 2, V_ROWS, seq), lambda b, p, i: (b, p, 0, 0)),
        ],
        out_specs=pl.BlockSpec((1, tq, LANES), lambda b, p, i: (b, i, p)),
        out_shape=jax.ShapeDtypeStruct((bsz, seq, D_MIX), BF16),
        scratch_shapes=[pltpu.VMEM((2 * tq // tk, 1, tk), F32),
                        pltpu.VMEM((2 * tq // tk, V_ROWS, tk), F32),
                        pltpu.VMEM((2, 2 * tq // tk, tk, tk), BF16)],
        compiler_params=pltpu.CompilerParams(
            dimension_semantics=("arbitrary", "arbitrary", "arbitrary"),
            vmem_limit_bytes=VMEM_LIMIT),
        name="fox_attention",
    )(qT, qb, ka, vT)


def kernel(x, mem, a_w_in, a_sgu_ln_g, a_sgu_ln_b, a_w_s, a_b_s, kv_w, kv_b_f, b_w_q,
           mem_w_kv, w_o, ln_g, ln_b, w_up, w_down):
    bsz, seq, d = x.shape
    t = bsz * seq
    scale = 1.0 / math.sqrt(HEAD_DIM)

    wkT = (jnp.swapaxes(mem_w_kv[:, :, :D_MEM], 1, 2) * scale).astype(BF16)
    wv = mem_w_kv[:, :, D_MEM:].astype(BF16)
    ws_pair = (a_w_s[0].reshape(N_MIX_HEADS // 2, 2, CHUNK, CHUNK)
               .transpose(0, 2, 1, 3).reshape(N_MIX_HEADS // 2, CHUNK, 2 * CHUNK))
    bs_full = jnp.repeat(a_b_s[0].T, HEAD_DIM, axis=1)
    wf = kv_w[:, 2 * D_MIX:]
    zf = jnp.zeros((d, 16 - N_MIX_HEADS), F32)
    wf3 = jnp.concatenate([wf, zf, wf, zf, wf, zf, jnp.zeros((d, LANES - 48), F32)], axis=1)
    zb = jnp.zeros((16 - N_MIX_HEADS,), F32)
    bf3 = jnp.concatenate([kv_b_f, zb, kv_b_f, zb, kv_b_f, zb, jnp.zeros((LANES - 48,), F32)])[None, :]
    row = lambda v: v[None, :]
    ln_g4 = ln_g.reshape(2 * DEPTH, 1, d)
    ln_b4 = ln_b.reshape(2 * DEPTH, 1, d)

    mkT, mvb = _mem_kv(mem, wkT, wv)
    x2d = x.reshape(t, d)

    tm_a = min(1024, seq)
    mix, qm = _sgu(x2d, a_w_in, row(a_sgu_ln_g[0]), row(a_sgu_ln_b[0]), ws_pair, bs_full, tm_a)
    x2d = _tail(mix, qm, x2d, mkT, mvb, w_o, ln_g4, ln_b4, seq, tm_a, 0)
    tm_m = min(1024, seq)
    x2d = _mlp(x2d, w_up, w_down, ln_g4, ln_b4, tm_m,0)

    tm_b = min(1024, seq)
    qT, qb, ka, vT, qm = _bproj(x2d.reshape(bsz, seq, d), b_w_q, kv_w[:, :2 * D_MIX], wf3, bf3, tm_b,
                                scale * LOG2E)
    tq = min(2048, seq)
    att = _fox(qT, qb, ka, vT, tq, min(256, seq)).reshape(t, D_MIX)
    x2d = _tail(att, qm, x2d, mkT, mvb, w_o, ln_g4, ln_b4, seq, tm_a, 1)
    x2d = _mlp(x2d, w_up, w_down, ln_g4, ln_b4, tm_m,1)
    return x2d.reshape(bsz, seq, d)
```

```python
import functools
import math

import jax
import jax.numpy as jnp
import numpy as np
from jax import lax
from jax.experimental import pallas as pl
from jax.experimental.pallas import tpu as pltpu

D_MODEL = 1024
HEAD_DIM = 64
N_MIX_HEADS = 12
N_MEM_HEADS = 4
D_MIX = N_MIX_HEADS * HEAD_DIM
D_MEM = N_MEM_HEADS * HEAD_DIM
CHUNK = 128
N_MEM_TOKENS = 256
D_FF = 4 * D_MODEL
DEPTH = 2
DN_ALPHA = (2 * DEPTH) ** 0.25
LN_EPS = 1e-5

LANES = 128
V_ROWS = 80
BIAS_ROWS = 64
NEG_BIG = -1e30
LOG2E = math.log2(math.e)

F32 = jnp.float32
BF16 = jnp.bfloat16

VMEM_LIMIT = 56 * 1024 * 1024


def _const_spec(shape):
    nd = len(shape)
    return pl.BlockSpec(shape, lambda *_: (0,) * nd, pipeline_mode=pl.Buffered(1))


def _layer_spec(shape, layer):
    nd = len(shape)
    return pl.BlockSpec((None,) + tuple(shape), lambda *_: (layer,) + (0,) * nd,
                        pipeline_mode=pl.Buffered(1))


def _dot(a, b):
    return jnp.dot(a, b, preferred_element_type=F32)


def _dot_nt(a, b):
    return lax.dot_general(a, b, (((1,), (1,)), ((), ())), preferred_element_type=F32)


def _layer_norm(r, g, b):
    mu = jnp.mean(r, axis=-1, keepdims=True)
    d = r - mu
    var = jnp.mean(d * d, axis=-1, keepdims=True)
    return d * lax.rsqrt(var + LN_EPS) * g + b


def _split3(c):
    hi = c.astype(BF16).astype(F32)
    r1 = c - hi
    mid = r1.astype(BF16).astype(F32)
    lo = (r1 - mid).astype(BF16).astype(F32)
    return hi, mid, lo


def _mem_kv_kernel(mem_ref, wkT_ref, wv_ref, mkT_ref, mvb_ref):
    memb = mem_ref[0].astype(BF16)
    kT = _dot_nt(wkT_ref[0], memb)
    kT4 = jnp.concatenate([kT] * N_MEM_HEADS, axis=1)
    row = lax.broadcasted_iota(jnp.int32, kT4.shape, 0)
    col = lax.broadcasted_iota(jnp.int32, kT4.shape, 1)
    same = (row >> 6) == (col >> 8)
    mkT_ref[0, 0] = jnp.where(same, kT4, 0.0).astype(BF16)
    v = _dot(memb, wv_ref[0])
    v4 = jnp.concatenate([v] * N_MEM_HEADS, axis=0)
    row = lax.broadcasted_iota(jnp.int32, v4.shape, 0)
    col = lax.broadcasted_iota(jnp.int32, v4.shape, 1)
    same = (row >> 8) == (col >> 6)
    mvb_ref[0, 0] = jnp.where(same, v4, 0.0).astype(BF16)


def _mem_kv(mem, wkT, wv):
    bsz = mem.shape[0]
    m = N_MEM_TOKENS
    return pl.pallas_call(
        _mem_kv_kernel,
        grid=(DEPTH, bsz),
        in_specs=[
            pl.BlockSpec((1, m, D_MODEL), lambda l, b: (b, 0, 0)),
            pl.BlockSpec((1, D_MEM, D_MODEL), lambda l, b: (l, 0, 0)),
            pl.BlockSpec((1, D_MODEL, D_MEM), lambda l, b: (l, 0, 0)),
        ],
        out_specs=[
            pl.BlockSpec((1, 1, D_MEM, N_MEM_HEADS * m), lambda l, b: (l, b, 0, 0)),
            pl.BlockSpec((1, 1, N_MEM_HEADS * m, D_MEM), lambda l, b: (l, b, 0, 0)),
        ],
        out_shape=[
            jax.ShapeDtypeStruct((DEPTH, bsz, D_MEM, N_MEM_HEADS * m), BF16),
            jax.ShapeDtypeStruct((DEPTH, bsz, N_MEM_HEADS * m, D_MEM), BF16),
        ],
        name="mem_kv",
    )(mem, wkT, wv)


def _sgu_kernel(x_ref, win32_ref, g_ref, b_ref, ws_ref, bs_ref, mix_ref, qm_ref, win_ref, *, tm):
    @pl.when(pl.program_id(0) == 0)
    def _():
        win_ref[...] = win32_ref[...].astype(BF16)

    lane = lax.broadcasted_iota(jnp.int32, (CHUNK, LANES), 1)
    low = lane < HEAD_DIM
    wrow = lax.broadcasted_iota(jnp.int32, (CHUNK, 2 * CHUNK), 0)
    wcol = lax.broadcasted_iota(jnp.int32, (CHUNK, 2 * CHUNK), 1) & (CHUNK - 1)
    causal = wcol <= wrow
    xb = x_ref[...].astype(BF16)
    z = _dot(xb, win_ref[...])
    zu = jax.nn.gelu(z[:, :D_MIX], approximate=True)
    zv = jax.nn.gelu(z[:, D_MIX:2 * D_MIX], approximate=True)
    zv = _layer_norm(zv, g_ref[...], b_ref[...])
    qm_ref[...] = z[:, 2 * D_MIX:].astype(BF16)
    for p in range(N_MIX_HEADS // 2):
        w = jnp.where(causal, ws_ref[p], 0.0).astype(BF16)
        bias = bs_ref[:, p * LANES:(p + 1) * LANES]
        for c0 in range(0, tm // CHUNK, 2):
            rhs = []
            for c in (c0, c0 + 1):
                slab = zv[c * CHUNK:(c + 1) * CHUNK, p * LANES:(p + 1) * LANES]
                rhs.append(jnp.concatenate(
                    [jnp.where(low, slab, 0.0), jnp.where(low, 0.0, slab)], axis=0))
            rhs = jnp.concatenate(rhs, axis=1).astype(BF16)
            mixed = _dot(w, rhs)
            for i, c in enumerate((c0, c0 + 1)):
                u = zu[c * CHUNK:(c + 1) * CHUNK, p * LANES:(p + 1) * LANES]
                out = u * (mixed[:, i * LANES:(i + 1) * LANES] + bias)
                mix_ref[c * CHUNK:(c + 1) * CHUNK, p * LANES:(p + 1) * LANES] = out.astype(BF16)


def _sgu(x2d, w_in, ln_g, ln_b, ws_pair, bs_full, tm):
    t = x2d.shape[0]
    n_in = w_in.shape[-1]
    return pl.pallas_call(
        functools.partial(_sgu_kernel, tm=tm),
        grid=(t // tm,),
        in_specs=[
            pl.BlockSpec((tm, D_MODEL), lambda i: (i, 0)),
            _layer_spec((D_MODEL, n_in), 0),
            _const_spec((1, D_MIX)),
            _const_spec((1, D_MIX)),
            _const_spec((N_MIX_HEADS // 2, CHUNK, 2 * CHUNK)),
            _const_spec((CHUNK, D_MIX)),
        ],
        out_specs=[pl.BlockSpec((tm, D_MIX), lambda i: (i, 0)),
                   pl.BlockSpec((tm, D_MEM), lambda i: (i, 0))],
        out_shape=[jax.ShapeDtypeStruct((t, D_MIX), BF16),
                   jax.ShapeDtypeStruct((t, D_MEM), BF16)],
        scratch_shapes=[pltpu.VMEM((D_MODEL, n_in), BF16)],
        compiler_params=pltpu.CompilerParams(
            dimension_semantics=("arbitrary",), vmem_limit_bytes=VMEM_LIMIT),
        name="sgu_mixer",
    )(x2d, w_in, ln_g, ln_b, ws_pair, bs_full)


def _tail_kernel(mix_ref, qm_ref, x_ref, mkT_ref, mvb_ref, wo32_ref, g_ref, b_ref, o_ref, wo_ref,
                 *, rows):
    @pl.when(pl.program_id(0) == 0)
    def _():
        wo_ref[...] = wo32_ref[...].astype(BF16)

    m = N_MEM_TOKENS
    for r0 in range(0, x_ref.shape[0], rows):
        rs = slice(r0, r0 + rows)
        s = _dot(qm_ref[rs, :], mkT_ref[0])
        ps = []
        for h in range(N_MEM_HEADS):
            sh = s[:, h * m:(h + 1) * m]
            e = jnp.exp(sh - jnp.max(sh, axis=-1, keepdims=True))
            ps.append((e / jnp.sum(e, axis=-1, keepdims=True)).astype(BF16))
        p = jnp.concatenate(ps, axis=1)
        mo = _dot(p, mvb_ref[0]).astype(BF16)
        y = _dot(mix_ref[rs, :], wo_ref[:D_MIX, :]) + _dot(mo, wo_ref[D_MIX:, :])
        r = DN_ALPHA * x_ref[rs, :] + y
        o_ref[rs, :] = _layer_norm(r, g_ref[...], b_ref[...])


def _tail(mix, qm, x2d, mkT, mvb, w_o, ln_g, ln_b, seq, tm, layer):
    t = x2d.shape[0]
    m = N_MEM_TOKENS
    per_b = seq // tm
    return pl.pallas_call(
        functools.partial(_tail_kernel, rows=tm),
        grid=(t // tm,),
        in_specs=[
            pl.BlockSpec((tm, D_MIX), lambda i: (i, 0)),
            pl.BlockSpec((tm, D_MEM), lambda i: (i, 0)),
            pl.BlockSpec((tm, D_MODEL), lambda i: (i, 0)),
            pl.BlockSpec((None, 1, D_MEM, N_MEM_HEADS * m), lambda i: (layer, i // per_b, 0, 0)),
            pl.BlockSpec((None, 1, N_MEM_HEADS * m, D_MEM), lambda i: (layer, i // per_b, 0, 0)),
            _layer_spec((D_MODEL, D_MODEL), layer),
            _layer_spec((1, D_MODEL), 2 * layer),
            _layer_spec((1, D_MODEL), 2 * layer),
        ],
        out_specs=pl.BlockSpec((tm, D_MODEL), lambda i: (i, 0)),
        out_shape=jax.ShapeDtypeStruct((t, D_MODEL), F32),
        scratch_shapes=[pltpu.VMEM((D_MODEL, D_MODEL), BF16)],
        compiler_params=pltpu.CompilerParams(
            dimension_semantics=("arbitrary",), vmem_limit_bytes=VMEM_LIMIT),
        name="mixer_tail",
    )(mix, qm, x2d, mkT, mvb, w_o, ln_g, ln_b)


def _mlp_kernel(x_ref, wu32_ref, wd32_ref, g_ref, b_ref, o_ref, wu_ref, wd_ref, h_ref,
                *, n_chunk, rows, n_w, wc):
    step = pl.program_id(0)

    @pl.when(step < n_w)
    def _():
        c0 = pl.multiple_of(step * wc, wc)
        wu_ref[:, pl.ds(c0, wc)] = wu32_ref[...].astype(BF16)
        wd_ref[pl.ds(c0, wc), :] = wd32_ref[...].astype(BF16)

    @pl.when(step >= n_w - 1)
    def _():
        for r0 in range(0, x_ref.shape[0], rows):
            rs = slice(r0, r0 + rows)
            x = x_ref[rs, :]
            xb = x.astype(BF16)
            for c in range(D_FF // n_chunk):
                h = _dot(xb, wu_ref[:, c * n_chunk:(c + 1) * n_chunk])
                h = jnp.maximum(h, 0.0)
                h_ref[rs, c * n_chunk:(c + 1) * n_chunk] = (h * h).astype(BF16)
            y = _dot(h_ref[rs, :], wd_ref[...])
            r = DN_ALPHA * x + y
            o_ref[rs, :] = _layer_norm(r, g_ref[...], b_ref[...])


def _mlp(x2d, w_up, w_down, ln_g, ln_b, tm, layer, n_chunk=1024, rows=256, wc=1024):
    t = x2d.shape[0]
    rows = min(rows, tm)
    n_w = D_FF // wc
    tile = lambda s: (jnp.maximum(s - (n_w - 1), 0), 0)
    return pl.pallas_call(
        functools.partial(_mlp_kernel, n_chunk=n_chunk, rows=rows, n_w=n_w, wc=wc),
        grid=(n_w - 1 + t // tm,),
        in_specs=[
            pl.BlockSpec((tm, D_MODEL), tile),
            pl.BlockSpec((None, D_MODEL, wc), lambda s: (layer, 0, jnp.minimum(s, n_w - 1))),
            pl.BlockSpec((None, wc, D_MODEL), lambda s: (layer, jnp.minimum(s, n_w - 1), 0)),
            _layer_spec((1, D_MODEL), 2 * layer + 1),
            _layer_spec((1, D_MODEL), 2 * layer + 1),
        ],
        out_specs=pl.BlockSpec((tm, D_MODEL), tile),
        out_shape=jax.ShapeDtypeStruct((t, D_MODEL), F32),
        scratch_shapes=[pltpu.VMEM((D_MODEL, D_FF), BF16),
                        pltpu.VMEM((D_FF, D_MODEL), BF16),
                        pltpu.VMEM((tm, D_FF), BF16)],
        compiler_params=pltpu.CompilerParams(
            dimension_semantics=("arbitrary",), vmem_limit_bytes=VMEM_LIMIT),
        name="relu2_mlp",
    )(x2d, w_up, w_down, ln_g, ln_b)


def _log_sigmoid(x):
    return jnp.minimum(x, 0.0) - jnp.log(1.0 + jnp.exp(-jnp.abs(x)))


def _bproj_kernel(x_ref, wq32_ref, kvwT32_ref, wf3_ref, bf_ref,
                  qT_ref, qb_ref, ka_ref, vT_ref, qm_ref, wnt_ref, wnn_ref, carry_ref, *, tm, q_scale):
    @pl.when((pl.program_id(0) == 0) & (pl.program_id(1) == 0))
    def _():
        wq = wq32_ref[...]
        wnt_ref[:D_MIX, :] = (wq[:, :D_MIX] * q_scale).T.astype(BF16)
        wnt_ref[D_MIX:, :] = kvwT32_ref[D_MIX:2 * D_MIX, :].astype(BF16)
        wnn_ref[:, :D_MIX] = kvwT32_ref[:D_MIX, :].T.astype(BF16)
        wnn_ref[:, D_MIX:D_MIX + D_MEM] = wq[:, D_MIX:].astype(BF16)
        wnn_ref[:, D_MIX + D_MEM:] = wf3_ref[...].astype(BF16)

    @pl.when(pl.program_id(1) == 0)
    def _():
        carry_ref[...] = jnp.zeros_like(carry_ref)

    xb = x_ref[...].astype(BF16)
    nt = _dot_nt(wnt_ref[...], xb)
    qT_ref[0] = nt[:D_MIX].astype(BF16)
    vrow = lax.broadcasted_iota(jnp.int32, (V_ROWS - HEAD_DIM, tm), 0)
    ones_rows = jnp.where(vrow == 0, 1.0, 0.0).astype(BF16)
    for h in range(N_MIX_HEADS):
        vT_ref[0, h, :HEAD_DIM, :] = nt[D_MIX + h * HEAD_DIM:D_MIX + (h + 1) * HEAD_DIM].astype(BF16)
        vT_ref[0, h, HEAD_DIM:, :] = ones_rows

    nn = _dot(xb, wnn_ref[...])
    qm_ref[...] = nn[:, D_MIX:D_MIX + D_MEM].astype(BF16)

    f3 = nn[:, D_MIX + D_MEM:] + bf_ref[...]
    lane = lax.broadcasted_iota(jnp.int32, (tm, LANES), 1)
    valid = (lane & 15) < N_MIX_HEADS
    valid = valid & (lane < 48)
    c = jnp.where(valid, _log_sigmoid(f3) * LOG2E, 0.0)
    rowi = lax.broadcasted_iota(jnp.int32, (tm, LANES), 0)
    d = 1
    while d < tm:
        c = c + jnp.where(rowi >= d, pltpu.roll(c, d, axis=0), 0.0)
        d *= 2
    c = c + carry_ref[...]
    carry_ref[...] = c[tm - 1:tm, :]

    hi, mid, lo = _split3(c)
    kb = jnp.where(lane < 16, -hi,
                   jnp.where(lane < 32, -mid,
                             jnp.where(lane < 48, -lo,
                                       jnp.where(lane < 51, 1.0, 0.0))))
    kb = kb.astype(BF16)
    for p in range(N_MIX_HEADS // 2):
        ka_ref[0, p, :, :LANES] = nn[:, p * LANES:(p + 1) * LANES].astype(BF16)
        ka_ref[0, p, :, LANES:] = kb

    cT = c.T
    brow = lax.broadcasted_iota(jnp.int32, (BIAS_ROWS, tm), 0)
    for h in range(N_MIX_HEADS):
        chi, cmid, clo = _split3(cT[h:h + 1, :])
        sel = (brow == h) | (brow == 16 + h) | (brow == 32 + h)
        tile = jnp.where(brow == 48, chi,
                         jnp.where(brow == 49, cmid,
                                   jnp.where(brow == 50, clo,
                                             jnp.where(sel, 1.0, 0.0))))
        qb_ref[0, h] = tile.astype(BF16)


def _bproj(x3d, b_w_q, kv_wT, wf3, bf3, tm, q_scale):
    bsz, seq, _ = x3d.shape
    n_nn = D_MIX + D_MEM + LANES
    return pl.pallas_call(
        functools.partial(_bproj_kernel, tm=tm, q_scale=q_scale),
        grid=(bsz, seq // tm),
        in_specs=[
            pl.BlockSpec((None, tm, D_MODEL), lambda b, j: (b, j, 0)),
            _layer_spec((D_MODEL, D_MODEL), 0),
            _const_spec(kv_wT.shape),
            _const_spec((D_MODEL, LANES)),
            _const_spec((1, LANES)),
        ],
        out_specs=[
            pl.BlockSpec((1, D_MIX, tm), lambda b, j: (b, 0, j)),
            pl.BlockSpec((1, N_MIX_HEADS, BIAS_ROWS, tm), lambda b, j: (b, 0, 0, j)),
            pl.BlockSpec((1, N_MIX_HEADS // 2, tm, 2 * LANES), lambda b, j: (b, 0, j, 0)),
            pl.BlockSpec((1, N_MIX_HEADS, V_ROWS, tm), lambda b, j: (b, 0, 0, j)),
            pl.BlockSpec((tm, D_MEM), lambda b, j: (b * (seq // tm) + j, 0)),
        ],
        out_shape=[
            jax.ShapeDtypeStruct((bsz, D_MIX, seq), BF16),
            jax.ShapeDtypeStruct((bsz, N_MIX_HEADS, BIAS_ROWS, seq), BF16),
            jax.ShapeDtypeStruct((bsz, N_MIX_HEADS // 2, seq, 2 * LANES), BF16),
            jax.ShapeDtypeStruct((bsz, N_MIX_HEADS, V_ROWS, seq), BF16),
            jax.ShapeDtypeStruct((bsz * seq, D_MEM), BF16),
        ],
        scratch_shapes=[pltpu.VMEM((2 * D_MIX, D_MODEL), BF16),
                        pltpu.VMEM((D_MODEL, n_nn), BF16),
                        pltpu.VMEM((1, LANES), F32)],
        compiler_params=pltpu.CompilerParams(
            dimension_semantics=("arbitrary", "arbitrary"), vmem_limit_bytes=VMEM_LIMIT),
        name="fox_proj",
    )(x3d, b_w_q, kv_wT, wf3, bf3)


def _fox_kernel(qT_ref, qb_ref, ka_ref, vT_ref, o_ref, m_ref, acc_ref, s_ref, *, tq, tk):
    qi = pl.program_id(2)
    n_sub = tq // tk
    n_groups = 2 * n_sub

    srow = lax.broadcasted_iota(jnp.int32, (2 * HEAD_DIM, tk), 0)
    zpad = jnp.zeros((LANES - BIAS_ROWS, tk), BF16)
    qa = []
    for sub in range(n_sub):
        cols = slice(sub * tk, (sub + 1) * tk)
        sl = qT_ref[0, :, cols]
        for hh in range(2):
            keep = (srow < HEAD_DIM) if hh == 0 else (srow >= HEAD_DIM)
            qa.append(jnp.concatenate(
                [jnp.where(keep, sl, jnp.zeros_like(sl)), qb_ref[0, hh, :, cols], zpad], axis=0))

    m_ref[...] = jnp.full(m_ref.shape, NEG_BIG, F32)
    acc_ref[...] = jnp.zeros(acc_ref.shape, F32)

    assert n_sub % 2 == 0
    tri = (lax.broadcasted_iota(jnp.int32, (tk, tk), 0)
           <= lax.broadcasted_iota(jnp.int32, (tk, tk), 1))

    def scores(i, slot, g):
        ks = pl.multiple_of(i * tk, tk)
        s_ref[slot, g] = _dot(ka_ref[0, 0, pl.ds(ks, tk), :], qa[g]).astype(s_ref.dtype)

    def softmax_pv(i, slot, g, masked):
        ks = pl.multiple_of(i * tk, tk)
        s = s_ref[slot, g]
        if masked:
            s = jnp.where(tri, s, NEG_BIG)
        m_prev = m_ref[g]
        m_new = jnp.maximum(m_prev, jnp.max(s, axis=0, keepdims=True).astype(F32))
        m_ref[g] = m_new
        p = jnp.exp2((s - m_new.astype(s.dtype)).astype(BF16))
        pv = _dot(vT_ref[0, g % 2, :, pl.ds(ks, tk)], p)
        acc_ref[g] = jnp.exp2(m_prev - m_new) * acc_ref[g] + pv

    def block_pair(i_next, slot_next, i_cur, slot_cur, g0, diagonal):
        for g in range(g0, n_groups):
            if i_next is not None and g >= g0 + (2 if diagonal else 0):
                scores(i_next, slot_next, g)
            softmax_pv(i_cur, slot_cur, g, diagonal and g < g0 + 2)

    n_off = qi * n_sub
    for g in range(n_groups):
        scores(0, 0, g)

    def body(j, carry):
        for u in range(n_sub):
            block_pair(n_sub * j + u + 1, (u + 1) % 2, n_sub * j + u, u % 2, 0, False)
        return carry

    lax.fori_loop(0, qi, body, 0)
    for dj in range(n_sub):
        nxt = n_off + dj + 1 if dj + 1 < n_sub else None
        block_pair(nxt, (dj + 1) % 2, n_off + dj, dj % 2, 2 * dj, True)

    def normalized(g):
        acc = acc_ref[g]
        return acc[:HEAD_DIM] * (1.0 / acc[HEAD_DIM:HEAD_DIM + 1])

    o = jnp.concatenate(
        [jnp.concatenate([normalized(2 * sub + hh) for sub in range(n_sub)], axis=1)
         for hh in range(2)], axis=0)
    o_ref[0] = o.T.astype(BF16)


def _fox(qT, qb, ka, vT, tq, tk):
    bsz, _, seq = qT.shape
    n_pair = N_MIX_HEADS // 2
    return pl.pallas_call(
        functools.partial(_fox_kernel, tq=tq, tk=tk),
        grid=(bsz, n_pair, seq // tq),
        in_specs=[
            pl.BlockSpec((1, 2 * HEAD_DIM, tq), lambda b, p, i: (b, p, i)),
            pl.BlockSpec((1, 2, BIAS_ROWS, tq), lambda b, p, i: (b, p, 0, i)),
            pl.BlockSpec((1, 1, seq, 2 * LANES), lambda b, p, i: (b, p, 0, 0)),
            pl.BlockSpec((1, 2, V_ROWS, seq), lambda b, p, i: (b, p, 0, 0)),
        ],
        out_specs=pl.BlockSpec((1, tq, LANES), lambda b, p, i: (b, i, p)),
        out_shape=jax.ShapeDtypeStruct((bsz, seq, D_MIX), BF16),
        scratch_shapes=[pltpu.VMEM((2 * tq // tk, 1, tk), F32),
                        pltpu.VMEM((2 * tq // tk, V_ROWS, tk), F32),
                        pltpu.VMEM((2, 2 * tq // tk, tk, tk), BF16)],
        compiler_params=pltpu.CompilerParams(
            dimension_semantics=("arbitrary", "arbitrary", "arbitrary"),
            vmem_limit_bytes=VMEM_LIMIT),
        name="fox_attention",
    )(qT, qb, ka, vT)


def kernel(x, mem, a_w_in, a_sgu_ln_g, a_sgu_ln_b, a_w_s, a_b_s, kv_w, kv_b_f, b_w_q,
           mem_w_kv, w_o, ln_g, ln_b, w_up, w_down):
    bsz, seq, d = x.shape
    t = bsz * seq
    scale = 1.0 / math.sqrt(HEAD_DIM)

    wkT = (jnp.swapaxes(mem_w_kv[:, :, :D_MEM], 1, 2) * scale).astype(BF16)
    wv = mem_w_kv[:, :, D_MEM:].astype(BF16)
    ws_pair = (a_w_s[0].reshape(N_MIX_HEADS // 2, 2, CHUNK, CHUNK)
               .transpose(0, 2, 1, 3).reshape(N_MIX_HEADS // 2, CHUNK, 2 * CHUNK))
    bs_full = jnp.repeat(a_b_s[0].T, HEAD_DIM, axis=1)
    place = np.zeros((N_MIX_HEADS, LANES), np.float32)
    for rep in range(3):
        place[np.arange(N_MIX_HEADS), 16 * rep + np.arange(N_MIX_HEADS)] = 1.0
    place_w = np.concatenate([np.zeros((2 * D_MIX, LANES), np.float32), place], axis=0)
    wf3 = jnp.dot(kv_w, place_w)
    bf3 = jnp.dot(kv_b_f[None, :], place, precision=lax.Precision.HIGHEST)
    row = lambda v: v[None, :]
    ln_g4 = ln_g.reshape(2 * DEPTH, 1, d)
    ln_b4 = ln_b.reshape(2 * DEPTH, 1, d)

    mkT, mvb = _mem_kv(mem, wkT, wv)
    x2d = x.reshape(t, d)

    tm_a = min(1024, seq)
    mix, qm = _sgu(x2d, a_w_in, row(a_sgu_ln_g[0]), row(a_sgu_ln_b[0]), ws_pair, bs_full, tm_a)
    x2d = _tail(mix, qm, x2d, mkT, mvb, w_o, ln_g4, ln_b4, seq, tm_a, 0)
    tm_m = min(1024, seq)
    x2d = _mlp(x2d, w_up, w_down, ln_g4, ln_b4, tm_m,0)

    tm_b = min(1024, seq)
    qT, qb, ka, vT, qm = _bproj(x2d.reshape(bsz, seq, d), b_w_q, kv_w.T, wf3, bf3, tm_b, scale * LOG2E)
    tq = min(2048, seq)
    att = _fox(qT, qb, ka, vT, tq, min(256, seq)).reshape(t, D_MIX)
    x2d = _tail(att, qm, x2d, mkT, mvb, w_o, ln_g4, ln_b4, seq, tm_a, 1)
    x2d = _mlp(x2d, w_up, w_down, ln_g4, ln_b4, tm_m,1)
    return x2d.reshape(bsz, seq, d)
```

```python
import functools
import math

import jax
import jax.numpy as jnp
from jax import lax
from jax.experimental import pallas as pl
from jax.experimental.pallas import tpu as pltpu

D_MODEL = 1024
HEAD_DIM = 64
N_MIX_HEADS = 12
N_MEM_HEADS = 4
D_MIX = N_MIX_HEADS * HEAD_DIM
D_MEM = N_MEM_HEADS * HEAD_DIM
CHUNK = 128
N_MEM_TOKENS = 256
D_FF = 4 * D_MODEL
DEPTH = 2
DN_ALPHA = (2 * DEPTH) ** 0.25
LN_EPS = 1e-5

LANES = 128
V_ROWS = 80
BIAS_ROWS = 64
NEG_BIG = -1e30
LOG2E = math.log2(math.e)

F32 = jnp.float32
BF16 = jnp.bfloat16

VMEM_LIMIT = 56 * 1024 * 1024


def _const_spec(shape):
    nd = len(shape)
    return pl.BlockSpec(shape, lambda *_: (0,) * nd, pipeline_mode=pl.Buffered(1))


def _layer_spec(shape, layer):
    nd = len(shape)
    return pl.BlockSpec((None,) + tuple(shape), lambda *_: (layer,) + (0,) * nd,
                        pipeline_mode=pl.Buffered(1))


def _dot(a, b):
    return jnp.dot(a, b, preferred_element_type=F32)


def _dot_nt(a, b):
    return lax.dot_general(a, b, (((1,), (1,)), ((), ())), preferred_element_type=F32)


def _layer_norm(r, g, b):
    mu = jnp.mean(r, axis=-1, keepdims=True)
    d = r - mu
    var = jnp.mean(d * d, axis=-1, keepdims=True)
    return d * lax.rsqrt(var + LN_EPS) * g + b


def _split3(c):
    hi = c.astype(BF16).astype(F32)
    r1 = c - hi
    mid = r1.astype(BF16).astype(F32)
    lo = (r1 - mid).astype(BF16).astype(F32)
    return hi, mid, lo


def _mem_kv_kernel(mem_ref, wkT_ref, wv_ref, mkT_ref, mvb_ref):
    memb = mem_ref[0].astype(BF16)
    kT = _dot_nt(wkT_ref[0], memb)
    kT4 = jnp.concatenate([kT] * N_MEM_HEADS, axis=1)
    row = lax.broadcasted_iota(jnp.int32, kT4.shape, 0)
    col = lax.broadcasted_iota(jnp.int32, kT4.shape, 1)
    same = (row >> 6) == (col >> 8)
    mkT_ref[0, 0] = jnp.where(same, kT4, 0.0).astype(BF16)
    v = _dot(memb, wv_ref[0])
    v4 = jnp.concatenate([v] * N_MEM_HEADS, axis=0)
    row = lax.broadcasted_iota(jnp.int32, v4.shape, 0)
    col = lax.broadcasted_iota(jnp.int32, v4.shape, 1)
    same = (row >> 8) == (col >> 6)
    mvb_ref[0, 0] = jnp.where(same, v4, 0.0).astype(BF16)


def _mem_kv(mem, wkT, wv):
    bsz = mem.shape[0]
    m = N_MEM_TOKENS
    return pl.pallas_call(
        _mem_kv_kernel,
        grid=(DEPTH, bsz),
        in_specs=[
            pl.BlockSpec((1, m, D_MODEL), lambda l, b: (b, 0, 0)),
            pl.BlockSpec((1, D_MEM, D_MODEL), lambda l, b: (l, 0, 0)),
            pl.BlockSpec((1, D_MODEL, D_MEM), lambda l, b: (l, 0, 0)),
        ],
        out_specs=[
            pl.BlockSpec((1, 1, D_MEM, N_MEM_HEADS * m), lambda l, b: (l, b, 0, 0)),
            pl.BlockSpec((1, 1, N_MEM_HEADS * m, D_MEM), lambda l, b: (l, b, 0, 0)),
        ],
        out_shape=[
            jax.ShapeDtypeStruct((DEPTH, bsz, D_MEM, N_MEM_HEADS * m), BF16),
            jax.ShapeDtypeStruct((DEPTH, bsz, N_MEM_HEADS * m, D_MEM), BF16),
        ],
        name="mem_kv",
    )(mem, wkT, wv)


def _sgu_kernel(x_ref, win32_ref, g_ref, b_ref, ws_ref, bs_ref, mix_ref, qm_ref, win_ref, *, tm):
    @pl.when(pl.program_id(0) == 0)
    def _():
        win_ref[...] = win32_ref[...].astype(BF16)

    lane = lax.broadcasted_iota(jnp.int32, (CHUNK, LANES), 1)
    low = lane < HEAD_DIM
    wrow = lax.broadcasted_iota(jnp.int32, (CHUNK, 2 * CHUNK), 0)
    wcol = lax.broadcasted_iota(jnp.int32, (CHUNK, 2 * CHUNK), 1) & (CHUNK - 1)
    causal = wcol <= wrow
    xb = x_ref[...].astype(BF16)
    z = _dot(xb, win_ref[...])
    zu = jax.nn.gelu(z[:, :D_MIX], approximate=True)
    zv = jax.nn.gelu(z[:, D_MIX:2 * D_MIX], approximate=True)
    zv = _layer_norm(zv, g_ref[...], b_ref[...])
    qm_ref[...] = z[:, 2 * D_MIX:].astype(BF16)
    for p in range(N_MIX_HEADS // 2):
        w = jnp.where(causal, ws_ref[p], 0.0).astype(BF16)
        bias = bs_ref[:, p * LANES:(p + 1) * LANES]
        for c0 in range(0, tm // CHUNK, 2):
            rhs = []
            for c in (c0, c0 + 1):
                slab = zv[c * CHUNK:(c + 1) * CHUNK, p * LANES:(p + 1) * LANES]
                rhs.append(jnp.concatenate(
                    [jnp.where(low, slab, 0.0), jnp.where(low, 0.0, slab)], axis=0))
            rhs = jnp.concatenate(rhs, axis=1).astype(BF16)
            mixed = _dot(w, rhs)
            for i, c in enumerate((c0, c0 + 1)):
                u = zu[c * CHUNK:(c + 1) * CHUNK, p * LANES:(p + 1) * LANES]
                out = u * (mixed[:, i * LANES:(i + 1) * LANES] + bias)
                mix_ref[c * CHUNK:(c + 1) * CHUNK, p * LANES:(p + 1) * LANES] = out.astype(BF16)


def _sgu(x2d, w_in, ln_g, ln_b, ws_pair, bs_full, tm):
    t = x2d.shape[0]
    n_in = w_in.shape[-1]
    return pl.pallas_call(
        functools.partial(_sgu_kernel, tm=tm),
        grid=(t // tm,),
        in_specs=[
            pl.BlockSpec((tm, D_MODEL), lambda i: (i, 0)),
            _layer_spec((D_MODEL, n_in), 0),
            _const_spec((1, D_MIX)),
            _const_spec((1, D_MIX)),
            _const_spec((N_MIX_HEADS // 2, CHUNK, 2 * CHUNK)),
            _const_spec((CHUNK, D_MIX)),
        ],
        out_specs=[pl.BlockSpec((tm, D_MIX), lambda i: (i, 0)),
                   pl.BlockSpec((tm, D_MEM), lambda i: (i, 0))],
        out_shape=[jax.ShapeDtypeStruct((t, D_MIX), BF16),
                   jax.ShapeDtypeStruct((t, D_MEM), BF16)],
        scratch_shapes=[pltpu.VMEM((D_MODEL, n_in), BF16)],
        compiler_params=pltpu.CompilerParams(
            dimension_semantics=("arbitrary",), vmem_limit_bytes=VMEM_LIMIT),
        name="sgu_mixer",
    )(x2d, w_in, ln_g, ln_b, ws_pair, bs_full)


def _tail_kernel(mix_ref, qm_ref, x_ref, mkT_ref, mvb_ref, wo32_ref, g_ref, b_ref, o_ref, wo_ref,
                 *, rows):
    @pl.when(pl.program_id(0) == 0)
    def _():
        wo_ref[...] = wo32_ref[...].astype(BF16)

    m = N_MEM_TOKENS
    for r0 in range(0, x_ref.shape[0], rows):
        rs = slice(r0, r0 + rows)
        s = _dot(qm_ref[rs, :], mkT_ref[0])
        ps = []
        for h in range(N_MEM_HEADS):
            sh = s[:, h * m:(h + 1) * m]
            e = jnp.exp(sh - jnp.max(sh, axis=-1, keepdims=True))
            ps.append((e / jnp.sum(e, axis=-1, keepdims=True)).astype(BF16))
        p = jnp.concatenate(ps, axis=1)
        mo = _dot(p, mvb_ref[0]).astype(BF16)
        y = _dot(mix_ref[rs, :], wo_ref[:D_MIX, :]) + _dot(mo, wo_ref[D_MIX:, :])
        r = DN_ALPHA * x_ref[rs, :] + y
        o_ref[rs, :] = _layer_norm(r, g_ref[...], b_ref[...])


def _tail(mix, qm, x2d, mkT, mvb, w_o, ln_g, ln_b, seq, tm, layer):
    t = x2d.shape[0]
    m = N_MEM_TOKENS
    per_b = seq // tm
    return pl.pallas_call(
        functools.partial(_tail_kernel, rows=tm),
        grid=(t // tm,),
        in_specs=[
            pl.BlockSpec((tm, D_MIX), lambda i: (i, 0)),
            pl.BlockSpec((tm, D_MEM), lambda i: (i, 0)),
            pl.BlockSpec((tm, D_MODEL), lambda i: (i, 0)),
            pl.BlockSpec((None, 1, D_MEM, N_MEM_HEADS * m), lambda i: (layer, i // per_b, 0, 0)),
            pl.BlockSpec((None, 1, N_MEM_HEADS * m, D_MEM), lambda i: (layer, i // per_b, 0, 0)),
            _layer_spec((D_MODEL, D_MODEL), layer),
            _layer_spec((1, D_MODEL), 2 * layer),
            _layer_spec((1, D_MODEL), 2 * layer),
        ],
        out_specs=pl.BlockSpec((tm, D_MODEL), lambda i: (i, 0)),
        out_shape=jax.ShapeDtypeStruct((t, D_MODEL), F32),
        scratch_shapes=[pltpu.VMEM((D_MODEL, D_MODEL), BF16)],
        compiler_params=pltpu.CompilerParams(
            dimension_semantics=("arbitrary",), vmem_limit_bytes=VMEM_LIMIT),
        name="mixer_tail",
    )(mix, qm, x2d, mkT, mvb, w_o, ln_g, ln_b)


def _mlp_kernel(x_ref, wu32_ref, wd32_ref, g_ref, b_ref, o_ref, wu_ref, wd_ref, h_ref,
                *, n_chunk, rows, n_w, wc):
    step = pl.program_id(0)

    @pl.when(step < n_w)
    def _():
        c0 = pl.multiple_of(step * wc, wc)
        wu_ref[:, pl.ds(c0, wc)] = wu32_ref[...].astype(BF16)
        wd_ref[pl.ds(c0, wc), :] = wd32_ref[...].astype(BF16)

    @pl.when(step >= n_w - 1)
    def _():
        for r0 in range(0, x_ref.shape[0], rows):
            rs = slice(r0, r0 + rows)
            x = x_ref[rs, :]
            xb = x.astype(BF16)
            for c in range(D_FF // n_chunk):
                h = _dot(xb, wu_ref[:, c * n_chunk:(c + 1) * n_chunk])
                h = jnp.maximum(h, 0.0)
                h_ref[rs, c * n_chunk:(c + 1) * n_chunk] = (h * h).astype(BF16)
            y = _dot(h_ref[rs, :], wd_ref[...])
            r = DN_ALPHA * x + y
            o_ref[rs, :] = _layer_norm(r, g_ref[...], b_ref[...])


def _mlp(x2d, w_up, w_down, ln_g, ln_b, tm, layer, n_chunk=1024, rows=256, wc=1024):
    t = x2d.shape[0]
    rows = min(rows, tm)
    n_w = D_FF // wc
    tile = lambda s: (jnp.maximum(s - (n_w - 1), 0), 0)
    return pl.pallas_call(
        functools.partial(_mlp_kernel, n_chunk=n_chunk, rows=rows, n_w=n_w, wc=wc),
        grid=(n_w - 1 + t // tm,),
        in_specs=[
            pl.BlockSpec((tm, D_MODEL), tile),
            pl.BlockSpec((None, D_MODEL, wc), lambda s: (layer, 0, jnp.minimum(s, n_w - 1))),
            pl.BlockSpec((None, wc, D_MODEL), lambda s: (layer, jnp.minimum(s, n_w - 1), 0)),
            _layer_spec((1, D_MODEL), 2 * layer + 1),
            _layer_spec((1, D_MODEL), 2 * layer + 1),
        ],
        out_specs=pl.BlockSpec((tm, D_MODEL), tile),
        out_shape=jax.ShapeDtypeStruct((t, D_MODEL), F32),
        scratch_shapes=[pltpu.VMEM((D_MODEL, D_FF), BF16),
                        pltpu.VMEM((D_FF, D_MODEL), BF16),
                        pltpu.VMEM((tm, D_FF), BF16)],
        compiler_params=pltpu.CompilerParams(
            dimension_semantics=("arbitrary",), vmem_limit_bytes=VMEM_LIMIT),
        name="relu2_mlp",
    )(x2d, w_up, w_down, ln_g, ln_b)


def _log_sigmoid(x):
    return jnp.minimum(x, 0.0) - jnp.log(1.0 + jnp.exp(-jnp.abs(x)))


def _bproj_kernel(x_ref, wq32_ref, kvwT32_ref, wf3_ref, bf_ref,
                  qT_ref, qb_ref, ka_ref, vT_ref, qm_ref, wnt_ref, wnn_ref, carry_ref, *, tm, q_scale):
    @pl.when((pl.program_id(0) == 0) & (pl.program_id(1) == 0))
    def _():
        wq = wq32_ref[...]
        wnt_ref[:D_MIX, :] = (wq[:, :D_MIX] * q_scale).T.astype(BF16)
        wnt_ref[D_MIX:, :] = kvwT32_ref[D_MIX:2 * D_MIX, :].astype(BF16)
        wnn_ref[:, :D_MIX] = kvwT32_ref[:D_MIX, :].T.astype(BF16)
        wnn_ref[:, D_MIX:D_MIX + D_MEM] = wq[:, D_MIX:].astype(BF16)
        wnn_ref[:, D_MIX + D_MEM:] = wf3_ref[...].astype(BF16)

    @pl.when(pl.program_id(1) == 0)
    def _():
        carry_ref[...] = jnp.zeros_like(carry_ref)

    xb = x_ref[...].astype(BF16)
    nt = _dot_nt(wnt_ref[...], xb)
    qT_ref[0] = nt[:D_MIX].astype(BF16)
    vrow = lax.broadcasted_iota(jnp.int32, (V_ROWS - HEAD_DIM, tm), 0)
    ones_rows = jnp.where(vrow == 0, 1.0, 0.0).astype(BF16)
    for h in range(N_MIX_HEADS):
        vT_ref[0, h, :HEAD_DIM, :] = nt[D_MIX + h * HEAD_DIM:D_MIX + (h + 1) * HEAD_DIM].astype(BF16)
        vT_ref[0, h, HEAD_DIM:, :] = ones_rows

    nn = _dot(xb, wnn_ref[...])
    qm_ref[...] = nn[:, D_MIX:D_MIX + D_MEM].astype(BF16)

    f3 = nn[:, D_MIX + D_MEM:] + bf_ref[...]
    lane = lax.broadcasted_iota(jnp.int32, (tm, LANES), 1)
    valid = (lane & 15) < N_MIX_HEADS
    valid = valid & (lane < 48)
    c = jnp.where(valid, _log_sigmoid(f3) * LOG2E, 0.0)
    rowi = lax.broadcasted_iota(jnp.int32, (tm, LANES), 0)
    d = 1
    while d < tm:
        c = c + jnp.where(rowi >= d, pltpu.roll(c, d, axis=0), 0.0)
        d *= 2
    c = c + carry_ref[...]
    carry_ref[...] = c[tm - 1:tm, :]

    hi, mid, lo = _split3(c)
    kb = jnp.where(lane < 16, -hi,
                   jnp.where(lane < 32, -mid,
                             jnp.where(lane < 48, -lo,
                                       jnp.where(lane < 51, 1.0, 0.0))))
    kb = kb.astype(BF16)
    for p in range(N_MIX_HEADS // 2):
        ka_ref[0, p, :, :LANES] = nn[:, p * LANES:(p + 1) * LANES].astype(BF16)
        ka_ref[0, p, :, LANES:] = kb

    cT = c.T
    brow = lax.broadcasted_iota(jnp.int32, (BIAS_ROWS, tm), 0)
    for h in range(N_MIX_HEADS):
        chi, cmid, clo = _split3(cT[h:h + 1, :])
        sel = (brow == h) | (brow == 16 + h) | (brow == 32 + h)
        tile = jnp.where(brow == 48, chi,
                         jnp.where(brow == 49, cmid,
                                   jnp.where(brow == 50, clo,
                                             jnp.where(sel, 1.0, 0.0))))
        qb_ref[0, h] = tile.astype(BF16)


def _bproj(x3d, b_w_q, kv_wT, wf3, bf3, tm, q_scale):
    bsz, seq, _ = x3d.shape
    n_nn = D_MIX + D_MEM + LANES
    return pl.pallas_call(
        functools.partial(_bproj_kernel, tm=tm, q_scale=q_scale),
        grid=(bsz, seq // tm),
        in_specs=[
            pl.BlockSpec((None, tm, D_MODEL), lambda b, j: (b, j, 0)),
            _layer_spec((D_MODEL, D_MODEL), 0),
            _const_spec(kv_wT.shape),
            _const_spec((D_MODEL, LANES)),
            _const_spec((1, LANES)),
        ],
        out_specs=[
            pl.BlockSpec((1, D_MIX, tm), lambda b, j: (b, 0, j)),
            pl.BlockSpec((1, N_MIX_HEADS, BIAS_ROWS, tm), lambda b, j: (b, 0, 0, j)),
            pl.BlockSpec((1, N_MIX_HEADS // 2, tm, 2 * LANES), lambda b, j: (b, 0, j, 0)),
            pl.BlockSpec((1, N_MIX_HEADS, V_ROWS, tm), lambda b, j: (b, 0, 0, j)),
            pl.BlockSpec((tm, D_MEM), lambda b, j: (b * (seq // tm) + j, 0)),
        ],
        out_shape=[
            jax.ShapeDtypeStruct((bsz, D_MIX, seq), BF16),
            jax.ShapeDtypeStruct((bsz, N_MIX_HEADS, BIAS_ROWS, seq), BF16),
            jax.ShapeDtypeStruct((bsz, N_MIX_HEADS // 2, seq, 2 * LANES), BF16),
            jax.ShapeDtypeStruct((bsz, N_MIX_HEADS, V_ROWS, seq), BF16),
            jax.ShapeDtypeStruct((bsz * seq, D_MEM), BF16),
        ],
        scratch_shapes=[pltpu.VMEM((2 * D_MIX, D_MODEL), BF16),
                        pltpu.VMEM((D_MODEL, n_nn), BF16),
                        pltpu.VMEM((1, LANES), F32)],
        compiler_params=pltpu.CompilerParams(
            dimension_semantics=("arbitrary", "arbitrary"), vmem_limit_bytes=VMEM_LIMIT),
        name="fox_proj",
    )(x3d, b_w_q, kv_wT, wf3, bf3)


def _fox_kernel(qT_ref, qb_ref, ka_ref, vT_ref, o_ref, m_ref, acc_ref, s_ref, *, tq, tk):
    qi = pl.program_id(2)
    n_sub = tq // tk
    n_groups = 2 * n_sub

    srow = lax.broadcasted_iota(jnp.int32, (2 * HEAD_DIM, tk), 0)
    zpad = jnp.zeros((LANES - BIAS_ROWS, tk), BF16)
    qa = []
    for sub in range(n_sub):
        cols = slice(sub * tk, (sub + 1) * tk)
        sl = qT_ref[0, :, cols]
        for hh in range(2):
            keep = (srow < HEAD_DIM) if hh == 0 else (srow >= HEAD_DIM)
            qa.append(jnp.concatenate(
                [jnp.where(keep, sl, jnp.zeros_like(sl)), qb_ref[0, hh, :, cols], zpad], axis=0))

    m_ref[...] = jnp.full(m_ref.shape, NEG_BIG, F32)
    acc_ref[...] = jnp.zeros(acc_ref.shape, F32)

    assert n_sub % 2 == 0
    tri = (lax.broadcasted_iota(jnp.int32, (tk, tk), 0)
           <= lax.broadcasted_iota(jnp.int32, (tk, tk), 1))

    def scores(i, slot, g):
        ks = pl.multiple_of(i * tk, tk)
        s_ref[slot, g] = _dot(ka_ref[0, 0, pl.ds(ks, tk), :], qa[g]).astype(s_ref.dtype)

    def softmax_pv(i, slot, g, masked):
        ks = pl.multiple_of(i * tk, tk)
        s = s_ref[slot, g]
        if masked:
            s = jnp.where(tri, s, NEG_BIG)
        m_prev = m_ref[g]
        m_new = jnp.maximum(m_prev, jnp.max(s, axis=0, keepdims=True).astype(F32))
        m_ref[g] = m_new
        p = jnp.exp2((s - m_new.astype(s.dtype)).astype(BF16))
        pv = _dot(vT_ref[0, g % 2, :, pl.ds(ks, tk)], p)
        acc_ref[g] = jnp.exp2(m_prev - m_new) * acc_ref[g] + pv

    def block_pair(i_next, slot_next, i_cur, slot_cur, g0, diagonal):
        for g in range(g0, n_groups):
            if i_next is not None and g >= g0 + (2 if diagonal else 0):
                scores(i_next, slot_next, g)
            softmax_pv(i_cur, slot_cur, g, diagonal and g < g0 + 2)

    n_off = qi * n_sub
    for g in range(n_groups):
        scores(0, 0, g)

    def body(j, carry):
        for u in range(n_sub):
            block_pair(n_sub * j + u + 1, (u + 1) % 2, n_sub * j + u, u % 2, 0, False)
        return carry

    lax.fori_loop(0, qi, body, 0)
    for dj in range(n_sub):
        nxt = n_off + dj + 1 if dj + 1 < n_sub else None
        block_pair(nxt, (dj + 1) % 2, n_off + dj, dj % 2, 2 * dj, True)

    def normalized(g):
        acc = acc_ref[g]
        return acc[:HEAD_DIM] * (1.0 / acc[HEAD_DIM:HEAD_DIM + 1])

    o = jnp.concatenate(
        [jnp.concatenate([normalized(2 * sub + hh) for sub in range(n_sub)], axis=1)
         for hh in range(2)], axis=0)
    o_ref[0] = o.T.astype(BF16)


def _fox(qT, qb, ka, vT, tq, tk):
    bsz, _, seq = qT.shape
    n_pair = N_MIX_HEADS // 2
    return pl.pallas_call(
        functools.partial(_fox_kernel, tq=tq, tk=tk),
        grid=(bsz, n_pair, seq // tq),
        in_specs=[
            pl.BlockSpec((1, 2 * HEAD_DIM, tq), lambda b, p, i: (b, p, i)),
            pl.BlockSpec((1, 2, BIAS_ROWS, tq), lambda b, p, i: (b, p, 0, i)),
            pl.BlockSpec((1, 1, seq, 2 * LANES), lambda b, p, i: (b, p, 0, 0)),
            pl.BlockSpec((1, 2, V_ROWS, seq), lambda b, p, i: (b, p, 0, 0)),
        ],
        out_specs=pl.BlockSpec((1, tq, LANES), lambda b, p, i: (b, i, p)),
        out_shape=jax.ShapeDtypeStruct((bsz, seq, D_MIX), BF16),
        scratch_shapes=[pltpu.VMEM((2 * tq // tk, 1, tk), F32),
                        pltpu.VMEM((2 * tq // tk, V_ROWS, tk), F32),
                        pltpu.VMEM((2, 2 * tq // tk, tk, tk), BF16)],
        compiler_params=pltpu.CompilerParams(
            dimension_semantics=("arbitrary", "arbitrary", "arbitrary"),
            vmem_limit_bytes=VMEM_LIMIT),
        name="fox_attention",
    )(qT, qb, ka, vT)


def kernel(x, mem, a_w_in, a_sgu_ln_g, a_sgu_ln_b, a_w_s, a_b_s, kv_w, kv_b_f, b_w_q,
           mem_w_kv, w_o, ln_g, ln_b, w_up, w_down):
    bsz, seq, d = x.shape
    t = bsz * seq
    scale = 1.0 / math.sqrt(HEAD_DIM)

    wkT = (jnp.swapaxes(mem_w_kv[:, :, :D_MEM], 1, 2) * scale).astype(BF16)
    wv = mem_w_kv[:, :, D_MEM:].astype(BF16)
    ws_pair = (a_w_s[0].reshape(N_MIX_HEADS // 2, 2, CHUNK, CHUNK)
               .transpose(0, 2, 1, 3).reshape(N_MIX_HEADS // 2, CHUNK, 2 * CHUNK))
    bs_full = jnp.repeat(a_b_s[0].T, HEAD_DIM, axis=1)
    wf = kv_w[:, 2 * D_MIX:]
    zf = jnp.zeros((d, 16 - N_MIX_HEADS), F32)
    wf3 = jnp.concatenate([wf, zf, wf, zf, wf, zf, jnp.zeros((d, LANES - 48), F32)], axis=1)
    zb = jnp.zeros((16 - N_MIX_HEADS,), F32)
    bf3 = jnp.concatenate([kv_b_f, zb, kv_b_f, zb, kv_b_f, zb, jnp.zeros((LANES - 48,), F32)])[None, :]
    row = lambda v: v[None, :]
    ln_g4 = ln_g.reshape(2 * DEPTH, 1, d)
    ln_b4 = ln_b.reshape(2 * DEPTH, 1, d)

    mkT, mvb = _mem_kv(mem, wkT, wv)
    x2d = x.reshape(t, d)

    tm_a = min(1024, seq)
    mix, qm = _sgu(x2d, a_w_in, row(a_sgu_ln_g[0]), row(a_sgu_ln_b[0]), ws_pair, bs_full, tm_a)
    x2d = _tail(mix, qm, x2d, mkT, mvb, w_o, ln_g4, ln_b4, seq, tm_a, 0)
    tm_m = min(1024, seq)
    x2d = _mlp(x2d, w_up, w_down, ln_g4, ln_b4, tm_m,0)

    tm_b = min(1024, seq)
    qT, qb, ka, vT, qm = _bproj(x2d.reshape(bsz, seq, d), b_w_q, kv_w.T, wf3, bf3, tm_b, scale * LOG2E)
    tq = min(2048, seq)
    att = _fox(qT, qb, ka, vT, tq, min(256, seq)).reshape(t, D_MIX)
    x2d = _tail(att, qm, x2d, mkT, mvb, w_o, ln_g4, ln_b4, seq, tm_a, 1)
    x2d = _mlp(x2d, w_up, w_down, ln_g4, ln_b4, tm_m,1)
    return x2d.reshape(bsz, seq, d)
```

```python
import functools
import math

import jax
import jax.numpy as jnp
from jax import lax
from jax.experimental import pallas as pl
from jax.experimental.pallas import tpu as pltpu

D_MODEL = 1024
HEAD_DIM = 64
N_MIX_HEADS = 12
N_MEM_HEADS = 4
D_MIX = N_MIX_HEADS * HEAD_DIM
D_MEM = N_MEM_HEADS * HEAD_DIM
CHUNK = 128
N_MEM_TOKENS = 256
D_FF = 4 * D_MODEL
DEPTH = 2
DN_ALPHA = (2 * DEPTH) ** 0.25
LN_EPS = 1e-5

LANES = 128
V_ROWS = 80
BIAS_ROWS = 64
NEG_BIG = -1e30
LOG2E = math.log2(math.e)

F32 = jnp.float32
BF16 = jnp.bfloat16

VMEM_LIMIT = 56 * 1024 * 1024
TOKEN_TILE = 1024
MLP_ROWS = 256
MLP_SLAB = 1024
FOX_TQ = 2048
FOX_TK = 256


def _const_spec(shape):
    nd = len(shape)
    return pl.BlockSpec(shape, lambda *_: (0,) * nd, pipeline_mode=pl.Buffered(1))


def _layer_spec(shape, layer):
    nd = len(shape)
    return pl.BlockSpec((None,) + tuple(shape), lambda *_: (layer,) + (0,) * nd,
                        pipeline_mode=pl.Buffered(1))


def _dot(a, b):
    return jnp.dot(a, b, preferred_element_type=F32)


def _dot_nt(a, b):
    return lax.dot_general(a, b, (((1,), (1,)), ((), ())), preferred_element_type=F32)


def _layer_norm(r, g, b):
    mu = jnp.mean(r, axis=-1, keepdims=True)
    d = r - mu
    var = jnp.mean(d * d, axis=-1, keepdims=True)
    return d * lax.rsqrt(var + LN_EPS) * g + b


def _split3(c):
    hi = c.astype(BF16).astype(F32)
    r1 = c - hi
    mid = r1.astype(BF16).astype(F32)
    lo = (r1 - mid).astype(BF16).astype(F32)
    return hi, mid, lo


def _mem_kv_kernel(mem_ref, wkT_ref, wv_ref, mkT_ref, mvb_ref):
    memb = mem_ref[0].astype(BF16)
    kT = _dot_nt(wkT_ref[0], memb)
    kT4 = jnp.concatenate([kT] * N_MEM_HEADS, axis=1)
    row = lax.broadcasted_iota(jnp.int32, kT4.shape, 0)
    col = lax.broadcasted_iota(jnp.int32, kT4.shape, 1)
    same = (row >> 6) == (col >> 8)
    mkT_ref[0, 0] = jnp.where(same, kT4, 0.0).astype(BF16)
    v = _dot(memb, wv_ref[0])
    v4 = jnp.concatenate([v] * N_MEM_HEADS, axis=0)
    row = lax.broadcasted_iota(jnp.int32, v4.shape, 0)
    col = lax.broadcasted_iota(jnp.int32, v4.shape, 1)
    same = (row >> 8) == (col >> 6)
    mvb_ref[0, 0] = jnp.where(same, v4, 0.0).astype(BF16)


def _mem_kv(mem, wkT, wv):
    bsz = mem.shape[0]
    m = N_MEM_TOKENS
    return pl.pallas_call(
        _mem_kv_kernel,
        grid=(DEPTH, bsz),
        in_specs=[
            pl.BlockSpec((1, m, D_MODEL), lambda l, b: (b, 0, 0)),
            pl.BlockSpec((1, D_MEM, D_MODEL), lambda l, b: (l, 0, 0)),
            pl.BlockSpec((1, D_MODEL, D_MEM), lambda l, b: (l, 0, 0)),
        ],
        out_specs=[
            pl.BlockSpec((1, 1, D_MEM, N_MEM_HEADS * m), lambda l, b: (l, b, 0, 0)),
            pl.BlockSpec((1, 1, N_MEM_HEADS * m, D_MEM), lambda l, b: (l, b, 0, 0)),
        ],
        out_shape=[
            jax.ShapeDtypeStruct((DEPTH, bsz, D_MEM, N_MEM_HEADS * m), BF16),
            jax.ShapeDtypeStruct((DEPTH, bsz, N_MEM_HEADS * m, D_MEM), BF16),
        ],
        name="mem_kv",
    )(mem, wkT, wv)


def _sgu_kernel(x_ref, win32_ref, g_ref, b_ref, ws_ref, bs_ref, mix_ref, qm_ref, win_ref, *, tm):
    @pl.when(pl.program_id(0) == 0)
    def _():
        win_ref[...] = win32_ref[...].astype(BF16)

    lane = lax.broadcasted_iota(jnp.int32, (CHUNK, LANES), 1)
    low = lane < HEAD_DIM
    wrow = lax.broadcasted_iota(jnp.int32, (CHUNK, 2 * CHUNK), 0)
    wcol = lax.broadcasted_iota(jnp.int32, (CHUNK, 2 * CHUNK), 1) & (CHUNK - 1)
    causal = wcol <= wrow
    xb = x_ref[...].astype(BF16)
    z = _dot(xb, win_ref[...])
    zu = jax.nn.gelu(z[:, :D_MIX], approximate=True)
    zv = jax.nn.gelu(z[:, D_MIX:2 * D_MIX], approximate=True)
    zv = _layer_norm(zv, g_ref[...], b_ref[...])
    qm_ref[...] = z[:, 2 * D_MIX:].astype(BF16)
    for p in range(N_MIX_HEADS // 2):
        w = jnp.where(causal, ws_ref[p], 0.0).astype(BF16)
        bias = bs_ref[:, p * LANES:(p + 1) * LANES]
        for c0 in range(0, tm // CHUNK, 2):
            rhs = []
            for c in (c0, c0 + 1):
                slab = zv[c * CHUNK:(c + 1) * CHUNK, p * LANES:(p + 1) * LANES]
                rhs.append(jnp.concatenate(
                    [jnp.where(low, slab, 0.0), jnp.where(low, 0.0, slab)], axis=0))
            rhs = jnp.concatenate(rhs, axis=1).astype(BF16)
            mixed = _dot(w, rhs)
            for i, c in enumerate((c0, c0 + 1)):
                u = zu[c * CHUNK:(c + 1) * CHUNK, p * LANES:(p + 1) * LANES]
                out = u * (mixed[:, i * LANES:(i + 1) * LANES] + bias)
                mix_ref[c * CHUNK:(c + 1) * CHUNK, p * LANES:(p + 1) * LANES] = out.astype(BF16)


def _sgu(x2d, w_in, ln_g, ln_b, ws_pair, bs_full, tm):
    t = x2d.shape[0]
    n_in = w_in.shape[-1]
    return pl.pallas_call(
        functools.partial(_sgu_kernel, tm=tm),
        grid=(t // tm,),
        in_specs=[
            pl.BlockSpec((tm, D_MODEL), lambda i: (i, 0)),
            _layer_spec((D_MODEL, n_in), 0),
            _const_spec((1, D_MIX)),
            _const_spec((1, D_MIX)),
            _const_spec((N_MIX_HEADS // 2, CHUNK, 2 * CHUNK)),
            _const_spec((CHUNK, D_MIX)),
        ],
        out_specs=[pl.BlockSpec((tm, D_MIX), lambda i: (i, 0)),
                   pl.BlockSpec((tm, D_MEM), lambda i: (i, 0))],
        out_shape=[jax.ShapeDtypeStruct((t, D_MIX), BF16),
                   jax.ShapeDtypeStruct((t, D_MEM), BF16)],
        scratch_shapes=[pltpu.VMEM((D_MODEL, n_in), BF16)],
        compiler_params=pltpu.CompilerParams(
            dimension_semantics=("arbitrary",), vmem_limit_bytes=VMEM_LIMIT),
        name="sgu_mixer",
    )(x2d, w_in, ln_g, ln_b, ws_pair, bs_full)


def _tail_kernel(mix_ref, qm_ref, x_ref, mkT_ref, mvb_ref, wo32_ref, g_ref, b_ref, o_ref, wo_ref):
    @pl.when(pl.program_id(0) == 0)
    def _():
        wo_ref[...] = wo32_ref[...].astype(BF16)

    m = N_MEM_TOKENS
    s = _dot(qm_ref[...], mkT_ref[0])
    ps = []
    for h in range(N_MEM_HEADS):
        sh = s[:, h * m:(h + 1) * m]
        e = jnp.exp(sh - jnp.max(sh, axis=-1, keepdims=True))
        ps.append((e / jnp.sum(e, axis=-1, keepdims=True)).astype(BF16))
    p = jnp.concatenate(ps, axis=1)
    mo = _dot(p, mvb_ref[0]).astype(BF16)
    y = _dot(mix_ref[...], wo_ref[:D_MIX, :]) + _dot(mo, wo_ref[D_MIX:, :])
    r = DN_ALPHA * x_ref[...] + y
    o_ref[...] = _layer_norm(r, g_ref[...], b_ref[...])


def _tail(mix, qm, x2d, mkT, mvb, w_o, ln_g, ln_b, seq, tm, layer):
    t = x2d.shape[0]
    m = N_MEM_TOKENS
    per_b = seq // tm
    return pl.pallas_call(
        _tail_kernel,
        grid=(t // tm,),
        in_specs=[
            pl.BlockSpec((tm, D_MIX), lambda i: (i, 0)),
            pl.BlockSpec((tm, D_MEM), lambda i: (i, 0)),
            pl.BlockSpec((tm, D_MODEL), lambda i: (i, 0)),
            pl.BlockSpec((None, 1, D_MEM, N_MEM_HEADS * m), lambda i: (layer, i // per_b, 0, 0)),
            pl.BlockSpec((None, 1, N_MEM_HEADS * m, D_MEM), lambda i: (layer, i // per_b, 0, 0)),
            _layer_spec((D_MODEL, D_MODEL), layer),
            _layer_spec((1, D_MODEL), 2 * layer),
            _layer_spec((1, D_MODEL), 2 * layer),
        ],
        out_specs=pl.BlockSpec((tm, D_MODEL), lambda i: (i, 0)),
        out_shape=jax.ShapeDtypeStruct((t, D_MODEL), F32),
        scratch_shapes=[pltpu.VMEM((D_MODEL, D_MODEL), BF16)],
        compiler_params=pltpu.CompilerParams(
            dimension_semantics=("arbitrary",), vmem_limit_bytes=VMEM_LIMIT),
        name="mixer_tail",
    )(mix, qm, x2d, mkT, mvb, w_o, ln_g, ln_b)


def _mlp_kernel(x_ref, wu32_ref, wd32_ref, g_ref, b_ref, o_ref, wu_ref, wd_ref, h_ref,
                *, n_chunk, rows, n_w, wc):
    step = pl.program_id(0)

    @pl.when(step < n_w)
    def _():
        c0 = pl.multiple_of(step * wc, wc)
        wu_ref[:, pl.ds(c0, wc)] = wu32_ref[...].astype(BF16)
        wd_ref[pl.ds(c0, wc), :] = wd32_ref[...].astype(BF16)

    @pl.when(step >= n_w - 1)
    def _():
        for r0 in range(0, x_ref.shape[0], rows):
            rs = slice(r0, r0 + rows)
            x = x_ref[rs, :]
            xb = x.astype(BF16)
            for c in range(D_FF // n_chunk):
                h = _dot(xb, wu_ref[:, c * n_chunk:(c + 1) * n_chunk])
                h = jnp.maximum(h, 0.0)
                h_ref[rs, c * n_chunk:(c + 1) * n_chunk] = (h * h).astype(BF16)
            y = _dot(h_ref[rs, :], wd_ref[...])
            r = DN_ALPHA * x + y
            o_ref[rs, :] = _layer_norm(r, g_ref[...], b_ref[...])


def _mlp(x2d, w_up, w_down, ln_g, ln_b, tm, layer, n_chunk=1024, rows=MLP_ROWS, wc=MLP_SLAB):
    t = x2d.shape[0]
    rows = min(rows, tm)
    n_w = D_FF // wc
    tile = lambda s: (jnp.maximum(s - (n_w - 1), 0), 0)
    return pl.pallas_call(
        functools.partial(_mlp_kernel, n_chunk=n_chunk, rows=rows, n_w=n_w, wc=wc),
        grid=(n_w - 1 + t // tm,),
        in_specs=[
            pl.BlockSpec((tm, D_MODEL), tile),
            pl.BlockSpec((None, D_MODEL, wc), lambda s: (layer, 0, jnp.minimum(s, n_w - 1))),
            pl.BlockSpec((None, wc, D_MODEL), lambda s: (layer, jnp.minimum(s, n_w - 1), 0)),
            _layer_spec((1, D_MODEL), 2 * layer + 1),
            _layer_spec((1, D_MODEL), 2 * layer + 1),
        ],
        out_specs=pl.BlockSpec((tm, D_MODEL), tile),
        out_shape=jax.ShapeDtypeStruct((t, D_MODEL), F32),
        scratch_shapes=[pltpu.VMEM((D_MODEL, D_FF), BF16),
                        pltpu.VMEM((D_FF, D_MODEL), BF16),
                        pltpu.VMEM((tm, D_FF), BF16)],
        compiler_params=pltpu.CompilerParams(
            dimension_semantics=("arbitrary",), vmem_limit_bytes=VMEM_LIMIT),
        name="relu2_mlp",
    )(x2d, w_up, w_down, ln_g, ln_b)


def _log_sigmoid(x):
    return jnp.minimum(x, 0.0) - jnp.log(1.0 + jnp.exp(-jnp.abs(x)))


def _bproj_kernel(x_ref, wq32_ref, kvwT32_ref, wf3_ref, bf_ref,
                  qT_ref, qb_ref, ka_ref, vT_ref, qm_ref, wnt_ref, wnn_ref, carry_ref, *, tm, q_scale):
    @pl.when((pl.program_id(0) == 0) & (pl.program_id(1) == 0))
    def _():
        wq = wq32_ref[...]
        wnt_ref[:D_MIX, :] = (wq[:, :D_MIX] * q_scale).T.astype(BF16)
        wnt_ref[D_MIX:, :] = kvwT32_ref[D_MIX:2 * D_MIX, :].astype(BF16)
        wnn_ref[:, :D_MIX] = kvwT32_ref[:D_MIX, :].T.astype(BF16)
        wnn_ref[:, D_MIX:D_MIX + D_MEM] = wq[:, D_MIX:].astype(BF16)
        wnn_ref[:, D_MIX + D_MEM:] = wf3_ref[...].astype(BF16)

    @pl.when(pl.program_id(1) == 0)
    def _():
        carry_ref[...] = jnp.zeros_like(carry_ref)

    xb = x_ref[...].astype(BF16)
    nt = _dot_nt(wnt_ref[...], xb)
    qT_ref[0] = nt[:D_MIX].astype(BF16)
    vrow = lax.broadcasted_iota(jnp.int32, (V_ROWS - HEAD_DIM, tm), 0)
    ones_rows = jnp.where(vrow == 0, 1.0, 0.0).astype(BF16)
    for h in range(N_MIX_HEADS):
        vT_ref[0, h, :HEAD_DIM, :] = nt[D_MIX + h * HEAD_DIM:D_MIX + (h + 1) * HEAD_DIM].astype(BF16)
        vT_ref[0, h, HEAD_DIM:, :] = ones_rows

    nn = _dot(xb, wnn_ref[...])
    qm_ref[...] = nn[:, D_MIX:D_MIX + D_MEM].astype(BF16)

    f3 = nn[:, D_MIX + D_MEM:] + bf_ref[...]
    lane = lax.broadcasted_iota(jnp.int32, (tm, LANES), 1)
    valid = (lane & 15) < N_MIX_HEADS
    valid = valid & (lane < 48)
    c = jnp.where(valid, _log_sigmoid(f3) * LOG2E, 0.0)
    rowi = lax.broadcasted_iota(jnp.int32, (tm, LANES), 0)
    d = 1
    while d < tm:
        c = c + jnp.where(rowi >= d, pltpu.roll(c, d, axis=0), 0.0)
        d *= 2
    c = c + carry_ref[...]
    carry_ref[...] = c[tm - 1:tm, :]

    hi, mid, lo = _split3(c)
    kb = jnp.where(lane < 16, -hi,
                   jnp.where(lane < 32, -mid,
                             jnp.where(lane < 48, -lo,
                                       jnp.where(lane < 51, 1.0, 0.0))))
    kb = kb.astype(BF16)
    for p in range(N_MIX_HEADS // 2):
        ka_ref[0, p, :, :LANES] = nn[:, p * LANES:(p + 1) * LANES].astype(BF16)
        ka_ref[0, p, :, LANES:] = kb

    cT = c.T
    brow = lax.broadcasted_iota(jnp.int32, (BIAS_ROWS, tm), 0)
    for h in range(N_MIX_HEADS):
        chi, cmid, clo = _split3(cT[h:h + 1, :])
        sel = (brow == h) | (brow == 16 + h) | (brow == 32 + h)
        tile = jnp.where(brow == 48, chi,
                         jnp.where(brow == 49, cmid,
                                   jnp.where(brow == 50, clo,
                                             jnp.where(sel, 1.0, 0.0))))
        qb_ref[0, h] = tile.astype(BF16)


def _bproj(x3d, b_w_q, kv_wT, wf3, bf3, tm, q_scale):
    bsz, seq, _ = x3d.shape
    n_nn = D_MIX + D_MEM + LANES
    return pl.pallas_call(
        functools.partial(_bproj_kernel, tm=tm, q_scale=q_scale),
        grid=(bsz, seq // tm),
        in_specs=[
            pl.BlockSpec((None, tm, D_MODEL), lambda b, j: (b, j, 0)),
            _layer_spec((D_MODEL, D_MODEL), 0),
            _const_spec(kv_wT.shape),
            _const_spec((D_MODEL, LANES)),
            _const_spec((1, LANES)),
        ],
        out_specs=[
            pl.BlockSpec((1, D_MIX, tm), lambda b, j: (b, 0, j)),
            pl.BlockSpec((1, N_MIX_HEADS, BIAS_ROWS, tm), lambda b, j: (b, 0, 0, j)),
            pl.BlockSpec((1, N_MIX_HEADS // 2, tm, 2 * LANES), lambda b, j: (b, 0, j, 0)),
            pl.BlockSpec((1, N_MIX_HEADS, V_ROWS, tm), lambda b, j: (b, 0, 0, j)),
            pl.BlockSpec((tm, D_MEM), lambda b, j: (b * (seq // tm) + j, 0)),
        ],
        out_shape=[
            jax.ShapeDtypeStruct((bsz, D_MIX, seq), BF16),
            jax.ShapeDtypeStruct((bsz, N_MIX_HEADS, BIAS_ROWS, seq), BF16),
            jax.ShapeDtypeStruct((bsz, N_MIX_HEADS // 2, seq, 2 * LANES), BF16),
            jax.ShapeDtypeStruct((bsz, N_MIX_HEADS, V_ROWS, seq), BF16),
            jax.ShapeDtypeStruct((bsz * seq, D_MEM), BF16),
        ],
        scratch_shapes=[pltpu.VMEM((2 * D_MIX, D_MODEL), BF16),
                        pltpu.VMEM((D_MODEL, n_nn), BF16),
                        pltpu.VMEM((1, LANES), F32)],
        compiler_params=pltpu.CompilerParams(
            dimension_semantics=("arbitrary", "arbitrary"), vmem_limit_bytes=VMEM_LIMIT),
        name="fox_proj",
    )(x3d, b_w_q, kv_wT, wf3, bf3)


def _fox_kernel(qT_ref, qb_ref, ka_ref, vT_ref, o_ref, m_ref, acc_ref, s_ref, *, tq, tk):
    qi = pl.program_id(2)
    n_sub = tq // tk
    n_groups = 2 * n_sub

    srow = lax.broadcasted_iota(jnp.int32, (2 * HEAD_DIM, tk), 0)
    zpad = jnp.zeros((LANES - BIAS_ROWS, tk), BF16)
    qa = []
    for sub in range(n_sub):
        cols = slice(sub * tk, (sub + 1) * tk)
        sl = qT_ref[0, :, cols]
        for hh in range(2):
            keep = (srow < HEAD_DIM) if hh == 0 else (srow >= HEAD_DIM)
            qa.append(jnp.concatenate(
                [jnp.where(keep, sl, jnp.zeros_like(sl)), qb_ref[0, hh, :, cols], zpad], axis=0))

    m_ref[...] = jnp.full(m_ref.shape, NEG_BIG, F32)
    acc_ref[...] = jnp.zeros(acc_ref.shape, F32)

    assert n_sub % 2 == 0
    tri = (lax.broadcasted_iota(jnp.int32, (tk, tk), 0)
           <= lax.broadcasted_iota(jnp.int32, (tk, tk), 1))

    def scores(i, slot, g):
        ks = pl.multiple_of(i * tk, tk)
        s_ref[slot, g] = _dot(ka_ref[0, 0, pl.ds(ks, tk), :], qa[g]).astype(s_ref.dtype)

    def softmax_pv(i, slot, g, masked):
        ks = pl.multiple_of(i * tk, tk)
        s = s_ref[slot, g]
        if masked:
            s = jnp.where(tri, s, NEG_BIG)
        m_prev = m_ref[g]
        m_new = jnp.maximum(m_prev, jnp.max(s, axis=0, keepdims=True).astype(F32))
        m_ref[g] = m_new
        p = jnp.exp2((s - m_new.astype(s.dtype)).astype(BF16))
        pv = _dot(vT_ref[0, g % 2, :, pl.ds(ks, tk)], p)
        acc_ref[g] = jnp.exp2(m_prev - m_new) * acc_ref[g] + pv

    def block_pair(i_next, slot_next, i_cur, slot_cur, g0, diagonal):
        for g in range(g0, n_groups):
            if i_next is not None and g >= g0 + (2 if diagonal else 0):
                scores(i_next, slot_next, g)
            softmax_pv(i_cur, slot_cur, g, diagonal and g < g0 + 2)

    n_off = qi * n_sub
    for g in range(n_groups):
        scores(0, 0, g)

    def body(j, carry):
        for u in range(n_sub):
            block_pair(n_sub * j + u + 1, (u + 1) % 2, n_sub * j + u, u % 2, 0, False)
        return carry

    lax.fori_loop(0, qi, body, 0)
    for dj in range(n_sub):
        nxt = n_off + dj + 1 if dj + 1 < n_sub else None
        block_pair(nxt, (dj + 1) % 2, n_off + dj, dj % 2, 2 * dj, True)

    def normalized(g):
        acc = acc_ref[g]
        return acc[:HEAD_DIM] * (1.0 / acc[HEAD_DIM:HEAD_DIM + 1])

    o = jnp.concatenate(
        [jnp.concatenate([normalized(2 * sub + hh) for sub in range(n_sub)], axis=1)
         for hh in range(2)], axis=0)
    o_ref[0] = o.T.astype(BF16)


def _fox(qT, qb, ka, vT, tq, tk):
    bsz, _, seq = qT.shape
    n_pair = N_MIX_HEADS // 2
    return pl.pallas_call(
        functools.partial(_fox_kernel, tq=tq, tk=tk),
        grid=(bsz, n_pair, seq // tq),
        in_specs=[
            pl.BlockSpec((1, 2 * HEAD_DIM, tq), lambda b, p, i: (b, p, i)),
            pl.BlockSpec((1, 2, BIAS_ROWS, tq), lambda b, p, i: (b, p, 0, i)),
            pl.BlockSpec((1, 1, seq, 2 * LANES), lambda b, p, i: (b, p, 0, 0)),
            pl.BlockSpec((1, 2, V_ROWS, seq), lambda b, p, i: (b, p, 0, 0)),
        ],
        out_specs=pl.BlockSpec((1, tq, LANES), lambda b, p, i: (b, i, p)),
        out_shape=jax.ShapeDtypeStruct((bsz, seq, D_MIX), BF16),
        scratch_shapes=[pltpu.VMEM((2 * tq // tk, 1, tk), F32),
                        pltpu.VMEM((2 * tq // tk, V_ROWS, tk), F32),
                        pltpu.VMEM((2, 2 * tq // tk, tk, tk), BF16)],
        compiler_params=pltpu.CompilerParams(
            dimension_semantics=("arbitrary", "arbitrary", "arbitrary"),
            vmem_limit_bytes=VMEM_LIMIT),
        name="fox_attention",
    )(qT, qb, ka, vT)


def kernel(x, mem, a_w_in, a_sgu_ln_g, a_sgu_ln_b, a_w_s, a_b_s, kv_w, kv_b_f, b_w_q,
           mem_w_kv, w_o, ln_g, ln_b, w_up, w_down):
    bsz, seq, d = x.shape
    t = bsz * seq
    scale = 1.0 / math.sqrt(HEAD_DIM)

    wkT = (jnp.swapaxes(mem_w_kv[:, :, :D_MEM], 1, 2) * scale).astype(BF16)
    wv = mem_w_kv[:, :, D_MEM:].astype(BF16)
    ws_pair = (a_w_s[0].reshape(N_MIX_HEADS // 2, 2, CHUNK, CHUNK)
               .transpose(0, 2, 1, 3).reshape(N_MIX_HEADS // 2, CHUNK, 2 * CHUNK))
    bs_full = jnp.repeat(a_b_s[0].T, HEAD_DIM, axis=1)
    wf = kv_w[:, 2 * D_MIX:]
    zf = jnp.zeros((d, 16 - N_MIX_HEADS), F32)
    wf3 = jnp.concatenate([wf, zf, wf, zf, wf, zf, jnp.zeros((d, LANES - 48), F32)], axis=1)
    zb = jnp.zeros((16 - N_MIX_HEADS,), F32)
    bf3 = jnp.concatenate([kv_b_f, zb, kv_b_f, zb, kv_b_f, zb, jnp.zeros((LANES - 48,), F32)])[None, :]
    row = lambda v: v[None, :]
    ln_g4 = ln_g.reshape(2 * DEPTH, 1, d)
    ln_b4 = ln_b.reshape(2 * DEPTH, 1, d)

    mkT, mvb = _mem_kv(mem, wkT, wv)
    x2d = x.reshape(t, d)

    tm = min(TOKEN_TILE, seq)

    mix, qm = _sgu(x2d, a_w_in, row(a_sgu_ln_g[0]), row(a_sgu_ln_b[0]), ws_pair, bs_full, tm)
    x2d = _tail(mix, qm, x2d, mkT, mvb, w_o, ln_g4, ln_b4, seq, tm, 0)
    x2d = _mlp(x2d, w_up, w_down, ln_g4, ln_b4, tm, 0)

    qT, qb, ka, vT, qm = _bproj(x2d.reshape(bsz, seq, d), b_w_q, kv_w.T, wf3, bf3, tm, scale * LOG2E)
    att = _fox(qT, qb, ka, vT, min(FOX_TQ, seq), min(FOX_TK, seq)).reshape(t, D_MIX)
    x2d = _tail(att, qm, x2d, mkT, mvb, w_o, ln_g4, ln_b4, seq, tm, 1)
    x2d = _mlp(x2d, w_up, w_down, ln_g4, ln_b4, tm, 1)
    return x2d.reshape(bsz, seq, d)
```

```python
import functools
import math

import jax
import jax.numpy as jnp
from jax import lax
from jax.experimental import pallas as pl
from jax.experimental.pallas import tpu as pltpu

D_MODEL = 1024
HEAD_DIM = 64
N_MIX_HEADS = 12
N_MEM_HEADS = 4
D_MIX = N_MIX_HEADS * HEAD_DIM
D_MEM = N_MEM_HEADS * HEAD_DIM
CHUNK = 128
N_MEM_TOKENS = 256
D_FF = 4 * D_MODEL
DEPTH = 2
DN_ALPHA = (2 * DEPTH) ** 0.25
LN_EPS = 1e-5

LANES = 128
V_ROWS = 80
BIAS_ROWS = 64
NEG_BIG = -1e30
LOG2E = math.log2(math.e)

F32 = jnp.float32
BF16 = jnp.bfloat16

VMEM_LIMIT = 56 * 1024 * 1024
TOKEN_TILE = 1024
MLP_ROWS = 256
MLP_SLAB = 1024
FOX_TQ = 2048
FOX_TK = 256


def _const_spec(shape):
    nd = len(shape)
    return pl.BlockSpec(shape, lambda *_: (0,) * nd, pipeline_mode=pl.Buffered(1))


def _layer_spec(shape, layer):
    nd = len(shape)
    return pl.BlockSpec((None,) + tuple(shape), lambda *_: (layer,) + (0,) * nd,
                        pipeline_mode=pl.Buffered(1))


def _dot(a, b):
    return jnp.dot(a, b, preferred_element_type=F32)


def _dot_nt(a, b):
    return lax.dot_general(a, b, (((1,), (1,)), ((), ())), preferred_element_type=F32)


def _layer_norm(r, g, b):
    mu = jnp.mean(r, axis=-1, keepdims=True)
    d = r - mu
    var = jnp.mean(d * d, axis=-1, keepdims=True)
    return d * lax.rsqrt(var + LN_EPS) * g + b


def _split3(c):
    hi = c.astype(BF16).astype(F32)
    r1 = c - hi
    mid = r1.astype(BF16).astype(F32)
    lo = (r1 - mid).astype(BF16).astype(F32)
    return hi, mid, lo


def _mem_kv_kernel(mem_ref, wkT_ref, wv_ref, mkT_ref, mvb_ref):
    memb = mem_ref[0].astype(BF16)
    kT = _dot_nt(wkT_ref[0], memb)
    kT4 = jnp.concatenate([kT] * N_MEM_HEADS, axis=1)
    row = lax.broadcasted_iota(jnp.int32, kT4.shape, 0)
    col = lax.broadcasted_iota(jnp.int32, kT4.shape, 1)
    same = (row >> 6) == (col >> 8)
    mkT_ref[0, 0] = jnp.where(same, kT4, 0.0).astype(BF16)
    v = _dot(memb, wv_ref[0])
    v4 = jnp.concatenate([v] * N_MEM_HEADS, axis=0)
    row = lax.broadcasted_iota(jnp.int32, v4.shape, 0)
    col = lax.broadcasted_iota(jnp.int32, v4.shape, 1)
    same = (row >> 8) == (col >> 6)
    mvb_ref[0, 0] = jnp.where(same, v4, 0.0).astype(BF16)


def _mem_kv(mem, wkT, wv):
    bsz = mem.shape[0]
    m = N_MEM_TOKENS
    return pl.pallas_call(
        _mem_kv_kernel,
        grid=(DEPTH, bsz),
        in_specs=[
            pl.BlockSpec((1, m, D_MODEL), lambda l, b: (b, 0, 0)),
            pl.BlockSpec((1, D_MEM, D_MODEL), lambda l, b: (l, 0, 0)),
            pl.BlockSpec((1, D_MODEL, D_MEM), lambda l, b: (l, 0, 0)),
        ],
        out_specs=[
            pl.BlockSpec((1, 1, D_MEM, N_MEM_HEADS * m), lambda l, b: (l, b, 0, 0)),
            pl.BlockSpec((1, 1, N_MEM_HEADS * m, D_MEM), lambda l, b: (l, b, 0, 0)),
        ],
        out_shape=[
            jax.ShapeDtypeStruct((DEPTH, bsz, D_MEM, N_MEM_HEADS * m), BF16),
            jax.ShapeDtypeStruct((DEPTH, bsz, N_MEM_HEADS * m, D_MEM), BF16),
        ],
        name="mem_kv",
    )(mem, wkT, wv)


def _sgu_kernel(x_ref, win32_ref, g_ref, b_ref, ws_ref, bs_ref, mix_ref, qm_ref, win_ref, *, tm):
    @pl.when(pl.program_id(0) == 0)
    def _():
        win_ref[...] = win32_ref[...].astype(BF16)

    lane = lax.broadcasted_iota(jnp.int32, (CHUNK, LANES), 1)
    low = lane < HEAD_DIM
    wrow = lax.broadcasted_iota(jnp.int32, (CHUNK, 2 * CHUNK), 0)
    wcol = lax.broadcasted_iota(jnp.int32, (CHUNK, 2 * CHUNK), 1) & (CHUNK - 1)
    causal = wcol <= wrow
    xb = x_ref[...].astype(BF16)
    z = _dot(xb, win_ref[...])
    zu = jax.nn.gelu(z[:, :D_MIX], approximate=True)
    zv = jax.nn.gelu(z[:, D_MIX:2 * D_MIX], approximate=True)
    zv = _layer_norm(zv, g_ref[...], b_ref[...])
    qm_ref[...] = z[:, 2 * D_MIX:].astype(BF16)
    for p in range(N_MIX_HEADS // 2):
        w = jnp.where(causal, ws_ref[p], 0.0).astype(BF16)
        bias = bs_ref[:, p * LANES:(p + 1) * LANES]
        for c0 in range(0, tm // CHUNK, 2):
            rhs = []
            for c in (c0, c0 + 1):
                slab = zv[c * CHUNK:(c + 1) * CHUNK, p * LANES:(p + 1) * LANES]
                rhs.append(jnp.concatenate(
                    [jnp.where(low, slab, 0.0), jnp.where(low, 0.0, slab)], axis=0))
            rhs = jnp.concatenate(rhs, axis=1).astype(BF16)
            mixed = _dot(w, rhs)
            for i, c in enumerate((c0, c0 + 1)):
                u = zu[c * CHUNK:(c + 1) * CHUNK, p * LANES:(p + 1) * LANES]
                out = u * (mixed[:, i * LANES:(i + 1) * LANES] + bias)
                mix_ref[c * CHUNK:(c + 1) * CHUNK, p * LANES:(p + 1) * LANES] = out.astype(BF16)


def _sgu(x2d, w_in, ln_g, ln_b, ws_pair, bs_full, tm):
    t = x2d.shape[0]
    n_in = w_in.shape[-1]
    return pl.pallas_call(
        functools.partial(_sgu_kernel, tm=tm),
        grid=(t // tm,),
        in_specs=[
            pl.BlockSpec((tm, D_MODEL), lambda i: (i, 0)),
            _layer_spec((D_MODEL, n_in), 0),
            _const_spec((1, D_MIX)),
            _const_spec((1, D_MIX)),
            _const_spec((N_MIX_HEADS // 2, CHUNK, 2 * CHUNK)),
            _const_spec((CHUNK, D_MIX)),
        ],
        out_specs=[pl.BlockSpec((tm, D_MIX), lambda i: (i, 0)),
                   pl.BlockSpec((tm, D_MEM), lambda i: (i, 0))],
        out_shape=[jax.ShapeDtypeStruct((t, D_MIX), BF16),
                   jax.ShapeDtypeStruct((t, D_MEM), BF16)],
        scratch_shapes=[pltpu.VMEM((D_MODEL, n_in), BF16)],
        compiler_params=pltpu.CompilerParams(
            dimension_semantics=("arbitrary",), vmem_limit_bytes=VMEM_LIMIT),
        name="sgu_mixer",
    )(x2d, w_in, ln_g, ln_b, ws_pair, bs_full)


def _tail_kernel(mix_ref, qm_ref, x_ref, mkT_ref, mvb_ref, wo32_ref, g_ref, b_ref, o_ref, wo_ref):
    @pl.when(pl.program_id(0) == 0)
    def _():
        wo_ref[...] = wo32_ref[...].astype(BF16)

    m = N_MEM_TOKENS
    s = _dot(qm_ref[...], mkT_ref[0])
    ps = []
    for h in range(N_MEM_HEADS):
        sh = s[:, h * m:(h + 1) * m]
        e = jnp.exp(sh - jnp.max(sh, axis=-1, keepdims=True))
        ps.append((e / jnp.sum(e, axis=-1, keepdims=True)).astype(BF16))
    p = jnp.concatenate(ps, axis=1)
    mo = _dot(p, mvb_ref[0]).astype(BF16)
    y = _dot(mix_ref[...], wo_ref[:D_MIX, :]) + _dot(mo, wo_ref[D_MIX:, :])
    r = DN_ALPHA * x_ref[...] + y
    o_ref[...] = _layer_norm(r, g_ref[...], b_ref[...])


def _tail(mix, qm, x2d, mkT, mvb, w_o, ln_g, ln_b, seq, tm, layer):
    t = x2d.shape[0]
    m = N_MEM_TOKENS
    per_b = seq // tm
    return pl.pallas_call(
        _tail_kernel,
        grid=(t // tm,),
        in_specs=[
            pl.BlockSpec((tm, D_MIX), lambda i: (i, 0)),
            pl.BlockSpec((tm, D_MEM), lambda i: (i, 0)),
            pl.BlockSpec((tm, D_MODEL), lambda i: (i, 0)),
            pl.BlockSpec((None, 1, D_MEM, N_MEM_HEADS * m), lambda i: (layer, i // per_b, 0, 0)),
            pl.BlockSpec((None, 1, N_MEM_HEADS * m, D_MEM), lambda i: (layer, i // per_b, 0, 0)),
            _layer_spec((D_MODEL, D_MODEL), layer),
            _layer_spec((1, D_MODEL), 2 * layer),
            _layer_spec((1, D_MODEL), 2 * layer),
        ],
        out_specs=pl.BlockSpec((tm, D_MODEL), lambda i: (i, 0)),
        out_shape=jax.ShapeDtypeStruct((t, D_MODEL), F32),
        scratch_shapes=[pltpu.VMEM((D_MODEL, D_MODEL), BF16)],
        compiler_params=pltpu.CompilerParams(
            dimension_semantics=("arbitrary",), vmem_limit_bytes=VMEM_LIMIT),
        name="mixer_tail",
    )(mix, qm, x2d, mkT, mvb, w_o, ln_g, ln_b)


def _mlp_kernel(x_ref, wu32_ref, wd32_ref, g_ref, b_ref, o_ref, wu_ref, wd_ref, h_ref,
                *, n_chunk, rows, n_w, wc):
    step = pl.program_id(0)

    @pl.when(step < n_w)
    def _():
        c0 = pl.multiple_of(step * wc, wc)
        wu_ref[:, pl.ds(c0, wc)] = wu32_ref[...].astype(BF16)
        wd_ref[pl.ds(c0, wc), :] = wd32_ref[...].astype(BF16)

    @pl.when(step >= n_w - 1)
    def _():
        for r0 in range(0, x_ref.shape[0], rows):
            rs = slice(r0, r0 + rows)
            x = x_ref[rs, :]
            xb = x.astype(BF16)
            for c in range(D_FF // n_chunk):
                h = _dot(xb, wu_ref[:, c * n_chunk:(c + 1) * n_chunk])
                h = jnp.maximum(h, 0.0)
                h_ref[rs, c * n_chunk:(c + 1) * n_chunk] = (h * h).astype(BF16)
            y = _dot(h_ref[rs, :], wd_ref[...])
            r = DN_ALPHA * x + y
            o_ref[rs, :] = _layer_norm(r, g_ref[...], b_ref[...])


def _mlp(x2d, w_up, w_down, ln_g, ln_b, tm, layer, n_chunk=1024, rows=MLP_ROWS, wc=MLP_SLAB):
    t = x2d.shape[0]
    rows = min(rows, tm)
    n_w = D_FF // wc
    tile = lambda s: (jnp.maximum(s - (n_w - 1), 0), 0)
    return pl.pallas_call(
        functools.partial(_mlp_kernel, n_chunk=n_chunk, rows=rows, n_w=n_w, wc=wc),
        grid=(n_w - 1 + t // tm,),
        in_specs=[
            pl.BlockSpec((tm, D_MODEL), tile),
            pl.BlockSpec((None, D_MODEL, wc), lambda s: (layer, 0, jnp.minimum(s, n_w - 1))),
            pl.BlockSpec((None, wc, D_MODEL), lambda s: (layer, jnp.minimum(s, n_w - 1), 0)),
            _layer_spec((1, D_MODEL), 2 * layer + 1),
            _layer_spec((1, D_MODEL), 2 * layer + 1),
        ],
        out_specs=pl.BlockSpec((tm, D_MODEL), tile),
        out_shape=jax.ShapeDtypeStruct((t, D_MODEL), F32),
        scratch_shapes=[pltpu.VMEM((D_MODEL, D_FF), BF16),
                        pltpu.VMEM((D_FF, D_MODEL), BF16),
                        pltpu.VMEM((tm, D_FF), BF16)],
        compiler_params=pltpu.CompilerParams(
            dimension_semantics=("arbitrary",), vmem_limit_bytes=VMEM_LIMIT),
        name="relu2_mlp",
    )(x2d, w_up, w_down, ln_g, ln_b)


def _log_sigmoid(x):
    return jnp.minimum(x, 0.0) - jnp.log(1.0 + jnp.exp(-jnp.abs(x)))


def _bproj_kernel(x_ref, wq32_ref, kvwT32_ref, wf3_ref, bf_ref,
                  qT_ref, qb_ref, ka_ref, vT_ref, qm_ref, wnt_ref, wnn_ref, carry_ref, *, tm, q_scale):
    @pl.when((pl.program_id(0) == 0) & (pl.program_id(1) == 0))
    def _():
        wq = wq32_ref[...]
        wnt_ref[:D_MIX, :] = (wq[:, :D_MIX] * q_scale).T.astype(BF16)
        wnt_ref[D_MIX:, :] = kvwT32_ref[D_MIX:2 * D_MIX, :].astype(BF16)
        wnn_ref[:, :D_MIX] = kvwT32_ref[:D_MIX, :].T.astype(BF16)
        wnn_ref[:, D_MIX:D_MIX + D_MEM] = wq[:, D_MIX:].astype(BF16)
        wnn_ref[:, D_MIX + D_MEM:] = wf3_ref[...].astype(BF16)

    @pl.when(pl.program_id(1) == 0)
    def _():
        carry_ref[...] = jnp.zeros_like(carry_ref)

    xb = x_ref[...].astype(BF16)
    nt = _dot_nt(wnt_ref[...], xb)
    qT_ref[0] = nt[:D_MIX].astype(BF16)
    vrow = lax.broadcasted_iota(jnp.int32, (V_ROWS - HEAD_DIM, tm), 0)
    ones_rows = jnp.where(vrow == 0, 1.0, 0.0).astype(BF16)
    for h in range(N_MIX_HEADS):
        vT_ref[0, h, :HEAD_DIM, :] = nt[D_MIX + h * HEAD_DIM:D_MIX + (h + 1) * HEAD_DIM].astype(BF16)
        vT_ref[0, h, HEAD_DIM:, :] = ones_rows

    nn = _dot(xb, wnn_ref[...])
    qm_ref[...] = nn[:, D_MIX:D_MIX + D_MEM].astype(BF16)

    f3 = nn[:, D_MIX + D_MEM:] + bf_ref[...]
    lane = lax.broadcasted_iota(jnp.int32, (tm, LANES), 1)
    valid = (lane & 15) < N_MIX_HEADS
    valid = valid & (lane < 48)
    c = jnp.where(valid, _log_sigmoid(f3) * LOG2E, 0.0)
    rowi = lax.broadcasted_iota(jnp.int32, (tm, LANES), 0)
    d = 1
    while d < tm:
        c = c + jnp.where(rowi >= d, pltpu.roll(c, d, axis=0), 0.0)
        d *= 2
    c = c + carry_ref[...]
    carry_ref[...] = c[tm - 1:tm, :]

    hi, mid, lo = _split3(c)
    kb = jnp.where(lane < 16, -hi,
                   jnp.where(lane < 32, -mid,
                             jnp.where(lane < 48, -lo,
                                       jnp.where(lane < 51, 1.0, 0.0))))
    kb = kb.astype(BF16)
    for p in range(N_MIX_HEADS // 2):
        ka_ref[0, p, :, :LANES] = nn[:, p * LANES:(p + 1) * LANES].astype(BF16)
        ka_ref[0, p, :, LANES:] = kb

    cT = c.T
    brow = lax.broadcasted_iota(jnp.int32, (BIAS_ROWS, tm), 0)
    for h in range(N_MIX_HEADS):
        chi, cmid, clo = _split3(cT[h:h + 1, :])
        sel = (brow == h) | (brow == 16 + h) | (brow == 32 + h)
        tile = jnp.where(brow == 48, chi,
                         jnp.where(brow == 49, cmid,
                                   jnp.where(brow == 50, clo,
                                             jnp.where(sel, 1.0, 0.0))))
        qb_ref[0, h] = tile.astype(BF16)


def _bproj(x3d, b_w_q, kv_wT, wf3, bf3, tm, q_scale):
    bsz, seq, _ = x3d.shape
    n_nn = D_MIX + D_MEM + LANES
    return pl.pallas_call(
        functools.partial(_bproj_kernel, tm=tm, q_scale=q_scale),
        grid=(bsz, seq // tm),
        in_specs=[
            pl.BlockSpec((None, tm, D_MODEL), lambda b, j: (b, j, 0)),
            _layer_spec((D_MODEL, D_MODEL), 0),
            _const_spec(kv_wT.shape),
            _const_spec((D_MODEL, LANES)),
            _const_spec((1, LANES)),
        ],
        out_specs=[
            pl.BlockSpec((1, D_MIX, tm), lambda b, j: (b, 0, j)),
            pl.BlockSpec((1, N_MIX_HEADS, BIAS_ROWS, tm), lambda b, j: (b, 0, 0, j)),
            pl.BlockSpec((1, N_MIX_HEADS // 2, tm, 2 * LANES), lambda b, j: (b, 0, j, 0)),
            pl.BlockSpec((1, N_MIX_HEADS, V_ROWS, tm), lambda b, j: (b, 0, 0, j)),
            pl.BlockSpec((tm, D_MEM), lambda b, j: (b * (seq // tm) + j, 0)),
        ],
        out_shape=[
            jax.ShapeDtypeStruct((bsz, D_MIX, seq), BF16),
            jax.ShapeDtypeStruct((bsz, N_MIX_HEADS, BIAS_ROWS, seq), BF16),
            jax.ShapeDtypeStruct((bsz, N_MIX_HEADS // 2, seq, 2 * LANES), BF16),
            jax.ShapeDtypeStruct((bsz, N_MIX_HEADS, V_ROWS, seq), BF16),
            jax.ShapeDtypeStruct((bsz * seq, D_MEM), BF16),
        ],
        scratch_shapes=[pltpu.VMEM((2 * D_MIX, D_MODEL), BF16),
                        pltpu.VMEM((D_MODEL, n_nn), BF16),
                        pltpu.VMEM((1, LANES), F32)],
        compiler_params=pltpu.CompilerParams(
            dimension_semantics=("arbitrary", "arbitrary"), vmem_limit_bytes=VMEM_LIMIT),
        name="fox_proj",
    )(x3d, b_w_q, kv_wT, wf3, bf3)


def _fox_kernel(qT_ref, qb_ref, ka_ref, vT_ref, o_ref, m_ref, acc_ref, s_ref, *, tq, tk):
    qi = pl.program_id(2)
    n_sub = tq // tk
    n_groups = 2 * n_sub

    srow = lax.broadcasted_iota(jnp.int32, (2 * HEAD_DIM, tk), 0)
    zpad = jnp.zeros((LANES - BIAS_ROWS, tk), BF16)
    qa = []
    for sub in range(n_sub):
        cols = slice(sub * tk, (sub + 1) * tk)
        sl = qT_ref[0, :, cols]
        for hh in range(2):
            keep = (srow < HEAD_DIM) if hh == 0 else (srow >= HEAD_DIM)
            qa.append(jnp.concatenate(
                [jnp.where(keep, sl, jnp.zeros_like(sl)), qb_ref[0, hh, :, cols], zpad], axis=0))

    m_ref[...] = jnp.full(m_ref.shape, NEG_BIG, F32)
    acc_ref[...] = jnp.zeros(acc_ref.shape, F32)

    assert n_sub % 2 == 0
    tri = (lax.broadcasted_iota(jnp.int32, (tk, tk), 0)
           <= lax.broadcasted_iota(jnp.int32, (tk, tk), 1))

    def scores(i, slot, g):
        ks = pl.multiple_of(i * tk, tk)
        s_ref[slot, g] = _dot(ka_ref[0, 0, pl.ds(ks, tk), :], qa[g]).astype(s_ref.dtype)

    def softmax_pv(i, slot, g, masked):
        ks = pl.multiple_of(i * tk, tk)
        s = s_ref[slot, g]
        if masked:
            s = jnp.where(tri, s, NEG_BIG)
        m_prev = m_ref[g]
        m_new = jnp.maximum(m_prev, jnp.max(s, axis=0, keepdims=True).astype(F32))
        m_ref[g] = m_new
        p = jnp.exp2((s - m_new.astype(s.dtype)).astype(BF16))
        pv = _dot(vT_ref[0, g % 2, :, pl.ds(ks, tk)], p)
        acc_ref[g] = jnp.exp2(m_prev - m_new) * acc_ref[g] + pv

    def block_pair(i_next, slot_next, i_cur, slot_cur, g0, diagonal):
        for g in range(g0, n_groups):
            if i_next is not None and g >= g0 + (2 if diagonal else 0):
                scores(i_next, slot_next, g)
            softmax_pv(i_cur, slot_cur, g, diagonal and g < g0 + 2)

    n_off = qi * n_sub
    for g in range(n_groups):
        scores(0, 0, g)

    def body(j, carry):
        for u in range(n_sub):
            block_pair(n_sub * j + u + 1, (u + 1) % 2, n_sub * j + u, u % 2, 0, False)
        return carry

    def normalized(g):
        acc = acc_ref[g]
        return acc[:HEAD_DIM] * (1.0 / acc[HEAD_DIM:HEAD_DIM + 1])

    lax.fori_loop(0, qi, body, 0)
    for dj in range(n_sub):
        nxt = n_off + dj + 1 if dj + 1 < n_sub else None
        block_pair(nxt, (dj + 1) % 2, n_off + dj, dj % 2, 2 * dj, True)
        o = jnp.concatenate([normalized(2 * dj), normalized(2 * dj + 1)], axis=0)
        o_ref[0, dj * tk:(dj + 1) * tk, :] = o.T.astype(BF16)


def _fox(qT, qb, ka, vT, tq, tk):
    bsz, _, seq = qT.shape
    n_pair = N_MIX_HEADS // 2
    return pl.pallas_call(
        functools.partial(_fox_kernel, tq=tq, tk=tk),
        grid=(bsz, n_pair, seq // tq),
        in_specs=[
            pl.BlockSpec((1, 2 * HEAD_DIM, tq), lambda b, p, i: (b, p, i)),
            pl.BlockSpec((1, 2, BIAS_ROWS, tq), lambda b, p, i: (b, p, 0, i)),
            pl.BlockSpec((1, 1, seq, 2 * LANES), lambda b, p, i: (b, p, 0, 0)),
            pl.BlockSpec((1, 2, V_ROWS, seq), lambda b, p, i: (b, p, 0, 0)),
        ],
        out_specs=pl.BlockSpec((1, tq, LANES), lambda b, p, i: (b, i, p)),
        out_shape=jax.ShapeDtypeStruct((bsz, seq, D_MIX), BF16),
        scratch_shapes=[pltpu.VMEM((2 * tq // tk, 1, tk), F32),
                        pltpu.VMEM((2 * tq // tk, V_ROWS, tk), F32),
                        pltpu.VMEM((2, 2 * tq // tk, tk, tk), BF16)],
        compiler_params=pltpu.CompilerParams(
            dimension_semantics=("arbitrary", "arbitrary", "arbitrary"),
            vmem_limit_bytes=VMEM_LIMIT),
        name="fox_attention",
    )(qT, qb, ka, vT)


def kernel(x, mem, a_w_in, a_sgu_ln_g, a_sgu_ln_b, a_w_s, a_b_s, kv_w, kv_b_f, b_w_q,
           mem_w_kv, w_o, ln_g, ln_b, w_up, w_down):
    bsz, seq, d = x.shape
    t = bsz * seq
    scale = 1.0 / math.sqrt(HEAD_DIM)

    wkT = (jnp.swapaxes(mem_w_kv[:, :, :D_MEM], 1, 2) * scale).astype(BF16)
    wv = mem_w_kv[:, :, D_MEM:].astype(BF16)
    ws_pair = (a_w_s[0].reshape(N_MIX_HEADS // 2, 2, CHUNK, CHUNK)
               .transpose(0, 2, 1, 3).reshape(N_MIX_HEADS // 2, CHUNK, 2 * CHUNK))
    bs_full = jnp.repeat(a_b_s[0].T, HEAD_DIM, axis=1)
    wf = kv_w[:, 2 * D_MIX:]
    zf = jnp.zeros((d, 16 - N_MIX_HEADS), F32)
    wf3 = jnp.concatenate([wf, zf, wf, zf, wf, zf, jnp.zeros((d, LANES - 48), F32)], axis=1)
    zb = jnp.zeros((16 - N_MIX_HEADS,), F32)
    bf3 = jnp.concatenate([kv_b_f, zb, kv_b_f, zb, kv_b_f, zb, jnp.zeros((LANES - 48,), F32)])[None, :]
    row = lambda v: v[None, :]
    ln_g4 = ln_g.reshape(2 * DEPTH, 1, d)
    ln_b4 = ln_b.reshape(2 * DEPTH, 1, d)

    mkT, mvb = _mem_kv(mem, wkT, wv)
    x2d = x.reshape(t, d)

    tm = min(TOKEN_TILE, seq)

    mix, qm = _sgu(x2d, a_w_in, row(a_sgu_ln_g[0]), row(a_sgu_ln_b[0]), ws_pair, bs_full, tm)
    x2d = _tail(mix, qm, x2d, mkT, mvb, w_o, ln_g4, ln_b4, seq, tm, 0)
    x2d = _mlp(x2d, w_up, w_down, ln_g4, ln_b4, tm, 0)

    qT, qb, ka, vT, qm = _bproj(x2d.reshape(bsz, seq, d), b_w_q, kv_w.T, wf3, bf3, tm, scale * LOG2E)
    att = _fox(qT, qb, ka, vT, min(FOX_TQ, seq), min(FOX_TK, seq)).reshape(t, D_MIX)
    x2d = _tail(att, qm, x2d, mkT, mvb, w_o, ln_g4, ln_b4, seq, tm, 1)
    x2d = _mlp(x2d, w_up, w_down, ln_g4, ln_b4, tm, 1)
    return x2d.reshape(bsz, seq, d)
```
